```python
import math
import jax, jax.numpy as jnp
from jax import lax
import numpy as np

D_MODEL = 1024
BATCH = 8
SEQ = 4096
DEPTH = 2

LRU_WIDTH = D_MODEL
LRU_BLOCKS = 4
LRU_BLOCK_W = LRU_WIDTH // LRU_BLOCKS
RG_LRU_C = 8.0
CONV_WIDTH = 4
CONV_PAD_LEFT = 2
CONV_PAD_RIGHT = CONV_WIDTH - 1 - CONV_PAD_LEFT

RET_HEADS = 4
RET_QK_DIM = D_MODEL
RET_V_DIM = 2 * D_MODEL
RET_HEAD_QK = RET_QK_DIM // RET_HEADS
RET_HEAD_V = RET_V_DIM // RET_HEADS
RET_CHUNK = 128
ROPE_BASE = 10000.0

FFN_HIDDEN = ((8 * D_MODEL // 3 + 255) // 256) * 256

NORM_EPS = 1e-6
N_LRU_LAYERS = (DEPTH + 1) // 2
N_RET_LAYERS = DEPTH // 2

kernel_name = "bidir_hybrid_rglru_retention_swiglu"


def rms_norm(x, g):
    xf = x.astype(jnp.float32)
    y = xf * lax.rsqrt(jnp.mean(xf * xf, axis=-1, keepdims=True) + NORM_EPS)
    return (y * g.astype(jnp.float32)).astype(x.dtype)


def _linear_scan_combine(left, right):
    a1, b1 = left
    a2, b2 = right
    return a1 * a2, a2 * b1 + b2


def rg_lru_mixer(h, w_in, conv_w, conv_b, ga_w, ga_b, gx_w, gx_b, lam, w_out):
    b, s, _ = h.shape
    u = h @ w_in
    xb, gb = jnp.split(u, 2, axis=-1)
    xb = lax.conv_general_dilated(
        xb, conv_w[:, None, :].astype(xb.dtype), window_strides=(1,),
        padding=[(CONV_PAD_LEFT, CONV_PAD_RIGHT)],
        dimension_numbers=("NWC", "WIO", "NWC"),
        feature_group_count=LRU_WIDTH) + conv_b
    xg = xb.reshape(b, s, LRU_BLOCKS, LRU_BLOCK_W)
    h_sum = jnp.zeros((b, s, LRU_WIDTH), jnp.float32)
    for d in range(2):
        r = jax.nn.sigmoid(
            jnp.einsum("bsnk,nkj->bsnj", xg, ga_w[d]).reshape(b, s, LRU_WIDTH) + ga_b[d])
        i = jax.nn.sigmoid(
            jnp.einsum("bsnk,nkj->bsnj", xg, gx_w[d]).reshape(b, s, LRU_WIDTH) + gx_b[d])
        log_a = (-RG_LRU_C * r.astype(jnp.float32)) * jax.nn.softplus(-lam[d].astype(jnp.float32))
        a = jnp.exp(log_a)
        inp = jnp.sqrt(-jnp.expm1(2.0 * log_a)) * (i * xb).astype(jnp.float32)
        if d == 1:
            a, inp = jnp.flip(a, axis=1), jnp.flip(inp, axis=1)
        _, hs = lax.associative_scan(_linear_scan_combine, (a, inp), axis=1)
        if d == 1:
            hs = jnp.flip(hs, axis=1)
        h_sum = h_sum + hs
    y = h_sum.astype(h.dtype) * jax.nn.gelu(gb, approximate=True)
    return y @ w_out


def apply_rotary(x):
    s, d = x.shape[1], x.shape[-1]
    half = d // 2
    inv_freq = 1.0 / (ROPE_BASE ** jnp.linspace(0.0, 1.0, half, dtype=jnp.float32))
    ang = jnp.arange(s, dtype=jnp.float32)[:, None] * inv_freq[None, :]
    cos = jnp.cos(ang)[None, :, None, :].astype(x.dtype)
    sin = jnp.sin(ang)[None, :, None, :].astype(x.dtype)
    x1, x2 = x[..., :half], x[..., half:]
    return jnp.concatenate([x1 * cos - x2 * sin, x2 * cos + x1 * sin], axis=-1)


def chunk_retention(q, k, v, log_g, strict):
    b, nh, s, dk = q.shape
    dv = v.shape[-1]
    nc = s // RET_CHUNK
    qc = q.reshape(b, nh, nc, RET_CHUNK, dk)
    kc = k.reshape(b, nh, nc, RET_CHUNK, dk)
    vc = v.reshape(b, nh, nc, RET_CHUNK, dv)
    pos = jnp.arange(RET_CHUNK, dtype=jnp.float32)
    diff = pos[:, None] - pos[None, :]
    mask = diff > 0 if strict else diff >= 0
    dec = jnp.where(mask[None], jnp.exp(jnp.where(mask, diff, 0.0)[None] * log_g[:, None, None]), 0.0)
    scores = jnp.einsum("bhnid,bhnjd->bhnij", qc, kc) * dec[None, :, None].astype(q.dtype)
    y_intra = jnp.einsum("bhnij,bhnje->bhnie", scores, vc)
    zeta = jnp.exp((RET_CHUNK - 1.0 - pos)[None, :] * log_g[:, None]).astype(q.dtype)
    xi = jnp.exp((pos + 1.0)[None, :] * log_g[:, None]).astype(q.dtype)
    g_chunk = jnp.exp(RET_CHUNK * log_g).astype(q.dtype)

    def step(state, inp):
        q_n, k_n, v_n = inp
        y = jnp.einsum("bhid,bhde->bhie", q_n, state) * xi[None, :, :, None]
        state = state * g_chunk[None, :, None, None] + jnp.einsum(
            "bhjd,bhje->bhde", k_n * zeta[None, :, :, None], v_n)
        return state, y

    state0 = jnp.zeros((b, nh, dk, dv), q.dtype)
    xs = (jnp.moveaxis(qc, 2, 0), jnp.moveaxis(kc, 2, 0), jnp.moveaxis(vc, 2, 0))
    _, ys = lax.scan(step, state0, xs)
    y = y_intra + jnp.moveaxis(ys, 0, 2).astype(y_intra.dtype)
    return y.reshape(b, nh, s, dv)


def retention_mixer(h, w_in, w_out):
    b, s, _ = h.shape
    u = h @ w_in
    q, k, v, gate = jnp.split(u, [RET_QK_DIM, 2 * RET_QK_DIM, 2 * RET_QK_DIM + RET_V_DIM], axis=-1)
    q = apply_rotary(q.reshape(b, s, RET_HEADS, RET_HEAD_QK))
    k = apply_rotary(k.reshape(b, s, RET_HEADS, RET_HEAD_QK)) * (RET_HEAD_QK ** -0.5)
    v = v.reshape(b, s, RET_HEADS, RET_HEAD_V)
    q, k, v = (jnp.transpose(t, (0, 2, 1, 3)) for t in (q, k, v))
    log_g_fwd = jnp.log1p(-jnp.exp2(-5.0 - jnp.arange(RET_HEADS, dtype=jnp.float32)))
    log_g_bwd = log_g_fwd[::-1]
    y_f = chunk_retention(q, k, v, log_g_fwd, strict=False)
    y_b = jnp.flip(chunk_retention(jnp.flip(q, 2), jnp.flip(k, 2), jnp.flip(v, 2),
                                   log_g_bwd, strict=True), 2)
    y = (y_f + y_b).astype(jnp.float32)
    y = y * lax.rsqrt(jnp.mean(y * y, axis=-1, keepdims=True) + NORM_EPS)
    y = jnp.transpose(y, (0, 2, 1, 3)).reshape(b, s, RET_V_DIM).astype(h.dtype)
    return (y * jax.nn.silu(gate)) @ w_out


def swiglu_ffn(h, w_gate, w_up, w_down):
    return (jax.nn.silu(h @ w_gate) * (h @ w_up)) @ w_down


def setup_inputs(seed: int = 0) -> dict:
    key = jax.random.key(seed)
    ks = iter(jax.random.split(key, 32))
    f32 = jnp.float32

    def nrm(shape, fan_in):
        return jax.random.normal(next(ks), shape, f32) * (fan_in ** -0.5)

    def small(shape):
        return 0.01 * jax.random.normal(next(ks), shape, f32)

    a_c = jax.random.uniform(next(ks), (N_LRU_LAYERS, 2, LRU_WIDTH), f32, 0.9, 0.999)
    sig = a_c ** (1.0 / RG_LRU_C)
    lam = jnp.log(sig) - jnp.log1p(-sig)
    return {
        "x": jax.random.normal(next(ks), (BATCH, SEQ, D_MODEL), f32),
        "ln_mix": 1.0 + small((DEPTH, D_MODEL)),
        "ln_ffn": 1.0 + small((DEPTH, D_MODEL)),
        "ln_final": 1.0 + small((D_MODEL,)),
        "lru_w_in": nrm((N_LRU_LAYERS, D_MODEL, 2 * LRU_WIDTH), D_MODEL),
        "lru_conv_w": nrm((N_LRU_LAYERS, CONV_WIDTH, LRU_WIDTH), CONV_WIDTH),
        "lru_conv_b": small((N_LRU_LAYERS, LRU_WIDTH)),
        "lru_gate_a_w": nrm((N_LRU_LAYERS, 2, LRU_BLOCKS, LRU_BLOCK_W, LRU_BLOCK_W), LRU_BLOCK_W),
        "lru_gate_a_b": small((N_LRU_LAYERS, 2, LRU_WIDTH)),
        "lru_gate_x_w": nrm((N_LRU_LAYERS, 2, LRU_BLOCKS, LRU_BLOCK_W, LRU_BLOCK_W), LRU_BLOCK_W),
        "lru_gate_x_b": small((N_LRU_LAYERS, 2, LRU_WIDTH)),
        "lru_lambda": lam,
        "lru_w_out": nrm((N_LRU_LAYERS, LRU_WIDTH, D_MODEL), LRU_WIDTH),
        "ret_w_in": nrm((N_RET_LAYERS, D_MODEL, 2 * RET_QK_DIM + 2 * RET_V_DIM), D_MODEL),
        "ret_w_out": nrm((N_RET_LAYERS, RET_V_DIM, D_MODEL), RET_V_DIM),
        "ffn_w_gate": nrm((DEPTH, D_MODEL, FFN_HIDDEN), D_MODEL),
        "ffn_w_up": nrm((DEPTH, D_MODEL, FFN_HIDDEN), D_MODEL),
        "ffn_w_down": nrm((DEPTH, FFN_HIDDEN, D_MODEL), FFN_HIDDEN),
    }


def reference(x, ln_mix, ln_ffn, ln_final, lru_w_in, lru_conv_w, lru_conv_b,
              lru_gate_a_w, lru_gate_a_b, lru_gate_x_w, lru_gate_x_b, lru_lambda,
              lru_w_out, ret_w_in, ret_w_out, ffn_w_gate, ffn_w_up, ffn_w_down):
    h = x
    for layer in range(DEPTH):
        j = layer // 2
        hn = rms_norm(h, ln_mix[layer])
        if layer % 2 == 0:
            mix = rg_lru_mixer(hn, lru_w_in[j], lru_conv_w[j], lru_conv_b[j],
                               lru_gate_a_w[j], lru_gate_a_b[j], lru_gate_x_w[j],
                               lru_gate_x_b[j], lru_lambda[j], lru_w_out[j])
        else:
            mix = retention_mixer(hn, ret_w_in[j], ret_w_out[j])
        h = h + mix.astype(h.dtype)
        h = h + swiglu_ffn(rms_norm(h, ln_ffn[layer]), ffn_w_gate[layer],
                           ffn_w_up[layer], ffn_w_down[layer]).astype(h.dtype)
    return rms_norm(h, ln_final)
```

```python
import functools
import math

import jax
import jax.numpy as jnp
from jax import lax
from jax.experimental import pallas as pl
from jax.experimental.pallas import tpu as pltpu

F32 = jnp.float32
BF16 = jnp.bfloat16

NORM_EPS = 1e-6
RG_LRU_C = 8.0
LRU_BLOCKS = 4
CONV_WIDTH = 4
CONV_PAD_LEFT = 2
RET_HEADS = 4
ROPE_BASE = 10000.0

SUBLANES = 8
VMEM_LIMIT_BYTES = 56 * 1024 * 1024
ROW_TILE = 512
RET_CHUNK = 256


def _cparams(*sem):
    return pltpu.CompilerParams(dimension_semantics=sem,
                                vmem_limit_bytes=VMEM_LIMIT_BYTES)


def _const_spec(shape):
    nd = len(shape)
    return pl.BlockSpec(shape, lambda *_: (0,) * nd, pipeline_mode=pl.Buffered(1))


def _rms(x, g):
    ms = jnp.mean(x * x, axis=-1, keepdims=True)
    return x * lax.rsqrt(ms + NORM_EPS) * g


def _sigmoid(x):
    return 0.5 * jnp.tanh(0.5 * x) + 0.5


def _gelu_tanh(x):
    c = math.sqrt(2.0 / math.pi)
    return 0.5 * x * (1.0 + jnp.tanh(c * (x + 0.044715 * (x * x * x))))


def _softplus(x):
    return jnp.maximum(x, 0.0) + jnp.log(1.0 + jnp.exp(-jnp.abs(x)))


def _lru_in_kernel(x_ref, g_ref, w_ref, xb_ref, gact_ref, hn_scr):
    width = xb_ref.shape[1]
    hn_scr[...] = _rms(x_ref[...], g_ref[...]).astype(BF16)
    xb_ref[...] = jnp.dot(hn_scr[...], w_ref[:, :width], preferred_element_type=F32)
    gb = jnp.dot(hn_scr[...], w_ref[:, width:], preferred_element_type=F32)
    gact_ref[...] = _gelu_tanh(gb)


def _lru_in(x_tm, g, w_in):
    rows, d = x_tm.shape
    width = w_in.shape[1] // 2
    r = ROW_TILE
    return pl.pallas_call(
        _lru_in_kernel,
        grid=(rows // r,),
        in_specs=[pl.BlockSpec((r, d), lambda i: (i, 0)),
                  _const_spec((1, d)),
                  _const_spec(w_in.shape)],
        out_specs=[pl.BlockSpec((r, width), lambda i: (i, 0)),
                   pl.BlockSpec((r, width), lambda i: (i, 0))],
        out_shape=[jax.ShapeDtypeStruct((rows, width), F32),
                   jax.ShapeDtypeStruct((rows, width), F32)],
        scratch_shapes=[pltpu.VMEM((r, d), BF16)],
        compiler_params=_cparams("parallel"),
        name="lru_in",
    )(x_tm, g, w_in)


def _lru_gates(xbc_ref, wg_ref, ba_ref, bx_ref, lam_ref, a_scr, inp_scr, sub_rows):
    rows, width = xbc_ref.shape
    bw = width // LRU_BLOCKS
    coef = -RG_LRU_C * _softplus(-lam_ref[...])

    def body(s, carry):
        r0 = pl.multiple_of(s * sub_rows, sub_rows)
        for n in range(LRU_BLOCKS):
            cs = slice(n * bw, (n + 1) * bw)
            xg = xbc_ref[pl.ds(r0, sub_rows), cs]
            z = jnp.dot(xg.astype(BF16), wg_ref[n], preferred_element_type=F32)
            rg = _sigmoid(z[:, :bw] + ba_ref[:, cs])
            ig = _sigmoid(z[:, bw:] + bx_ref[:, cs])
            a = jnp.exp(coef[:, cs] * rg)
            inp = jnp.sqrt(1.0 - a * a) * (ig * xg)
            a_scr[pl.ds(r0, sub_rows), cs] = a
            inp_scr[pl.ds(r0, sub_rows), cs] = inp
        return carry

    lax.fori_loop(0, rows // sub_rows, body, 0)


def _lru_fwd_kernel(prev_ref, cur_ref, next_ref, cw_ref, cb_ref, wg_ref, ba_ref,
                    bx_ref, lam_ref, xbc_ref, hf_ref, ext_scr, a_scr, inp_scr,
                    carry_scr, *, batch):
    i = pl.program_id(0)
    nt = pl.num_programs(0)
    rows, width = cur_ref.shape
    lo = CONV_PAD_LEFT * batch
    hi = (CONV_WIDTH - 1 - CONV_PAD_LEFT) * batch

    @pl.when(i == 0)
    def _():
        carry_scr[...] = jnp.zeros_like(carry_scr)

    ext_scr[pl.ds(0, lo), :] = jnp.where(i > 0, prev_ref[...], 0.0)
    ext_scr[pl.ds(lo, rows), :] = cur_ref[...]
    ext_scr[pl.ds(lo + rows, hi), :] = jnp.where(i < nt - 1, next_ref[...], 0.0)

    acc = cb_ref[...] + cw_ref[pl.ds(0, 1), :] * ext_scr[pl.ds(0, rows), :]
    for k in range(1, CONV_WIDTH):
        acc = acc + cw_ref[pl.ds(k, 1), :] * ext_scr[pl.ds(k * batch, rows), :]
    xbc_ref[...] = acc

    _lru_gates(xbc_ref, wg_ref, ba_ref, bx_ref, lam_ref, a_scr, inp_scr, 256)

    def step(t, h):
        r0 = pl.multiple_of(t * batch, batch)
        h = a_scr[pl.ds(r0, batch), :] * h + inp_scr[pl.ds(r0, batch), :]
        hf_ref[pl.ds(r0, batch), :] = h
        return h

    carry_scr[...] = lax.fori_loop(0, rows // batch, step, carry_scr[...], unroll=8)


def _lru_fwd(xb, conv_w, conv_b, wg, ba, bx, lam, batch):
    rows, width = xb.shape
    r = ROW_TILE
    nt = rows // r
    lo = CONV_PAD_LEFT * batch
    hi = (CONV_WIDTH - 1 - CONV_PAD_LEFT) * batch
    kern = functools.partial(_lru_fwd_kernel, batch=batch)
    return pl.pallas_call(
        kern,
        grid=(nt,),
        in_specs=[
            pl.BlockSpec((lo, width), lambda i: (jnp.maximum(i * (r // lo) - 1, 0), 0)),
            pl.BlockSpec((r, width), lambda i: (i, 0)),
            pl.BlockSpec((hi, width),
                         lambda i: (jnp.minimum((i + 1) * (r // hi), rows // hi - 1), 0)),
            _const_spec(conv_w.shape), _const_spec(conv_b.shape),
            _const_spec(wg.shape), _const_spec(ba.shape), _const_spec(bx.shape),
            _const_spec(lam.shape),
        ],
        out_specs=[pl.BlockSpec((r, width), lambda i: (i, 0)),
                   pl.BlockSpec((r, width), lambda i: (i, 0))],
        out_shape=[jax.ShapeDtypeStruct((rows, width), F32),
                   jax.ShapeDtypeStruct((rows, width), F32)],
        scratch_shapes=[pltpu.VMEM((lo + r + hi, width), F32),
                        pltpu.VMEM((r, width), F32),
                        pltpu.VMEM((r, width), F32),
                        pltpu.VMEM((batch, width), F32)],
        compiler_params=_cparams("arbitrary"),
        name="lru_fwd",
    )(xb, xb, xb, conv_w, conv_b, wg, ba, bx, lam)


def _lru_bwd_kernel(xbc_ref, hf_ref, gact_ref, x_ref, wg_ref, ba_ref, bx_ref,
                    lam_ref, wo_ref, o_ref, a_scr, inp_scr, carry_scr, *, batch):
    i = pl.program_id(0)
    rows, width = xbc_ref.shape
    nsteps = rows // batch

    @pl.when(i == 0)
    def _():
        carry_scr[...] = jnp.zeros_like(carry_scr)

    _lru_gates(xbc_ref, wg_ref, ba_ref, bx_ref, lam_ref, a_scr, inp_scr, 256)

    def step(s, h):
        r0 = pl.multiple_of((nsteps - 1 - s) * batch, batch)
        h = a_scr[pl.ds(r0, batch), :] * h + inp_scr[pl.ds(r0, batch), :]
        inp_scr[pl.ds(r0, batch), :] = (h + hf_ref[pl.ds(r0, batch), :]) * gact_ref[pl.ds(r0, batch), :]
        return h

    carry_scr[...] = lax.fori_loop(0, nsteps, step, carry_scr[...], unroll=8)
    o_ref[...] = x_ref[...] + jnp.dot(inp_scr[...].astype(BF16), wo_ref[...],
                                      preferred_element_type=F32)


def _lru_bwd(xbc, hf, gact, x_tm, wg, ba, bx, lam, w_out, batch):
    rows, width = xbc.shape
    d = x_tm.shape[1]
    r = ROW_TILE
    nt = rows // r
    rev = lambda i: (nt - 1 - i, 0)
    kern = functools.partial(_lru_bwd_kernel, batch=batch)
    return pl.pallas_call(
        kern,
        grid=(nt,),
        in_specs=[pl.BlockSpec((r, width), rev), pl.BlockSpec((r, width), rev),
                  pl.BlockSpec((r, width), rev), pl.BlockSpec((r, d), rev),
                  _const_spec(wg.shape), _const_spec(ba.shape), _const_spec(bx.shape),
                  _const_spec(lam.shape), _const_spec(w_out.shape)],
        out_specs=pl.BlockSpec((r, d), rev),
        out_shape=jax.ShapeDtypeStruct((rows, d), F32),
        scratch_shapes=[pltpu.VMEM((r, width), F32),
                        pltpu.VMEM((r, width), F32),
                        pltpu.VMEM((batch, width), F32)],
        compiler_params=_cparams("arbitrary"),
        name="lru_bwd",
    )(xbc, hf, gact, x_tm, wg, ba, bx, lam, w_out)


def _ffn_chunks(hidden):
    chunks, c = [], 0
    while c < hidden:
        size = 512 if hidden - c >= 512 else hidden - c
        chunks.append((c, size))
        c += size
    return chunks


def _ffn_kernel(*refs, final_norm):
    if final_norm:
        h_ref, g_ref, wg_ref, wu_ref, wd_ref, gf_ref, o_ref, hn_scr, act_scr = refs
    else:
        h_ref, g_ref, wg_ref, wu_ref, wd_ref, o_ref, hn_scr, act_scr = refs
    hn_scr[...] = _rms(h_ref[...], g_ref[...]).astype(BF16)
    for c, size in _ffn_chunks(wg_ref.shape[1]):
        gt = jnp.dot(hn_scr[...], wg_ref[:, c:c + size], preferred_element_type=F32)
        up = jnp.dot(hn_scr[...], wu_ref[:, c:c + size], preferred_element_type=F32)
        act_scr[:, c:c + size] = (gt * _sigmoid(gt) * up).astype(BF16)
    y = h_ref[...] + jnp.dot(act_scr[...], wd_ref[...], preferred_element_type=F32)
    if final_norm:
        y = _rms(y, gf_ref[...])
    o_ref[...] = y


def _ffn(h, g, w_gate, w_up, w_down, g_final=None):
    rows, d = h.shape
    hidden = w_gate.shape[1]
    r = ROW_TILE
    final_norm = g_final is not None
    in_specs = [pl.BlockSpec((r, d), lambda i: (i, 0)), _const_spec((1, d)),
                _const_spec(w_gate.shape), _const_spec(w_up.shape),
                _const_spec(w_down.shape)]
    args = [h, g, w_gate, w_up, w_down]
    if final_norm:
        in_specs.append(_const_spec((1, d)))
        args.append(g_final)
    return pl.pallas_call(
        functools.partial(_ffn_kernel, final_norm=final_norm),
        grid=(rows // r,),
        in_specs=in_specs,
        out_specs=pl.BlockSpec((r, d), lambda i: (i, 0)),
        out_shape=jax.ShapeDtypeStruct((rows, d), F32),
        scratch_shapes=[pltpu.VMEM((r, d), BF16), pltpu.VMEM((r, hidden), BF16)],
        compiler_params=_cparams("parallel"),
        name="ffn_final" if final_norm else "ffn",
    )(*args)


def _ret_in_kernel(h_ref, g_ref, w_ref, cos_ref, sin_ref, q_ref, k_ref, v_ref,
                   sg_ref, hn_scr, *, heads):
    qk = q_ref.shape[1]
    vd = v_ref.shape[1]
    hd = qk // heads
    half = hd // 2
    hn_scr[...] = _rms(h_ref[...], g_ref[...]).astype(BF16)
    cos = cos_ref[...]
    sin = sin_ref[...]
    k_scale = hd ** -0.5
    for h in range(heads):
        for which, dst, scale in ((0, q_ref, 1.0), (1, k_ref, k_scale)):
            c0 = which * qk + h * hd
            u = jnp.dot(hn_scr[...], w_ref[:, c0:c0 + hd], preferred_element_type=F32)
            x1, x2 = u[:, :half], u[:, half:]
            r1 = x1 * cos - x2 * sin
            r2 = x2 * cos + x1 * sin
            if scale != 1.0:
                r1, r2 = r1 * scale, r2 * scale
            dst[:, h * hd:h * hd + half] = r1.astype(dst.dtype)
            dst[:, h * hd + half:(h + 1) * hd] = r2.astype(dst.dtype)
    step = 512
    for c in range(0, vd, step):
        v_ref[:, c:c + step] = jnp.dot(
            hn_scr[...], w_ref[:, 2 * qk + c:2 * qk + c + step],
            preferred_element_type=F32).astype(v_ref.dtype)
    for c in range(0, vd, step):
        gt = jnp.dot(hn_scr[...], w_ref[:, 2 * qk + vd + c:2 * qk + vd + c + step],
                     preferred_element_type=F32)
        sg_ref[:, c:c + step] = gt * _sigmoid(gt)


def _ret_in(h, g, w_in, cos, sin, seq, heads, qk, vd):
    rows, d = h.shape
    r = ROW_TILE
    half = cos.shape[1]
    per_seq = seq // r
    pos = lambda i: (i % per_seq, 0)
    row = lambda i: (i, 0)
    return pl.pallas_call(
        functools.partial(_ret_in_kernel, heads=heads),
        grid=(rows // r,),
        in_specs=[pl.BlockSpec((r, d), row), _const_spec((1, d)),
                  _const_spec(w_in.shape),
                  pl.BlockSpec((r, half), pos), pl.BlockSpec((r, half), pos)],
        out_specs=[pl.BlockSpec((r, qk), row), pl.BlockSpec((r, qk), row),
                   pl.BlockSpec((r, vd), row), pl.BlockSpec((r, vd), row)],
        out_shape=[jax.ShapeDtypeStruct((rows, qk), BF16),
                   jax.ShapeDtypeStruct((rows, qk), BF16),
                   jax.ShapeDtypeStruct((rows, vd), BF16),
                   jax.ShapeDtypeStruct((rows, vd), F32)],
        scratch_shapes=[pltpu.VMEM((r, d), BF16)],
        compiler_params=_cparams("parallel"),
        name="ret_in",
    )(h, g, w_in, cos, sin)


def _ret_core_kernel(q_ref, k_ref, v_ref, sg_ref, dec_ref, zf_ref, zb_ref, xf_ref,
                     xb_ref, gc_ref, o_ref, sb_scr, st_scr):
    seq, dk = q_ref.shape
    dv = v_ref.shape[1]
    c = dec_ref.shape[0]
    nc = seq // c
    gcf = gc_ref[pl.ds(0, 1), :]
    gcb = gc_ref[pl.ds(1, 1), :]

    def kv_update(state, r0, z_ref, gcd):
        kz = (k_ref[pl.ds(r0, c), :].astype(F32) * z_ref[...]).astype(BF16)
        upd = lax.dot_general(kz, v_ref[pl.ds(r0, c), :], (((0,), (0,)), ((), ())),
                              preferred_element_type=F32)
        return state * gcd + upd

    st_scr[...] = jnp.zeros_like(st_scr)

    def bwd_body(s, carry):
        n = nc - 1 - s
        r0 = pl.multiple_of(n * c, c)
        sb_scr[n] = st_scr[...].astype(BF16)
        st_scr[...] = kv_update(st_scr[...], r0, zb_ref, gcb)
        return carry

    lax.fori_loop(0, nc, bwd_body, 0)

    st_scr[...] = jnp.zeros_like(st_scr)

    def fwd_body(n, carry):
        r0 = pl.multiple_of(n * c, c)
        q = q_ref[pl.ds(r0, c), :]
        scores = lax.dot_general(q, k_ref[pl.ds(r0, c), :], (((1,), (1,)), ((), ())),
                                 preferred_element_type=F32)
        p = (scores * dec_ref[...]).astype(BF16)
        y = jnp.dot(p, v_ref[pl.ds(r0, c), :], preferred_element_type=F32)
        y = y + xf_ref[...] * jnp.dot(q, st_scr[...].astype(BF16),
                                      preferred_element_type=F32)
        y = y + xb_ref[...] * jnp.dot(q, sb_scr[n], preferred_element_type=F32)
        st_scr[...] = kv_update(st_scr[...], r0, zf_ref, gcf)
        y = y * lax.rsqrt(jnp.mean(y * y, axis=-1, keepdims=True) + NORM_EPS)
        o_ref[pl.ds(r0, c), :] = (y * sg_ref[pl.ds(r0, c), :]).astype(o_ref.dtype)
        return carry

    lax.fori_loop(0, nc, fwd_body, 0)


def _ret_core(q, k, v, sg, tabs, heads):
    b, seq, qk = q.shape
    vd = v.shape[2]
    dk, dv = qk // heads, vd // heads
    c = RET_CHUNK
    dec, zf, zb, xf, xb, gc = tabs
    bh = lambda i, j: (i, 0, j)
    hh = lambda i, j: (j, 0, 0)
    return pl.pallas_call(
        _ret_core_kernel,
        grid=(b, heads),
        in_specs=[pl.BlockSpec((None, seq, dk), bh), pl.BlockSpec((None, seq, dk), bh),
                  pl.BlockSpec((None, seq, dv), bh), pl.BlockSpec((None, seq, dv), bh),
                  pl.BlockSpec((None, c, c), hh),
                  pl.BlockSpec((None, c, dk), hh), pl.BlockSpec((None, c, dk), hh),
                  pl.BlockSpec((None, c, dv), hh), pl.BlockSpec((None, c, dv), hh),
                  pl.BlockSpec((None, SUBLANES, dv), hh)],
        out_specs=pl.BlockSpec((None, seq, dv), bh),
        out_shape=jax.ShapeDtypeStruct((b, seq, vd), BF16),
        scratch_shapes=[pltpu.VMEM((seq // c, dk, dv), BF16),
                        pltpu.VMEM((dk, dv), F32)],
        compiler_params=_cparams("parallel", "parallel"),
        name="ret_core",
    )(q, k, v, sg, dec, zf, zb, xf, xb, gc)


def _ret_tables(heads, dk, dv):
    c = RET_CHUNK
    log_gf = jnp.log1p(-jnp.exp2(-5.0 - jnp.arange(heads, dtype=F32)))
    log_gb = log_gf[::-1]
    pos = jnp.arange(c, dtype=F32)
    diff = pos[:, None] - pos[None, :]
    lower = diff >= 0
    upper = diff < 0
    dec_f = jnp.where(lower[None], jnp.exp(jnp.where(lower, diff, 0.0)[None]
                                           * log_gf[:, None, None]), 0.0)
    dec_b = jnp.where(upper[None], jnp.exp(jnp.where(upper, -diff, 0.0)[None]
                                           * log_gb[:, None, None]), 0.0)
    dec = dec_f + dec_b
    zeta_f = jnp.exp((c - 1.0 - pos)[None, :] * log_gf[:, None])
    xi_f = jnp.exp((pos + 1.0)[None, :] * log_gf[:, None])
    zeta_b = jnp.exp(pos[None, :] * log_gb[:, None])
    xi_b = jnp.exp((c - pos)[None, :] * log_gb[:, None])
    wide = lambda t, n: jnp.broadcast_to(t[:, :, None], (heads, c, n))
    g_f = jnp.exp(c * log_gf)
    g_b = jnp.exp(c * log_gb)
    gc = jnp.zeros((heads, SUBLANES, dv), F32)
    gc = gc.at[:, 0, :].set(g_f[:, None]).at[:, 1, :].set(g_b[:, None])
    return (dec, wide(zeta_f, dk), wide(zeta_b, dk), wide(xi_f, dv), wide(xi_b, dv), gc)


def _proj_res_kernel(y_ref, w_ref, h_ref, o_ref):
    o_ref[...] = h_ref[...] + jnp.dot(y_ref[...], w_ref[...],
                                      preferred_element_type=F32)


def _proj_res(y, w, h):
    rows, kd = y.shape
    d = h.shape[1]
    r = ROW_TILE
    return pl.pallas_call(
        _proj_res_kernel,
        grid=(rows // r,),
        in_specs=[pl.BlockSpec((r, kd), lambda i: (i, 0)), _const_spec(w.shape),
                  pl.BlockSpec((r, d), lambda i: (i, 0))],
        out_specs=pl.BlockSpec((r, d), lambda i: (i, 0)),
        out_shape=jax.ShapeDtypeStruct((rows, d), F32),
        compiler_params=_cparams("parallel"),
        name="proj_res",
    )(y, w, h)


def _rope_tables(seq, hd):
    half = hd // 2
    inv_freq = 1.0 / (ROPE_BASE ** jnp.linspace(0.0, 1.0, half, dtype=F32))
    ang = jnp.arange(seq, dtype=F32)[:, None] * inv_freq[None, :]
    return jnp.cos(ang), jnp.sin(ang)


def kernel(x, ln_mix, ln_ffn, ln_final, lru_w_in, lru_conv_w, lru_conv_b,
           lru_gate_a_w, lru_gate_a_b, lru_gate_x_w, lru_gate_x_b, lru_lambda,
           lru_w_out, ret_w_in, ret_w_out, ffn_w_gate, ffn_w_up, ffn_w_down):
    b, seq, d = x.shape
    depth = ln_mix.shape[0]
    rows = b * seq
    assert b == SUBLANES and rows % ROW_TILE == 0 and seq % ROW_TILE == 0
    row = lambda v: v.reshape(1, -1)

    h = x.reshape(rows, d)
    for layer in range(depth):
        j = layer // 2
        if layer % 2 == 0:
            width = lru_w_out.shape[1]
            x_tm = jnp.transpose(h.reshape(b, seq, d), (1, 0, 2)).reshape(rows, d)
            xb, gact = _lru_in(x_tm, row(ln_mix[layer]), lru_w_in[j].astype(BF16))
            wg = [jnp.concatenate([lru_gate_a_w[j, dr], lru_gate_x_w[j, dr]],
                                  axis=-1).astype(BF16) for dr in range(2)]
            xbc, hf = _lru_fwd(xb, lru_conv_w[j], row(lru_conv_b[j]), wg[0],
                               row(lru_gate_a_b[j, 0]), row(lru_gate_x_b[j, 0]),
                               row(lru_lambda[j, 0]), b)
            h_tm = _lru_bwd(xbc, hf, gact, x_tm, wg[1], row(lru_gate_a_b[j, 1]),
                            row(lru_gate_x_b[j, 1]), row(lru_lambda[j, 1]),
                            lru_w_out[j].astype(BF16), b)
            h = jnp.transpose(h_tm.reshape(seq, b, d), (1, 0, 2)).reshape(rows, d)
        else:
            heads = RET_HEADS
            vd = ret_w_out.shape[1]
            qk = (ret_w_in.shape[2] - 2 * vd) // 2
            cos, sin = _rope_tables(seq, qk // heads)
            q, k, v, sg = _ret_in(h, row(ln_mix[layer]), ret_w_in[j].astype(BF16),
                                  cos, sin, seq, heads, qk, vd)
            tabs = _ret_tables(heads, qk // heads, vd // heads)
            yg = _ret_core(q.reshape(b, seq, qk), k.reshape(b, seq, qk),
                           v.reshape(b, seq, vd), sg.reshape(b, seq, vd), tabs, heads)
            h = _proj_res(yg.reshape(rows, vd), ret_w_out[j].astype(BF16), h)
        last = layer == depth - 1
        h = _ffn(h, row(ln_ffn[layer]), ffn_w_gate[layer].astype(BF16),
                 ffn_w_up[layer].astype(BF16), ffn_w_down[layer].astype(BF16),
                 row(ln_final) if last else None)
    return h.reshape(b, seq, d)
```

```python
import functools
import math

import jax
import jax.numpy as jnp
from jax import lax
from jax.experimental import pallas as pl
from jax.experimental.pallas import tpu as pltpu

F32 = jnp.float32
BF16 = jnp.bfloat16

NORM_EPS = 1e-6
RG_LRU_C = 8.0
LRU_BLOCKS = 4
CONV_WIDTH = 4
CONV_PAD_LEFT = 2
RET_HEADS = 4
ROPE_BASE = 10000.0

SUBLANES = 8
LANES = 128
VMEM_LIMIT_BYTES = 56 * 1024 * 1024
ROW_TILE = 512
LRU_T = 64
RET_CHUNK = 256


def _cparams(*sem):
    return pltpu.CompilerParams(dimension_semantics=sem,
                                vmem_limit_bytes=VMEM_LIMIT_BYTES)


def _const_spec(shape):
    nd = len(shape)
    return pl.BlockSpec(shape, lambda *_: (0,) * nd, pipeline_mode=pl.Buffered(1))


def _rms(x, g):
    ms = jnp.mean(x * x, axis=-1, keepdims=True)
    return x * lax.rsqrt(ms + NORM_EPS) * g


def _sigmoid(x):
    return 0.5 * jnp.tanh(0.5 * x) + 0.5


def _gelu_tanh(x):
    c = math.sqrt(2.0 / math.pi)
    return 0.5 * x * (1.0 + jnp.tanh(c * (x + 0.044715 * (x * x * x))))


def _softplus(x):
    return jnp.maximum(x, 0.0) + jnp.log(1.0 + jnp.exp(-jnp.abs(x)))


def _lru_gates(xbc_ref, wg_ref, ba_ref, bx_ref, lam_ref, a_scr, inp_scr, sub_rows):
    rows, width = xbc_ref.shape
    bw = width // LRU_BLOCKS
    c2 = (-0.5 * RG_LRU_C * math.log2(math.e)) * _softplus(-lam_ref[...])
    ba2 = 0.5 * ba_ref[...]
    bx2 = 0.5 * bx_ref[...]

    def body(s, carry):
        r0 = pl.multiple_of(s * sub_rows, sub_rows)
        for n in range(LRU_BLOCKS):
            cs = slice(n * bw, (n + 1) * bw)
            xg = xbc_ref[pl.ds(r0, sub_rows), cs]
            z = jnp.dot(xg.astype(BF16), wg_ref[n], preferred_element_type=F32)
            tr = jnp.tanh(z[:, :bw] + ba2[:, cs])
            ti = jnp.tanh(z[:, bw:] + bx2[:, cs])
            a = jnp.exp2(c2[:, cs] * (tr + 1.0))
            om = 1.0 - a * a
            root = jnp.where(om > 0.0, om * lax.rsqrt(om), 0.0)
            a_scr[pl.ds(r0, sub_rows), cs] = a
            inp_scr[pl.ds(r0, sub_rows), cs] = ((ti + 1.0) * (0.5 * xg.astype(F32))) * root
        return carry

    lax.fori_loop(0, rows // sub_rows, body, 0)


def _lru_fwd_kernel(x_ref, g_ref, w_ref, cw_ref, cb_ref, wg_ref, ba_ref, bx_ref,
                    lam_ref, xbc_ref, hf_ref, gact_ref, hn_scr, u_scr, ext_scr,
                    xbc_scr, a_scr, inp_scr, carry_scr):
    i = pl.program_id(0)
    last = pl.num_programs(0) - 1
    nb, t, _ = x_ref.shape
    rows, width = xbc_ref.shape
    halo = (CONV_WIDTH - 1) * nb
    nslab = width // LANES

    @pl.when(i == 0)
    def _():
        carry_scr[...] = jnp.zeros_like(carry_scr)
        ext_scr[:, pl.ds(0, halo), :] = jnp.zeros((nslab, halo, LANES), F32)

    @pl.when(i > 0)
    def _():
        ext_scr[:, pl.ds(0, halo), :] = ext_scr[:, pl.ds(rows, halo), :]

    @pl.when(i < last)
    def _():
        for b in range(nb):
            hn_scr[pl.ds(b * t, t), :] = _rms(x_ref[b], g_ref[...]).astype(BF16)
        u_scr[...] = jnp.dot(hn_scr[...], w_ref[:, :width], preferred_element_type=F32)
        for b in range(nb):
            for l in range(nslab):
                ext_scr[l, pl.ds(halo + b, t, stride=nb), :] = (
                    u_scr[pl.ds(b * t, t), l * LANES:(l + 1) * LANES])
        u_scr[...] = jnp.dot(hn_scr[...], w_ref[:, width:], preferred_element_type=F32)
        for b in range(nb):
            gact_ref[b] = _gelu_tanh(u_scr[pl.ds(b * t, t), :]).astype(gact_ref.dtype)

    @pl.when(i == last)
    def _():
        ext_scr[:, pl.ds(halo, rows), :] = jnp.zeros((nslab, rows, LANES), F32)

    for l in range(nslab):
        ls = slice(l * LANES, (l + 1) * LANES)
        acc = cb_ref[:, ls] + cw_ref[pl.ds(0, 1), ls] * ext_scr[l, pl.ds(0, rows), :]
        for k in range(1, CONV_WIDTH):
            acc = acc + cw_ref[pl.ds(k, 1), ls] * ext_scr[l, pl.ds(k * nb, rows), :]
        xbc_scr[:, ls] = acc

    @pl.when(i == last)
    def _():
        xbc_scr[pl.ds(nb, rows - nb), :] = jnp.zeros((rows - nb, width), F32)

    xbc_ref[...] = xbc_scr[...].astype(xbc_ref.dtype)
    _lru_gates(xbc_scr, wg_ref, ba_ref, bx_ref, lam_ref, a_scr, inp_scr, 256)

    @pl.when(i == 0)
    def _():
        inp_scr[pl.ds(0, nb), :] = jnp.zeros((nb, width), F32)

    def step(s, h):
        r0 = pl.multiple_of(s * nb, nb)
        h = a_scr[pl.ds(r0, nb), :] * h + inp_scr[pl.ds(r0, nb), :]
        inp_scr[pl.ds(r0, nb), :] = h
        return h

    carry_scr[...] = lax.fori_loop(0, rows // nb, step, carry_scr[...], unroll=8)
    hf_ref[...] = inp_scr[...].astype(hf_ref.dtype)


def _lru_fwd(x, g, w_in, conv_w, conv_b, wg, ba, bx, lam):
    nb, seq, d = x.shape
    width = w_in.shape[1] // 2
    t = LRU_T
    r = nb * t
    nt = seq // t
    halo = (CONV_WIDTH - 1) * nb
    assert CONV_WIDTH - 1 - CONV_PAD_LEFT == 1
    tile = lambda i: (0, jnp.minimum(i, nt - 1), 0)
    win = lambda i: (i, 0)
    return pl.pallas_call(
        _lru_fwd_kernel,
        grid=(nt + 1,),
        in_specs=[pl.BlockSpec((nb, t, d), tile), _const_spec((1, d)),
                  _const_spec(w_in.shape), _const_spec(conv_w.shape),
                  _const_spec(conv_b.shape), _const_spec(wg.shape),
                  _const_spec(ba.shape), _const_spec(bx.shape), _const_spec(lam.shape)],
        out_specs=[pl.BlockSpec((r, width), win), pl.BlockSpec((r, width), win),
                   pl.BlockSpec((nb, t, width), tile)],
        out_shape=[jax.ShapeDtypeStruct(((nt + 1) * r, width), BF16),
                   jax.ShapeDtypeStruct(((nt + 1) * r, width), BF16),
                   jax.ShapeDtypeStruct((nb, seq, width), BF16)],
        scratch_shapes=[pltpu.VMEM((r, d), BF16),
                        pltpu.VMEM((r, width), F32),
                        pltpu.VMEM((width // LANES, halo + r, LANES), F32),
                        pltpu.VMEM((r, width), F32),
                        pltpu.VMEM((r, width), F32),
                        pltpu.VMEM((r, width), F32),
                        pltpu.VMEM((nb, width), F32)],
        compiler_params=_cparams("arbitrary"),
        name="lru_fwd",
    )(x, g, w_in, conv_w, conv_b, wg, ba, bx, lam)


def _lru_bwd_kernel(xbc_ref, hf_ref, gact_ref, x_ref, wg_ref, ba_ref, bx_ref,
                    lam_ref, wo_ref, o_ref, a_scr, inp_scr, carry_scr, ysum_scr,
                    ybm_scr):
    i = pl.program_id(0)
    nb, t, _ = x_ref.shape
    rows, width = xbc_ref.shape
    nsteps = rows // nb
    nslab = width // LANES

    @pl.when(i == 0)
    def _():
        carry_scr[...] = jnp.zeros_like(carry_scr)

    _lru_gates(xbc_ref, wg_ref, ba_ref, bx_ref, lam_ref, a_scr, inp_scr, 256)

    def step(s, h):
        r0 = pl.multiple_of((nsteps - 1 - s) * nb, nb)
        h = a_scr[pl.ds(r0, nb), :] * h + inp_scr[pl.ds(r0, nb), :]
        inp_scr[pl.ds(r0, nb), :] = h
        return h

    carry_scr[...] = lax.fori_loop(0, nsteps, step, carry_scr[...], unroll=8)

    for l in range(nslab):
        ls = slice(l * LANES, (l + 1) * LANES)
        ysum_scr[l, pl.ds(0, rows), :] = inp_scr[:, ls] + hf_ref[:, ls].astype(F32)

    @pl.when(i > 0)
    def _():
        for b in range(nb):
            for l in range(nslab):
                ls = slice(l * LANES, (l + 1) * LANES)
                yb = ysum_scr[l, pl.ds(nb + b, t, stride=nb), :]
                ybm_scr[pl.ds(b * t, t), ls] = (
                    yb * gact_ref[b, :, ls].astype(F32)).astype(BF16)
        res = jnp.dot(ybm_scr[...], wo_ref[...], preferred_element_type=F32)
        for b in range(nb):
            o_ref[b] = x_ref[b] + res[b * t:(b + 1) * t]

    ysum_scr[:, pl.ds(rows, nb), :] = ysum_scr[:, pl.ds(0, nb), :]


def _lru_bwd(xbc, hf, gact, x, wg, ba, bx, lam, w_out):
    nb, seq, d = x.shape
    width = xbc.shape[1]
    t = LRU_T
    r = nb * t
    nt = seq // t
    tile = lambda i: (0, jnp.minimum(nt - i, nt - 1), 0)
    win = lambda i: (nt - i, 0)
    return pl.pallas_call(
        _lru_bwd_kernel,
        grid=(nt + 1,),
        in_specs=[pl.BlockSpec((r, width), win), pl.BlockSpec((r, width), win),
                  pl.BlockSpec((nb, t, width), tile), pl.BlockSpec((nb, t, d), tile),
                  _const_spec(wg.shape), _const_spec(ba.shape), _const_spec(bx.shape),
                  _const_spec(lam.shape), _const_spec(w_out.shape)],
        out_specs=pl.BlockSpec((nb, t, d), tile),
        out_shape=jax.ShapeDtypeStruct((nb, seq, d), F32),
        scratch_shapes=[pltpu.VMEM((r, width), F32),
                        pltpu.VMEM((r, width), F32),
                        pltpu.VMEM((nb, width), F32),
                        pltpu.VMEM((width // LANES, r + nb, LANES), F32),
                        pltpu.VMEM((r, width), BF16)],
        compiler_params=_cparams("arbitrary"),
        name="lru_bwd",
    )(xbc, hf, gact, x, wg, ba, bx, lam, w_out)


def _ffn_chunks(hidden):
    chunks, c = [], 0
    while c < hidden:
        size = 512 if hidden - c >= 512 else hidden - c
        chunks.append((c, size))
        c += size
    return chunks


def _ffn_kernel(*refs, final_norm):
    if final_norm:
        h_ref, g_ref, wg_ref, wu_ref, wd_ref, gf_ref, o_ref, hn_scr, act_scr = refs
    else:
        h_ref, g_ref, wg_ref, wu_ref, wd_ref, o_ref, hn_scr, act_scr = refs
    hn_scr[...] = _rms(h_ref[...], g_ref[...]).astype(BF16)
    for c, size in _ffn_chunks(wg_ref.shape[1]):
        gt = jnp.dot(hn_scr[...], wg_ref[:, c:c + size], preferred_element_type=F32)
        up = jnp.dot(hn_scr[...], wu_ref[:, c:c + size], preferred_element_type=F32)
        act_scr[:, c:c + size] = (gt * _sigmoid(gt) * up).astype(BF16)
    y = h_ref[...] + jnp.dot(act_scr[...], wd_ref[...], preferred_element_type=F32)
    if final_norm:
        y = _rms(y, gf_ref[...])
    o_ref[...] = y


def _ffn(h, g, w_gate, w_up, w_down, g_final=None):
    rows, d = h.shape
    hidden = w_gate.shape[1]
    r = ROW_TILE
    final_norm = g_final is not None
    in_specs = [pl.BlockSpec((r, d), lambda i: (i, 0)), _const_spec((1, d)),
                _const_spec(w_gate.shape), _const_spec(w_up.shape),
                _const_spec(w_down.shape)]
    args = [h, g, w_gate, w_up, w_down]
    if final_norm:
        in_specs.append(_const_spec((1, d)))
        args.append(g_final)
    return pl.pallas_call(
        functools.partial(_ffn_kernel, final_norm=final_norm),
        grid=(rows // r,),
        in_specs=in_specs,
        out_specs=pl.BlockSpec((r, d), lambda i: (i, 0)),
        out_shape=jax.ShapeDtypeStruct((rows, d), F32),
        scratch_shapes=[pltpu.VMEM((r, d), BF16), pltpu.VMEM((r, hidden), BF16)],
        compiler_params=_cparams("parallel"),
        name="ffn_final" if final_norm else "ffn",
    )(*args)


def _ret_in_kernel(h_ref, g_ref, w_ref, cos_ref, sin_ref, q_ref, k_ref, v_ref,
                   sg_ref, hn_scr, *, heads):
    qk = q_ref.shape[1]
    vd = v_ref.shape[1]
    hd = qk // heads
    half = hd // 2
    hn_scr[...] = _rms(h_ref[...], g_ref[...]).astype(BF16)
    cos = cos_ref[...]
    sin = sin_ref[...]
    k_scale = hd ** -0.5
    for h in range(heads):
        for which, dst, scale in ((0, q_ref, 1.0), (1, k_ref, k_scale)):
            c0 = which * qk + h * hd
            u = jnp.dot(hn_scr[...], w_ref[:, c0:c0 + hd], preferred_element_type=F32)
            x1, x2 = u[:, :half], u[:, half:]
            r1 = x1 * cos - x2 * sin
            r2 = x2 * cos + x1 * sin
            if scale != 1.0:
                r1, r2 = r1 * scale, r2 * scale
            dst[:, h * hd:h * hd + half] = r1.astype(dst.dtype)
            dst[:, h * hd + half:(h + 1) * hd] = r2.astype(dst.dtype)
    step = 512
    for c in range(0, vd, step):
        v_ref[:, c:c + step] = jnp.dot(
            hn_scr[...], w_ref[:, 2 * qk + c:2 * qk + c + step],
            preferred_element_type=F32).astype(v_ref.dtype)
    for c in range(0, vd, step):
        gt = jnp.dot(hn_scr[...], w_ref[:, 2 * qk + vd + c:2 * qk + vd + c + step],
                     preferred_element_type=F32)
        sg_ref[:, c:c + step] = gt * _sigmoid(gt)


def _ret_in(h, g, w_in, cos, sin, seq, heads, qk, vd):
    rows, d = h.shape
    r = ROW_TILE
    half = cos.shape[1]
    per_seq = seq // r
    pos = lambda i: (i % per_seq, 0)
    row = lambda i: (i, 0)
    return pl.pallas_call(
        functools.partial(_ret_in_kernel, heads=heads),
        grid=(rows // r,),
        in_specs=[pl.BlockSpec((r, d), row), _const_spec((1, d)),
                  _const_spec(w_in.shape),
                  pl.BlockSpec((r, half), pos), pl.BlockSpec((r, half), pos)],
        out_specs=[pl.BlockSpec((r, qk), row), pl.BlockSpec((r, qk), row),
                   pl.BlockSpec((r, vd), row), pl.BlockSpec((r, vd), row)],
        out_shape=[jax.ShapeDtypeStruct((rows, qk), BF16),
                   jax.ShapeDtypeStruct((rows, qk), BF16),
                   jax.ShapeDtypeStruct((rows, vd), BF16),
                   jax.ShapeDtypeStruct((rows, vd), F32)],
        scratch_shapes=[pltpu.VMEM((r, d), BF16)],
        compiler_params=_cparams("parallel"),
        name="ret_in",
    )(h, g, w_in, cos, sin)


def _ret_core_kernel(q_ref, k_ref, v_ref, sg_ref, dec_ref, zf_ref, zb_ref, xf_ref,
                     xb_ref, gc_ref, o_ref, sb_scr, st_scr):
    seq, dk = q_ref.shape
    dv = v_ref.shape[1]
    c = dec_ref.shape[0]
    nc = seq // c
    gcf = gc_ref[pl.ds(0, 1), :]
    gcb = gc_ref[pl.ds(1, 1), :]

    def kv_update(state, r0, z_ref, gcd):
        kz = (k_ref[pl.ds(r0, c), :].astype(F32) * z_ref[...]).astype(BF16)
        upd = lax.dot_general(kz, v_ref[pl.ds(r0, c), :], (((0,), (0,)), ((), ())),
                              preferred_element_type=F32)
        return state * gcd + upd

    st_scr[...] = jnp.zeros_like(st_scr)

    def bwd_body(s, carry):
        n = nc - 1 - s
        r0 = pl.multiple_of(n * c, c)
        sb_scr[n] = st_scr[...].astype(BF16)
        st_scr[...] = kv_update(st_scr[...], r0, zb_ref, gcb)
        return carry

    lax.fori_loop(0, nc, bwd_body, 0, unroll=4)

    st_scr[...] = jnp.zeros_like(st_scr)

    def fwd_body(n, carry):
        r0 = pl.multiple_of(n * c, c)
        q = q_ref[pl.ds(r0, c), :]
        scores = lax.dot_general(q, k_ref[pl.ds(r0, c), :], (((1,), (1,)), ((), ())),
                                 preferred_element_type=F32)
        p = (scores * dec_ref[...]).astype(BF16)
        y = jnp.dot(p, v_ref[pl.ds(r0, c), :], preferred_element_type=F32)
        y = y + xf_ref[...] * jnp.dot(q, st_scr[...].astype(BF16),
                                      preferred_element_type=F32)
        y = y + xb_ref[...] * jnp.dot(q, sb_scr[n], preferred_element_type=F32)
        st_scr[...] = kv_update(st_scr[...], r0, zf_ref, gcf)
        y = y * lax.rsqrt(jnp.mean(y * y, axis=-1, keepdims=True) + NORM_EPS)
        o_ref[pl.ds(r0, c), :] = (y * sg_ref[pl.ds(r0, c), :]).astype(o_ref.dtype)
        return carry

    lax.fori_loop(0, nc, fwd_body, 0, unroll=2)


def _ret_core(q, k, v, sg, tabs, heads):
    b, seq, qk = q.shape
    vd = v.shape[2]
    dk, dv = qk // heads, vd // heads
    c = RET_CHUNK
    dec, zf, zb, xf, xb, gc = tabs
    bh = lambda i, j: (i, 0, j)
    hh = lambda i, j: (j, 0, 0)
    return pl.pallas_call(
        _ret_core_kernel,
        grid=(b, heads),
        in_specs=[pl.BlockSpec((None, seq, dk), bh), pl.BlockSpec((None, seq, dk), bh),
                  pl.BlockSpec((None, seq, dv), bh), pl.BlockSpec((None, seq, dv), bh),
                  pl.BlockSpec((None, c, c), hh),
                  pl.BlockSpec((None, c, dk), hh), pl.BlockSpec((None, c, dk), hh),
                  pl.BlockSpec((None, c, dv), hh), pl.BlockSpec((None, c, dv), hh),
                  pl.BlockSpec((None, SUBLANES, dv), hh)],
        out_specs=pl.BlockSpec((None, seq, dv), bh),
        out_shape=jax.ShapeDtypeStruct((b, seq, vd), BF16),
        scratch_shapes=[pltpu.VMEM((seq // c, dk, dv), BF16),
                        pltpu.VMEM((dk, dv), F32)],
        compiler_params=_cparams("parallel", "parallel"),
        name="ret_core",
    )(q, k, v, sg, dec, zf, zb, xf, xb, gc)


def _ret_tables(heads, dk, dv):
    c = RET_CHUNK
    log_gf = jnp.log1p(-jnp.exp2(-5.0 - jnp.arange(heads, dtype=F32)))
    log_gb = log_gf[::-1]
    pos = jnp.arange(c, dtype=F32)
    diff = pos[:, None] - pos[None, :]
    lower = diff >= 0
    upper = diff < 0
    dec_f = jnp.where(lower[None], jnp.exp(jnp.where(lower, diff, 0.0)[None]
                                           * log_gf[:, None, None]), 0.0)
    dec_b = jnp.where(upper[None], jnp.exp(jnp.where(upper, -diff, 0.0)[None]
                                           * log_gb[:, None, None]), 0.0)
    dec = dec_f + dec_b
    zeta_f = jnp.exp((c - 1.0 - pos)[None, :] * log_gf[:, None])
    xi_f = jnp.exp((pos + 1.0)[None, :] * log_gf[:, None])
    zeta_b = jnp.exp(pos[None, :] * log_gb[:, None])
    xi_b = jnp.exp((c - pos)[None, :] * log_gb[:, None])
    wide = lambda t, n: jnp.broadcast_to(t[:, :, None], (heads, c, n))
    g_f = jnp.exp(c * log_gf)
    g_b = jnp.exp(c * log_gb)
    gc = jnp.zeros((heads, SUBLANES, dv), F32)
    gc = gc.at[:, 0, :].set(g_f[:, None]).at[:, 1, :].set(g_b[:, None])
    return (dec, wide(zeta_f, dk), wide(zeta_b, dk), wide(xi_f, dv), wide(xi_b, dv), gc)


def _proj_res_kernel(y_ref, w_ref, h_ref, o_ref):
    o_ref[...] = h_ref[...] + jnp.dot(y_ref[...], w_ref[...],
                                      preferred_element_type=F32)


def _proj_res(y, w, h):
    rows, kd = y.shape
    d = h.shape[1]
    r = ROW_TILE
    return pl.pallas_call(
        _proj_res_kernel,
        grid=(rows // r,),
        in_specs=[pl.BlockSpec((r, kd), lambda i: (i, 0)), _const_spec(w.shape),
                  pl.BlockSpec((r, d), lambda i: (i, 0))],
        out_specs=pl.BlockSpec((r, d), lambda i: (i, 0)),
        out_shape=jax.ShapeDtypeStruct((rows, d), F32),
        compiler_params=_cparams("parallel"),
        name="proj_res",
    )(y, w, h)


def _rope_tables(seq, hd):
    half = hd // 2
    inv_freq = 1.0 / (ROPE_BASE ** jnp.linspace(0.0, 1.0, half, dtype=F32))
    ang = jnp.arange(seq, dtype=F32)[:, None] * inv_freq[None, :]
    return jnp.cos(ang), jnp.sin(ang)


def kernel(x, ln_mix, ln_ffn, ln_final, lru_w_in, lru_conv_w, lru_conv_b,
           lru_gate_a_w, lru_gate_a_b, lru_gate_x_w, lru_gate_x_b, lru_lambda,
           lru_w_out, ret_w_in, ret_w_out, ffn_w_gate, ffn_w_up, ffn_w_down):
    b, seq, d = x.shape
    depth = ln_mix.shape[0]
    rows = b * seq
    assert b == SUBLANES and seq % ROW_TILE == 0 and seq % LRU_T == 0
    row = lambda v: v.reshape(1, -1)

    h = x.reshape(rows, d)
    for layer in range(depth):
        j = layer // 2
        if layer % 2 == 0:
            x3 = h.reshape(b, seq, d)
            wg = [(0.5 * jnp.concatenate([lru_gate_a_w[j, dr], lru_gate_x_w[j, dr]],
                                         axis=-1)).astype(BF16) for dr in range(2)]
            xbc, hf, gact = _lru_fwd(
                x3, row(ln_mix[layer]), lru_w_in[j].astype(BF16), lru_conv_w[j],
                row(lru_conv_b[j]), wg[0], row(lru_gate_a_b[j, 0]),
                row(lru_gate_x_b[j, 0]), row(lru_lambda[j, 0]))
            h = _lru_bwd(xbc, hf, gact, x3, wg[1], row(lru_gate_a_b[j, 1]),
                         row(lru_gate_x_b[j, 1]), row(lru_lambda[j, 1]),
                         lru_w_out[j].astype(BF16)).reshape(rows, d)
        else:
            heads = RET_HEADS
            vd = ret_w_out.shape[1]
            qk = (ret_w_in.shape[2] - 2 * vd) // 2
            cos, sin = _rope_tables(seq, qk // heads)
            q, k, v, sg = _ret_in(h, row(ln_mix[layer]), ret_w_in[j].astype(BF16),
                                  cos, sin, seq, heads, qk, vd)
            tabs = _ret_tables(heads, qk // heads, vd // heads)
            yg = _ret_core(q.reshape(b, seq, qk), k.reshape(b, seq, qk),
                           v.reshape(b, seq, vd), sg.reshape(b, seq, vd), tabs, heads)
            h = _proj_res(yg.reshape(rows, vd), ret_w_out[j].astype(BF16), h)
        last = layer == depth - 1
        h = _ffn(h, row(ln_ffn[layer]), ffn_w_gate[layer].astype(BF16),
                 ffn_w_up[layer].astype(BF16), ffn_w_down[layer].astype(BF16),
                 row(ln_final) if last else None)
    return h.reshape(b, seq, d)
```

```python
import functools
import math

import jax
import jax.numpy as jnp
from jax import lax
from jax.experimental import pallas as pl
from jax.experimental.pallas import tpu as pltpu

F32 = jnp.float32
BF16 = jnp.bfloat16

NORM_EPS = 1e-6
RG_LRU_C = 8.0
LRU_BLOCKS = 4
CONV_WIDTH = 4
CONV_PAD_LEFT = 2
RET_HEADS = 4
ROPE_BASE = 10000.0

SUBLANES = 8
LANES = 128
VMEM_LIMIT_BYTES = 56 * 1024 * 1024
ROW_TILE = 512
LRU_T = 64
GATE_ROWS = 256
PROJ_COLS = 512
RET_CHUNK = 256


def _cparams(*sem):
    return pltpu.CompilerParams(dimension_semantics=sem,
                                vmem_limit_bytes=VMEM_LIMIT_BYTES)


def _const_spec(shape):
    nd = len(shape)
    return pl.BlockSpec(shape, lambda *_: (0,) * nd, pipeline_mode=pl.Buffered(1))


def _rms(x, g):
    ms = jnp.mean(x * x, axis=-1, keepdims=True)
    return x * lax.rsqrt(ms + NORM_EPS) * g


def _sigmoid(x):
    return 0.5 * jnp.tanh(0.5 * x) + 0.5


def _gelu_tanh(x):
    c = math.sqrt(2.0 / math.pi)
    return 0.5 * x * (1.0 + jnp.tanh(c * (x + 0.044715 * (x * x * x))))


def _softplus(x):
    return jnp.maximum(x, 0.0) + jnp.log(1.0 + jnp.exp(-jnp.abs(x)))


def _lru_gate_pieces(xbc_ref, wg_ref, ba_ref, bx_ref, lam_ref, a_dst, inp_dst):
    rows, width = xbc_ref.shape
    bw = width // LRU_BLOCKS
    c2 = (-0.5 * RG_LRU_C * math.log2(math.e)) * _softplus(-lam_ref[...])
    ba2 = 0.5 * ba_ref[...]
    bx2 = 0.5 * bx_ref[...]

    def piece(r0, n):
        cs = slice(n * bw, (n + 1) * bw)
        xg = xbc_ref[pl.ds(r0, GATE_ROWS), cs]
        z = jnp.dot(xg.astype(BF16), wg_ref[n], preferred_element_type=F32)
        tr = jnp.tanh(z[:, :bw] + ba2[:, cs])
        ti = jnp.tanh(z[:, bw:] + bx2[:, cs])
        a = jnp.exp2(c2[:, cs] * (tr + 1.0))
        om = 1.0 - a * a
        root = jnp.where(om > 0.0, om * lax.rsqrt(om), 0.0)
        a_dst[pl.ds(r0, GATE_ROWS), cs] = a
        inp_dst[pl.ds(r0, GATE_ROWS), cs] = ((ti + 1.0) * (0.5 * xg.astype(F32))) * root

    return [functools.partial(piece, r0, n)
            for r0 in range(0, rows, GATE_ROWS) for n in range(LRU_BLOCKS)]


def _lru_fwd_kernel(x_ref, g_ref, w_ref, cw_ref, cb_ref, wg_ref, ba_ref, bx_ref,
                    lam_ref, xbc_ref, hf_ref, gact_ref, ext_scr, xbc_scr,
                    a_scr, inp_scr, carry_scr):
    i = pl.program_id(0)
    nt = pl.num_programs(0) - 2
    nb, t, _ = x_ref.shape
    rows, width = xbc_ref.shape
    halo = (CONV_WIDTH - 1) * nb
    nslab = width // LANES

    @pl.when(i == 0)
    def _():
        carry_scr[...] = jnp.zeros_like(carry_scr)
        ext_scr[...] = jnp.zeros_like(ext_scr)

    for l in range(nslab):
        ls = slice(l * LANES, (l + 1) * LANES)
        acc = cb_ref[:, ls] + cw_ref[pl.ds(0, 1), ls] * ext_scr[l, pl.ds(0, rows), :]
        for k in range(1, CONV_WIDTH):
            acc = acc + cw_ref[pl.ds(k, 1), ls] * ext_scr[l, pl.ds(k * nb, rows), :]
        xbc_scr[:, ls] = acc
    xbc_ref[...] = xbc_scr[...].astype(xbc_ref.dtype)

    ext_scr[:, pl.ds(0, halo), :] = ext_scr[:, pl.ds(rows, halo), :]
    gate_pieces = _lru_gate_pieces(xbc_scr, wg_ref, ba_ref, bx_ref, lam_ref,
                                   a_scr, inp_scr)
    nbs = GATE_ROWS // t
    for b0 in range(0, nb, nbs):
        hn = jnp.concatenate([_rms(x_ref[b0 + bb], g_ref[...]).astype(BF16)
                              for bb in range(nbs)], axis=0)
        for c0 in range(0, 2 * width, PROJ_COLS):
            u = jnp.dot(hn, w_ref[:, c0:c0 + PROJ_COLS], preferred_element_type=F32)
            for bb in range(nbs):
                ub = u[bb * t:(bb + 1) * t]
                if c0 < width:
                    for l in range(PROJ_COLS // LANES):
                        ext_scr[c0 // LANES + l, pl.ds(halo + b0 + bb, t, stride=nb), :] = (
                            ub[:, l * LANES:(l + 1) * LANES])
                else:
                    gact_ref[b0 + bb, :, c0 - width:c0 - width + PROJ_COLS] = (
                        _gelu_tanh(ub).astype(gact_ref.dtype))
            if gate_pieces:
                gate_pieces.pop(0)()
    for piece in gate_pieces:
        piece()


    h = jnp.where(i > 1, carry_scr[...], 0.0)
    for s in range(rows // nb):
        h = a_scr[pl.ds(s * nb, nb), :] * h + inp_scr[pl.ds(s * nb, nb), :]
        if s == 0:
            h = jnp.where(i > 1, h, 0.0)
        inp_scr[pl.ds(s * nb, nb), :] = h
    carry_scr[...] = h
    hf_ref[...] = inp_scr[...].astype(hf_ref.dtype)

    @pl.when(i == nt)
    def _():
        ext_scr[:, pl.ds(halo, nb), :] = jnp.zeros((nslab, nb, LANES), F32)

    @pl.when(i == nt + 1)
    def _():
        xbc_ref[pl.ds(2 * nb, rows - 2 * nb), :] = jnp.zeros(
            (rows - 2 * nb, width), xbc_ref.dtype)
        xbc_ref[pl.ds(0, 2 * nb), :] = jnp.where(
            lax.broadcasted_iota(jnp.int32, (2 * nb, width), 0) < nb,
            xbc_ref[pl.ds(0, 2 * nb), :], jnp.zeros((), xbc_ref.dtype))


def _lru_fwd(x, g, w_in, conv_w, conv_b, wg, ba, bx, lam):
    nb, seq, d = x.shape
    width = w_in.shape[1] // 2
    t = LRU_T
    r = nb * t
    nt = seq // t
    halo = (CONV_WIDTH - 1) * nb
    assert CONV_WIDTH - 1 - CONV_PAD_LEFT == 1
    tile = lambda i: (0, jnp.minimum(i, nt - 1), 0)
    win = lambda i: (jnp.maximum(i - 1, 0), 0)
    return pl.pallas_call(
        _lru_fwd_kernel,
        grid=(nt + 2,),
        in_specs=[pl.BlockSpec((nb, t, d), tile), _const_spec((1, d)),
                  _const_spec(w_in.shape), _const_spec(conv_w.shape),
                  _const_spec(conv_b.shape), _const_spec(wg.shape),
                  _const_spec(ba.shape), _const_spec(bx.shape), _const_spec(lam.shape)],
        out_specs=[pl.BlockSpec((r, width), win), pl.BlockSpec((r, width), win),
                   pl.BlockSpec((nb, t, width), tile)],
        out_shape=[jax.ShapeDtypeStruct(((nt + 1) * r, width), BF16),
                   jax.ShapeDtypeStruct(((nt + 1) * r, width), BF16),
                   jax.ShapeDtypeStruct((nb, seq, width), BF16)],
        scratch_shapes=[pltpu.VMEM((width // LANES, halo + r, LANES), F32),
                        pltpu.VMEM((r, width), F32),
                        pltpu.VMEM((r, width), F32),
                        pltpu.VMEM((r, width), F32),
                        pltpu.VMEM((nb, width), F32)],
        compiler_params=_cparams("arbitrary"),
        name="lru_fwd",
    )(x, g, w_in, conv_w, conv_b, wg, ba, bx, lam)


def _lru_bwd_kernel(xbc_ref, hf_ref, gact_ref, x_ref, wg_ref, ba_ref, bx_ref,
                    lam_ref, wo_ref, o_ref, a_scr, inp_scr, hb_scr, carry_scr,
                    ysum_scr, ybm_scr):
    i = pl.program_id(0)
    nb, t, _ = x_ref.shape
    rows, width = xbc_ref.shape
    nsteps = rows // nb
    nslab = width // LANES

    @pl.when(i == 0)
    def _():
        carry_scr[...] = jnp.zeros_like(carry_scr)
        hb_scr[...] = jnp.zeros_like(hb_scr)
        ysum_scr[:, pl.ds(rows, nb), :] = jnp.zeros((nslab, nb, LANES), F32)

    for l in range(nslab):
        ls = slice(l * LANES, (l + 1) * LANES)
        ysum_scr[l, pl.ds(0, rows), :] = hb_scr[:, ls] + hf_ref[:, ls].astype(F32)

    for b in range(nb):
        for l in range(nslab):
            ls = slice(l * LANES, (l + 1) * LANES)
            yb = ysum_scr[l, pl.ds(nb + b, t, stride=nb), :]
            ybm_scr[pl.ds(b * t, t), ls] = (
                yb * gact_ref[b, :, ls].astype(F32)).astype(BF16)
    ysum_scr[:, pl.ds(rows, nb), :] = ysum_scr[:, pl.ds(0, nb), :]

    gate_pieces = _lru_gate_pieces(xbc_ref, wg_ref, ba_ref, bx_ref, lam_ref,
                                   a_scr, inp_scr)
    gate_pieces = [p for r0 in range(rows - GATE_ROWS, -1, -GATE_ROWS)
                   for p in gate_pieces[(r0 // GATE_ROWS) * LRU_BLOCKS:
                                        (r0 // GATE_ROWS + 1) * LRU_BLOCKS]]
    d_out = wo_ref.shape[1]
    per_dot = -(-len(gate_pieces) * PROJ_COLS // (2 * d_out))
    for c0 in range(0, d_out, PROJ_COLS // 2):
        cs = slice(c0, c0 + PROJ_COLS // 2)
        res = jnp.dot(ybm_scr[...], wo_ref[:, cs], preferred_element_type=F32)
        for b in range(nb):
            o_ref[b, :, cs] = x_ref[b, :, cs] + res[b * t:(b + 1) * t]
        for _ in range(per_dot):
            if gate_pieces:
                gate_pieces.pop(0)()
    for piece in gate_pieces:
        piece()

    h = carry_scr[...]
    for s in range(nsteps - 1, -1, -1):
        h = a_scr[pl.ds(s * nb, nb), :] * h + inp_scr[pl.ds(s * nb, nb), :]
        hb_scr[pl.ds(s * nb, nb), :] = h
    carry_scr[...] = h


def _lru_bwd(xbc, hf, gact, x, wg, ba, bx, lam, w_out):
    nb, seq, d = x.shape
    width = xbc.shape[1]
    t = LRU_T
    r = nb * t
    nt = seq // t
    win_g = lambda i: (jnp.maximum(nt - i, 0), 0)
    win_s = lambda i: (jnp.minimum(nt - i + 1, nt), 0)
    tile = lambda i: (0, jnp.minimum(nt - i + 1, nt - 1), 0)
    return pl.pallas_call(
        _lru_bwd_kernel,
        grid=(nt + 2,),
        in_specs=[pl.BlockSpec((r, width), win_g), pl.BlockSpec((r, width), win_s),
                  pl.BlockSpec((nb, t, width), tile), pl.BlockSpec((nb, t, d), tile),
                  _const_spec(wg.shape), _const_spec(ba.shape), _const_spec(bx.shape),
                  _const_spec(lam.shape), _const_spec(w_out.shape)],
        out_specs=pl.BlockSpec((nb, t, d), tile),
        out_shape=jax.ShapeDtypeStruct((nb, seq, d), F32),
        scratch_shapes=[pltpu.VMEM((r, width), F32),
                        pltpu.VMEM((r, width), F32),
                        pltpu.VMEM((r, width), F32),
                        pltpu.VMEM((nb, width), F32),
                        pltpu.VMEM((width // LANES, r + nb, LANES), F32),
                        pltpu.VMEM((r, width), BF16)],
        compiler_params=_cparams("arbitrary"),
        name="lru_bwd",
    )(xbc, hf, gact, x, wg, ba, bx, lam, w_out)


def _ffn_chunks(hidden):
    chunks, c = [], 0
    while c < hidden:
        size = 512 if hidden - c >= 512 else hidden - c
        chunks.append((c, size))
        c += size
    return chunks


def _ffn_kernel(*refs, final_norm):
    if final_norm:
        h_ref, g_ref, wg_ref, wu_ref, wd_ref, gf_ref, o_ref, hn_scr, act_scr = refs
    else:
        h_ref, g_ref, wg_ref, wu_ref, wd_ref, o_ref, hn_scr, act_scr = refs
    hn_scr[...] = _rms(h_ref[...], g_ref[...]).astype(BF16)
    for c, size in _ffn_chunks(wg_ref.shape[1]):
        gt = jnp.dot(hn_scr[...], wg_ref[:, c:c + size], preferred_element_type=F32)
        up = jnp.dot(hn_scr[...], wu_ref[:, c:c + size], preferred_element_type=F32)
        act_scr[:, c:c + size] = (gt * _sigmoid(gt) * up).astype(BF16)
    y = h_ref[...] + jnp.dot(act_scr[...], wd_ref[...], preferred_element_type=F32)
    if final_norm:
        y = _rms(y, gf_ref[...])
    o_ref[...] = y


def _ffn(h, g, w_gate, w_up, w_down, g_final=None):
    rows, d = h.shape
    hidden = w_gate.shape[1]
    r = ROW_TILE
    final_norm = g_final is not None
    in_specs = [pl.BlockSpec((r, d), lambda i: (i, 0)), _const_spec((1, d)),
                _const_spec(w_gate.shape), _const_spec(w_up.shape),
                _const_spec(w_down.shape)]
    args = [h, g, w_gate, w_up, w_down]
    if final_norm:
        in_specs.append(_const_spec((1, d)))
        args.append(g_final)
    return pl.pallas_call(
        functools.partial(_ffn_kernel, final_norm=final_norm),
        grid=(rows // r,),
        in_specs=in_specs,
        out_specs=pl.BlockSpec((r, d), lambda i: (i, 0)),
        out_shape=jax.ShapeDtypeStruct((rows, d), F32),
        scratch_shapes=[pltpu.VMEM((r, d), BF16), pltpu.VMEM((r, hidden), BF16)],
        compiler_params=_cparams("parallel"),
        name="ffn_final" if final_norm else "ffn",
    )(*args)


def _ret_in_kernel(h_ref, g_ref, w_ref, cos_ref, sin_ref, q_ref, k_ref, v_ref,
                   sg_ref, hn_scr, *, heads):
    qk = q_ref.shape[1]
    vd = v_ref.shape[1]
    hd = qk // heads
    half = hd // 2
    hn_scr[...] = _rms(h_ref[...], g_ref[...]).astype(BF16)
    cos = cos_ref[...]
    sin = sin_ref[...]
    k_scale = hd ** -0.5
    for h in range(heads):
        for which, dst, scale in ((0, q_ref, 1.0), (1, k_ref, k_scale)):
            c0 = which * qk + h * hd
            u = jnp.dot(hn_scr[...], w_ref[:, c0:c0 + hd], preferred_element_type=F32)
            x1, x2 = u[:, :half], u[:, half:]
            r1 = x1 * cos - x2 * sin
            r2 = x2 * cos + x1 * sin
            if scale != 1.0:
                r1, r2 = r1 * scale, r2 * scale
            dst[:, h * hd:h * hd + half] = r1.astype(dst.dtype)
            dst[:, h * hd + half:(h + 1) * hd] = r2.astype(dst.dtype)
    step = 512
    for c in range(0, vd, step):
        v_ref[:, c:c + step] = jnp.dot(
            hn_scr[...], w_ref[:, 2 * qk + c:2 * qk + c + step],
            preferred_element_type=F32).astype(v_ref.dtype)
    for c in range(0, vd, step):
        gt = jnp.dot(hn_scr[...], w_ref[:, 2 * qk + vd + c:2 * qk + vd + c + step],
                     preferred_element_type=F32)
        sg_ref[:, c:c + step] = gt * _sigmoid(gt)


def _ret_in(h, g, w_in, cos, sin, seq, heads, qk, vd):
    rows, d = h.shape
    r = ROW_TILE
    half = cos.shape[1]
    per_seq = seq // r
    pos = lambda i: (i % per_seq, 0)
    row = lambda i: (i, 0)
    return pl.pallas_call(
        functools.partial(_ret_in_kernel, heads=heads),
        grid=(rows // r,),
        in_specs=[pl.BlockSpec((r, d), row), _const_spec((1, d)),
                  _const_spec(w_in.shape),
                  pl.BlockSpec((r, half), pos), pl.BlockSpec((r, half), pos)],
        out_specs=[pl.BlockSpec((r, qk), row), pl.BlockSpec((r, qk), row),
                   pl.BlockSpec((r, vd), row), pl.BlockSpec((r, vd), row)],
        out_shape=[jax.ShapeDtypeStruct((rows, qk), BF16),
                   jax.ShapeDtypeStruct((rows, qk), BF16),
                   jax.ShapeDtypeStruct((rows, vd), BF16),
                   jax.ShapeDtypeStruct((rows, vd), F32)],
        scratch_shapes=[pltpu.VMEM((r, d), BF16)],
        compiler_params=_cparams("parallel"),
        name="ret_in",
    )(h, g, w_in, cos, sin)


def _ret_core_kernel(q_ref, k_ref, v_ref, sg_ref, dec_ref, zf_ref, zb_ref, xf_ref,
                     xb_ref, gc_ref, o_ref, sb_scr, st_scr):
    seq, dk = q_ref.shape
    dv = v_ref.shape[1]
    c = dec_ref.shape[0]
    nc = seq // c
    gcf = gc_ref[pl.ds(0, 1), :]
    gcb = gc_ref[pl.ds(1, 1), :]

    def kv_update(state, r0, z_ref, gcd):
        kz = (k_ref[pl.ds(r0, c), :].astype(F32) * z_ref[...]).astype(BF16)
        upd = lax.dot_general(kz, v_ref[pl.ds(r0, c), :], (((0,), (0,)), ((), ())),
                              preferred_element_type=F32)
        return state * gcd + upd

    st_scr[...] = jnp.zeros_like(st_scr)

    def bwd_body(s, carry):
        n = nc - 1 - s
        r0 = pl.multiple_of(n * c, c)
        sb_scr[n] = st_scr[...].astype(BF16)
        st_scr[...] = kv_update(st_scr[...], r0, zb_ref, gcb)
        return carry

    lax.fori_loop(0, nc, bwd_body, 0, unroll=4)

    st_scr[...] = jnp.zeros_like(st_scr)

    def fwd_body(n, carry):
        r0 = pl.multiple_of(n * c, c)
        q = q_ref[pl.ds(r0, c), :]
        scores = lax.dot_general(q, k_ref[pl.ds(r0, c), :], (((1,), (1,)), ((), ())),
                                 preferred_element_type=F32)
        p = (scores * dec_ref[...]).astype(BF16)
        y = jnp.dot(p, v_ref[pl.ds(r0, c), :], preferred_element_type=F32)
        y = y + xf_ref[...] * jnp.dot(q, st_scr[...].astype(BF16),
                                      preferred_element_type=F32)
        y = y + xb_ref[...] * jnp.dot(q, sb_scr[n], preferred_element_type=F32)
        st_scr[...] = kv_update(st_scr[...], r0, zf_ref, gcf)
        y = y * lax.rsqrt(jnp.mean(y * y, axis=-1, keepdims=True) + NORM_EPS)
        o_ref[pl.ds(r0, c), :] = (y * sg_ref[pl.ds(r0, c), :]).astype(o_ref.dtype)
        return carry

    lax.fori_loop(0, nc, fwd_body, 0, unroll=2)


def _ret_core(q, k, v, sg, tabs, heads):
    b, seq, qk = q.shape
    vd = v.shape[2]
    dk, dv = qk // heads, vd // heads
    c = RET_CHUNK
    dec, zf, zb, xf, xb, gc = tabs
    bh = lambda i, j: (i, 0, j)
    hh = lambda i, j: (j, 0, 0)
    return pl.pallas_call(
        _ret_core_kernel,
        grid=(b, heads),
        in_specs=[pl.BlockSpec((None, seq, dk), bh), pl.BlockSpec((None, seq, dk), bh),
                  pl.BlockSpec((None, seq, dv), bh), pl.BlockSpec((None, seq, dv), bh),
                  pl.BlockSpec((None, c, c), hh),
                  pl.BlockSpec((None, c, dk), hh), pl.BlockSpec((None, c, dk), hh),
                  pl.BlockSpec((None, c, dv), hh), pl.BlockSpec((None, c, dv), hh),
                  pl.BlockSpec((None, SUBLANES, dv), hh)],
        out_specs=pl.BlockSpec((None, seq, dv), bh),
        out_shape=jax.ShapeDtypeStruct((b, seq, vd), BF16),
        scratch_shapes=[pltpu.VMEM((seq // c, dk, dv), BF16),
                        pltpu.VMEM((dk, dv), F32)],
        compiler_params=_cparams("parallel", "parallel"),
        name="ret_core",
    )(q, k, v, sg, dec, zf, zb, xf, xb, gc)


def _ret_tables(heads, dk, dv):
    c = RET_CHUNK
    log_gf = jnp.log1p(-jnp.exp2(-5.0 - jnp.arange(heads, dtype=F32)))
    log_gb = log_gf[::-1]
    pos = jnp.arange(c, dtype=F32)
    diff = pos[:, None] - pos[None, :]
    lower = diff >= 0
    upper = diff < 0
    dec_f = jnp.where(lower[None], jnp.exp(jnp.where(lower, diff, 0.0)[None]
                                           * log_gf[:, None, None]), 0.0)
    dec_b = jnp.where(upper[None], jnp.exp(jnp.where(upper, -diff, 0.0)[None]
                                           * log_gb[:, None, None]), 0.0)
    dec = dec_f + dec_b
    zeta_f = jnp.exp((c - 1.0 - pos)[None, :] * log_gf[:, None])
    xi_f = jnp.exp((pos + 1.0)[None, :] * log_gf[:, None])
    zeta_b = jnp.exp(pos[None, :] * log_gb[:, None])
    xi_b = jnp.exp((c - pos)[None, :] * log_gb[:, None])
    wide = lambda t, n: jnp.broadcast_to(t[:, :, None], (heads, c, n))
    g_f = jnp.exp(c * log_gf)
    g_b = jnp.exp(c * log_gb)
    gc = jnp.zeros((heads, SUBLANES, dv), F32)
    gc = gc.at[:, 0, :].set(g_f[:, None]).at[:, 1, :].set(g_b[:, None])
    return (dec, wide(zeta_f, dk), wide(zeta_b, dk), wide(xi_f, dv), wide(xi_b, dv), gc)


def _proj_res_kernel(y_ref, w_ref, h_ref, o_ref):
    o_ref[...] = h_ref[...] + jnp.dot(y_ref[...], w_ref[...],
                                      preferred_element_type=F32)


def _proj_res(y, w, h):
    rows, kd = y.shape
    d = h.shape[1]
    r = ROW_TILE
    return pl.pallas_call(
        _proj_res_kernel,
        grid=(rows // r,),
        in_specs=[pl.BlockSpec((r, kd), lambda i: (i, 0)), _const_spec(w.shape),
                  pl.BlockSpec((r, d), lambda i: (i, 0))],
        out_specs=pl.BlockSpec((r, d), lambda i: (i, 0)),
        out_shape=jax.ShapeDtypeStruct((rows, d), F32),
        compiler_params=_cparams("parallel"),
        name="proj_res",
    )(y, w, h)


def _rope_tables(seq, hd):
    half = hd // 2
    inv_freq = 1.0 / (ROPE_BASE ** jnp.linspace(0.0, 1.0, half, dtype=F32))
    ang = jnp.arange(seq, dtype=F32)[:, None] * inv_freq[None, :]
    return jnp.cos(ang), jnp.sin(ang)


def kernel(x, ln_mix, ln_ffn, ln_final, lru_w_in, lru_conv_w, lru_conv_b,
           lru_gate_a_w, lru_gate_a_b, lru_gate_x_w, lru_gate_x_b, lru_lambda,
           lru_w_out, ret_w_in, ret_w_out, ffn_w_gate, ffn_w_up, ffn_w_down):
    b, seq, d = x.shape
    depth = ln_mix.shape[0]
    rows = b * seq
    assert b == SUBLANES and seq % ROW_TILE == 0 and seq % LRU_T == 0
    row = lambda v: v.reshape(1, -1)

    h = x.reshape(rows, d)
    for layer in range(depth):
        j = layer // 2
        if layer % 2 == 0:
            x3 = h.reshape(b, seq, d)
            wg = [(0.5 * jnp.concatenate([lru_gate_a_w[j, dr], lru_gate_x_w[j, dr]],
                                         axis=-1)).astype(BF16) for dr in range(2)]
            xbc, hf, gact = _lru_fwd(
                x3, row(ln_mix[layer]), lru_w_in[j].astype(BF16), lru_conv_w[j],
                row(lru_conv_b[j]), wg[0], row(lru_gate_a_b[j, 0]),
                row(lru_gate_x_b[j, 0]), row(lru_lambda[j, 0]))
            h = _lru_bwd(xbc, hf, gact, x3, wg[1], row(lru_gate_a_b[j, 1]),
                         row(lru_gate_x_b[j, 1]), row(lru_lambda[j, 1]),
                         lru_w_out[j].astype(BF16)).reshape(rows, d)
        else:
            heads = RET_HEADS
            vd = ret_w_out.shape[1]
            qk = (ret_w_in.shape[2] - 2 * vd) // 2
            cos, sin = _rope_tables(seq, qk // heads)
            q, k, v, sg = _ret_in(h, row(ln_mix[layer]), ret_w_in[j].astype(BF16),
                                  cos, sin, seq, heads, qk, vd)
            tabs = _ret_tables(heads, qk // heads, vd // heads)
            yg = _ret_core(q.reshape(b, seq, qk), k.reshape(b, seq, qk),
                           v.reshape(b, seq, vd), sg.reshape(b, seq, vd), tabs, heads)
            h = _proj_res(yg.reshape(rows, vd), ret_w_out[j].astype(BF16), h)
        last = layer == depth - 1
        h = _ffn(h, row(ln_ffn[layer]), ffn_w_gate[layer].astype(BF16),
                 ffn_w_up[layer].astype(BF16), ffn_w_down[layer].astype(BF16),
                 row(ln_final) if last else None)
    return h.reshape(b, seq, d)
```

```python
import functools
import math

import jax
import jax.numpy as jnp
from jax import lax
from jax.experimental import pallas as pl
from jax.experimental.pallas import tpu as pltpu

F32 = jnp.float32
BF16 = jnp.bfloat16

NORM_EPS = 1e-6
RSQRT_FLOOR = 1e-30
RG_LRU_C = 8.0
LRU_BLOCKS = 4
CONV_WIDTH = 4
CONV_PAD_LEFT = 2
RET_HEADS = 4
ROPE_BASE = 10000.0

SUBLANES = 8
LANES = 128
VMEM_LIMIT_BYTES = 56 * 1024 * 1024
ROW_TILE = 1024
FUSED_FFN_TILE = 512
LRU_T = 64
GATE_ROWS = 256
PROJ_COLS = 1024
OUT_COLS = 256
RET_CHUNK = 256


def _cparams(*sem):
    return pltpu.CompilerParams(dimension_semantics=sem,
                                vmem_limit_bytes=VMEM_LIMIT_BYTES)


def _const_spec(shape):
    nd = len(shape)
    return pl.BlockSpec(shape, lambda *_: (0,) * nd, pipeline_mode=pl.Buffered(1))


def _layer_spec(shape, layer):
    nd = len(shape)
    return pl.BlockSpec((None,) + tuple(shape[1:]), lambda *_: (layer,) + (0,) * (nd - 1),
                        pipeline_mode=pl.Buffered(1))


def _rms(x, g):
    ms = jnp.mean(x * x, axis=-1, keepdims=True)
    return x * lax.rsqrt(ms + NORM_EPS) * g


def _sigmoid(x):
    return 0.5 * jnp.tanh(0.5 * x) + 0.5


def _gelu_tanh(x):
    c = math.sqrt(2.0 / math.pi)
    hx = 0.5 * x
    return hx * jnp.tanh(x * ((c * 0.044715) * (x * x) + c)) + hx


def _softplus(x):
    return jnp.maximum(x, 0.0) + jnp.log(1.0 + jnp.exp(-jnp.abs(x)))


def _lru_gate_pieces(xh_ref, wg_ref, ba_ref, bx_ref, lam_ref, a_dst, inp_dst):
    rows, width = xh_ref.shape
    bw = width // LRU_BLOCKS
    c2 = (-0.5 * RG_LRU_C * math.log2(math.e)) * _softplus(-lam_ref[...])
    ba2 = 0.5 * ba_ref[...]
    bx2 = 0.5 * bx_ref[...]

    def piece(r0, n):
        cs = slice(n * bw, (n + 1) * bw)
        xh = xh_ref[pl.ds(r0, GATE_ROWS), cs]
        z = jnp.dot(xh.astype(BF16), wg_ref[n], preferred_element_type=F32)
        tr = jnp.tanh(z[:, :bw] + ba2[:, cs])
        ti = jnp.tanh(z[:, bw:] + bx2[:, cs])
        a = jnp.exp2(c2[:, cs] * (tr + 1.0))
        om = 1.0 - a * a
        root = om * lax.rsqrt(jnp.maximum(om, RSQRT_FLOOR))
        a_dst[pl.ds(r0, GATE_ROWS), cs] = a
        inp_dst[pl.ds(r0, GATE_ROWS), cs] = ((ti + 1.0) * xh.astype(F32)) * root

    return [functools.partial(piece, r0, n)
            for r0 in range(0, rows, GATE_ROWS) for n in range(LRU_BLOCKS)]


def _lru_fwd_kernel(x_ref, g_ref, w_ref, cw_ref, cb_ref, wg_ref, ba_ref, bx_ref,
                    lam_ref, xh_ref, hf_ref, gb_ref, ext_scr, xh_scr,
                    a_scr, inp_scr, carry_scr):
    i = pl.program_id(0)
    nt = pl.num_programs(0) - 2
    nb, t, _ = x_ref.shape
    rows, width = xh_ref.shape
    halo = (CONV_WIDTH - 1) * nb
    nslab = width // LANES

    @pl.when(i == 0)
    def _():
        carry_scr[...] = jnp.zeros_like(carry_scr)
        ext_scr[...] = jnp.zeros_like(ext_scr)

    cwh = 0.5 * cw_ref[...]
    cbh = 0.5 * cb_ref[...]
    for l in range(nslab):
        ls = slice(l * LANES, (l + 1) * LANES)
        acc = cbh[:, ls] + cwh[0:1, ls] * ext_scr[l, pl.ds(0, rows), :]
        for k in range(1, CONV_WIDTH):
            acc = acc + cwh[k:k + 1, ls] * ext_scr[l, pl.ds(k * nb, rows), :]
        xh_scr[:, ls] = acc
    xh_ref[...] = xh_scr[...].astype(xh_ref.dtype)

    ext_scr[:, pl.ds(0, halo), :] = ext_scr[:, pl.ds(rows, halo), :]
    gate_pieces = _lru_gate_pieces(xh_scr, wg_ref, ba_ref, bx_ref, lam_ref,
                                   a_scr, inp_scr)
    nbs = GATE_ROWS // t
    n_dots = (nb // nbs) * (2 * width // PROJ_COLS)
    per_dot = -(-len(gate_pieces) // n_dots)
    for b0 in range(0, nb, nbs):
        hn = jnp.concatenate([_rms(x_ref[b0 + bb], g_ref[...]).astype(BF16)
                              for bb in range(nbs)], axis=0)
        for c0 in range(0, 2 * width, PROJ_COLS):
            u = jnp.dot(hn, w_ref[:, c0:c0 + PROJ_COLS], preferred_element_type=F32)
            for bb in range(nbs):
                ub = u[bb * t:(bb + 1) * t]
                if c0 < width:
                    for l in range(PROJ_COLS // LANES):
                        ext_scr[c0 // LANES + l, pl.ds(halo + b0 + bb, t, stride=nb), :] = (
                            ub[:, l * LANES:(l + 1) * LANES])
                else:
                    gb_ref[b0 + bb, :, c0 - width:c0 - width + PROJ_COLS] = (
                        ub.astype(gb_ref.dtype))
            for _ in range(per_dot):
                if gate_pieces:
                    gate_pieces.pop(0)()
    for piece in gate_pieces:
        piece()


    h = jnp.where(i > 1, carry_scr[...], 0.0)
    for s in range(rows // nb):
        h = a_scr[pl.ds(s * nb, nb), :] * h + inp_scr[pl.ds(s * nb, nb), :]
        if s == 0:
            h = jnp.where(i > 1, h, 0.0)
        inp_scr[pl.ds(s * nb, nb), :] = h
    carry_scr[...] = h
    hf_ref[...] = inp_scr[...].astype(hf_ref.dtype)

    @pl.when(i == nt)
    def _():
        ext_scr[:, pl.ds(halo, nb), :] = jnp.zeros((nslab, nb, LANES), F32)

    @pl.when(i == nt + 1)
    def _():
        xh_ref[pl.ds(2 * nb, rows - 2 * nb), :] = jnp.zeros(
            (rows - 2 * nb, width), xh_ref.dtype)
        xh_ref[pl.ds(0, 2 * nb), :] = jnp.where(
            lax.broadcasted_iota(jnp.int32, (2 * nb, width), 0) < nb,
            xh_ref[pl.ds(0, 2 * nb), :], jnp.zeros((), xh_ref.dtype))


def _lru_fwd(x, g, w_in, j, conv_w, conv_b, wg, ba, bx, lam):
    nb, seq, d = x.shape
    width = w_in.shape[2] // 2
    t = LRU_T
    r = nb * t
    nt = seq // t
    halo = (CONV_WIDTH - 1) * nb
    assert CONV_WIDTH - 1 - CONV_PAD_LEFT == 1
    tile = lambda i: (0, jnp.minimum(i, nt - 1), 0)
    win = lambda i: (jnp.maximum(i - 1, 0), 0)
    return pl.pallas_call(
        _lru_fwd_kernel,
        grid=(nt + 2,),
        in_specs=[pl.BlockSpec((nb, t, d), tile), _const_spec((1, d)),
                  _layer_spec(w_in.shape, j), _const_spec(conv_w.shape),
                  _const_spec(conv_b.shape), _const_spec(wg.shape),
                  _const_spec(ba.shape), _const_spec(bx.shape), _const_spec(lam.shape)],
        out_specs=[pl.BlockSpec((r, width), win), pl.BlockSpec((r, width), win),
                   pl.BlockSpec((nb, t, width), tile)],
        out_shape=[jax.ShapeDtypeStruct(((nt + 1) * r, width), BF16),
                   jax.ShapeDtypeStruct(((nt + 1) * r, width), BF16),
                   jax.ShapeDtypeStruct((nb, seq, width), BF16)],
        scratch_shapes=[pltpu.VMEM((width // LANES, halo + r, LANES), F32),
                        pltpu.VMEM((r, width), F32),
                        pltpu.VMEM((r, width), F32),
                        pltpu.VMEM((r, width), F32),
                        pltpu.VMEM((nb, width), F32)],
        compiler_params=_cparams("arbitrary"),
        name="lru_fwd",
    )(x, g, w_in, conv_w, conv_b, wg, ba, bx, lam)


def _lru_bwd_kernel(xh_ref, hf_ref, gb_ref, x_ref, wg_ref, ba_ref, bx_ref,
                    lam_ref, wo_ref, o_ref, a_scr, inp_scr, hb_scr, carry_scr,
                    ysum_scr, ybm_scr):
    i = pl.program_id(0)
    nb, t, _ = x_ref.shape
    rows, width = xh_ref.shape
    nsteps = rows // nb
    nslab = width // LANES

    @pl.when(i == 0)
    def _():
        carry_scr[...] = jnp.zeros_like(carry_scr)
        hb_scr[...] = jnp.zeros_like(hb_scr)
        ysum_scr[:, pl.ds(rows, nb), :] = jnp.zeros((nslab, nb, LANES), F32)

    for l in range(nslab):
        ls = slice(l * LANES, (l + 1) * LANES)
        ysum_scr[l, pl.ds(0, rows), :] = hb_scr[:, ls] + hf_ref[:, ls].astype(F32)

    for b in range(nb):
        for l in range(nslab):
            ls = slice(l * LANES, (l + 1) * LANES)
            yb = ysum_scr[l, pl.ds(nb + b, t, stride=nb), :]
            ybm_scr[pl.ds(b * t, t), ls] = (
                yb * _gelu_tanh(gb_ref[b, :, ls].astype(F32))).astype(BF16)
    ysum_scr[:, pl.ds(rows, nb), :] = ysum_scr[:, pl.ds(0, nb), :]

    gate_pieces = _lru_gate_pieces(xh_ref, wg_ref, ba_ref, bx_ref, lam_ref,
                                   a_scr, inp_scr)
    gate_pieces = [p for r0 in range(rows - GATE_ROWS, -1, -GATE_ROWS)
                   for p in gate_pieces[(r0 // GATE_ROWS) * LRU_BLOCKS:
                                        (r0 // GATE_ROWS + 1) * LRU_BLOCKS]]
    d_out = wo_ref.shape[1]
    per_dot = -(-len(gate_pieces) * OUT_COLS // d_out)
    for c0 in range(0, d_out, OUT_COLS):
        cs = slice(c0, c0 + OUT_COLS)
        res = jnp.dot(ybm_scr[...], wo_ref[:, cs], preferred_element_type=F32)
        for b in range(nb):
            o_ref[b, :, cs] = x_ref[b, :, cs] + res[b * t:(b + 1) * t]
        for _ in range(per_dot):
            if gate_pieces:
                gate_pieces.pop(0)()
    for piece in gate_pieces:
        piece()

    h = carry_scr[...]
    for s in range(nsteps - 1, -1, -1):
        h = a_scr[pl.ds(s * nb, nb), :] * h + inp_scr[pl.ds(s * nb, nb), :]
        hb_scr[pl.ds(s * nb, nb), :] = h
    carry_scr[...] = h


def _lru_bwd(xh, hf, gb, x, wg, ba, bx, lam, w_out, j):
    nb, seq, d = x.shape
    width = xh.shape[1]
    t = LRU_T
    r = nb * t
    nt = seq // t
    win_g = lambda i: (jnp.maximum(nt - i, 0), 0)
    win_s = lambda i: (jnp.minimum(nt - i + 1, nt), 0)
    tile = lambda i: (0, jnp.minimum(nt - i + 1, nt - 1), 0)
    return pl.pallas_call(
        _lru_bwd_kernel,
        grid=(nt + 2,),
        in_specs=[pl.BlockSpec((r, width), win_g), pl.BlockSpec((r, width), win_s),
                  pl.BlockSpec((nb, t, width), tile), pl.BlockSpec((nb, t, d), tile),
                  _const_spec(wg.shape), _const_spec(ba.shape), _const_spec(bx.shape),
                  _const_spec(lam.shape), _layer_spec(w_out.shape, j)],
        out_specs=pl.BlockSpec((nb, t, d), tile),
        out_shape=jax.ShapeDtypeStruct((nb, seq, d), F32),
        scratch_shapes=[pltpu.VMEM((r, width), F32),
                        pltpu.VMEM((r, width), F32),
                        pltpu.VMEM((r, width), F32),
                        pltpu.VMEM((nb, width), F32),
                        pltpu.VMEM((width // LANES, r + nb, LANES), F32),
                        pltpu.VMEM((r, width), BF16)],
        compiler_params=_cparams("arbitrary"),
        name="lru_bwd",
    )(xh, hf, gb, x, wg, ba, bx, lam, w_out)


def _ffn_chunks(hidden):
    chunks, c = [], 0
    while c < hidden:
        size = 512 if hidden - c >= 512 else hidden - c
        chunks.append((c, size))
        c += size
    return chunks


def _ffn_kernel(*refs, final_norm, fuse_proj):
    refs = list(refs)
    h_ref = refs.pop(0)
    if fuse_proj:
        y_ref, wo_ref = refs.pop(0), refs.pop(0)
    g_ref, wg_ref, wu_ref, wd_ref = (refs.pop(0) for _ in range(4))
    gf_ref = refs.pop(0) if final_norm else None
    o_ref, hn_scr, act_scr = refs[:3]
    if fuse_proj:
        res_ref = refs[3]
        res_ref[...] = h_ref[...] + jnp.dot(y_ref[...], wo_ref[...],
                                            preferred_element_type=F32)
    else:
        res_ref = h_ref
    hn_scr[...] = _rms(res_ref[...], g_ref[...]).astype(BF16)
    for c, size in _ffn_chunks(wg_ref.shape[1]):
        gt = jnp.dot(hn_scr[...], wg_ref[:, c:c + size], preferred_element_type=F32)
        up = jnp.dot(hn_scr[...], wu_ref[:, c:c + size], preferred_element_type=F32)
        act_scr[:, c:c + size] = (gt * _sigmoid(gt) * up).astype(BF16)
    y = res_ref[...] + jnp.dot(act_scr[...], wd_ref[...], preferred_element_type=F32)
    if final_norm:
        y = _rms(y, gf_ref[...])
    o_ref[...] = y


def _ffn(h, g, w_gate, w_up, w_down, layer, g_final=None, proj=None):
    rows, d = h.shape
    hidden = w_gate.shape[2]
    final_norm = g_final is not None
    fuse_proj = proj is not None
    r = FUSED_FFN_TILE if fuse_proj else ROW_TILE
    row = lambda i: (i, 0)
    in_specs = [pl.BlockSpec((r, d), row)]
    args = [h]
    scratch = [pltpu.VMEM((r, d), BF16), pltpu.VMEM((r, hidden), BF16)]
    if fuse_proj:
        y, w_out, j = proj
        in_specs += [pl.BlockSpec((r, y.shape[1]), row), _layer_spec(w_out.shape, j)]
        args += [y, w_out]
        scratch.append(pltpu.VMEM((r, d), F32))
    in_specs += [_const_spec((1, d)), _layer_spec(w_gate.shape, layer),
                 _layer_spec(w_up.shape, layer), _layer_spec(w_down.shape, layer)]
    args += [g, w_gate, w_up, w_down]
    if final_norm:
        in_specs.append(_const_spec((1, d)))
        args.append(g_final)
    return pl.pallas_call(
        functools.partial(_ffn_kernel, final_norm=final_norm, fuse_proj=fuse_proj),
        grid=(rows // r,),
        in_specs=in_specs,
        out_specs=pl.BlockSpec((r, d), row),
        out_shape=jax.ShapeDtypeStruct((rows, d), F32),
        scratch_shapes=scratch,
        compiler_params=_cparams("parallel"),
        name="ffn" + ("_proj" if fuse_proj else "") + ("_final" if final_norm else ""),
    )(*args)


def _ret_in_kernel(h_ref, g_ref, w_ref, cos_ref, sin_ref, q_ref, k_ref, v_ref,
                   sg_ref, hn_scr, *, heads):
    qk = q_ref.shape[1]
    vd = v_ref.shape[1]
    hd = qk // heads
    half = hd // 2
    hn_scr[...] = _rms(h_ref[...], g_ref[...]).astype(BF16)
    cos = cos_ref[...]
    sin = sin_ref[...]
    k_scale = hd ** -0.5
    for h in range(heads):
        for which, dst, scale in ((0, q_ref, 1.0), (1, k_ref, k_scale)):
            c0 = which * qk + h * hd
            u = jnp.dot(hn_scr[...], w_ref[:, c0:c0 + hd], preferred_element_type=F32)
            x1, x2 = u[:, :half], u[:, half:]
            r1 = x1 * cos - x2 * sin
            r2 = x2 * cos + x1 * sin
            if scale != 1.0:
                r1, r2 = r1 * scale, r2 * scale
            dst[:, h * hd:h * hd + half] = r1.astype(dst.dtype)
            dst[:, h * hd + half:(h + 1) * hd] = r2.astype(dst.dtype)
    step = 512
    for c in range(0, vd, step):
        v_ref[:, c:c + step] = jnp.dot(
            hn_scr[...], w_ref[:, 2 * qk + c:2 * qk + c + step],
            preferred_element_type=F32).astype(v_ref.dtype)
    for c in range(0, vd, step):
        gt = jnp.dot(hn_scr[...], w_ref[:, 2 * qk + vd + c:2 * qk + vd + c + step],
                     preferred_element_type=F32)
        sg_ref[:, c:c + step] = (gt * _sigmoid(gt)).astype(sg_ref.dtype)


def _ret_in(h, g, w_in, j, cos, sin, seq, heads, qk, vd):
    rows, d = h.shape
    r = ROW_TILE
    half = cos.shape[1]
    per_seq = seq // r
    pos = lambda i: (i % per_seq, 0)
    row = lambda i: (i, 0)
    return pl.pallas_call(
        functools.partial(_ret_in_kernel, heads=heads),
        grid=(rows // r,),
        in_specs=[pl.BlockSpec((r, d), row), _const_spec((1, d)),
                  _layer_spec(w_in.shape, j),
                  pl.BlockSpec((r, half), pos), pl.BlockSpec((r, half), pos)],
        out_specs=[pl.BlockSpec((r, qk), row), pl.BlockSpec((r, qk), row),
                   pl.BlockSpec((r, vd), row), pl.BlockSpec((r, vd), row)],
        out_shape=[jax.ShapeDtypeStruct((rows, qk), BF16),
                   jax.ShapeDtypeStruct((rows, qk), BF16),
                   jax.ShapeDtypeStruct((rows, vd), BF16),
                   jax.ShapeDtypeStruct((rows, vd), BF16)],
        scratch_shapes=[pltpu.VMEM((r, d), BF16)],
        compiler_params=_cparams("parallel"),
        name="ret_in",
    )(h, g, w_in, cos, sin)


def _ret_core_kernel(q_ref, k_ref, v_ref, sg_ref, dec_ref, zf_ref, zb_ref, xf_ref,
                     xb_ref, gc_ref, o_ref, sb_scr, st_scr):
    seq, dk = q_ref.shape
    dv = v_ref.shape[1]
    c = dec_ref.shape[0]
    nc = seq // c
    gcf = gc_ref[pl.ds(0, 1), :]
    gcb = gc_ref[pl.ds(1, 1), :]

    def kv_update(state, r0, z_ref, gcd):
        kz = (k_ref[pl.ds(r0, c), :].astype(F32) * z_ref[...]).astype(BF16)
        upd = lax.dot_general(kz, v_ref[pl.ds(r0, c), :], (((0,), (0,)), ((), ())),
                              preferred_element_type=F32)
        return state * gcd + upd

    st_scr[...] = jnp.zeros_like(st_scr)

    def bwd_body(s, carry):
        n = nc - 1 - s
        r0 = pl.multiple_of(n * c, c)
        sb_scr[n] = st_scr[...].astype(BF16)
        st_scr[...] = kv_update(st_scr[...], r0, zb_ref, gcb)
        return carry

    lax.fori_loop(0, nc, bwd_body, 0, unroll=4)

    st_scr[...] = jnp.zeros_like(st_scr)

    def fwd_body(n, carry):
        r0 = pl.multiple_of(n * c, c)
        q = q_ref[pl.ds(r0, c), :]
        scores = lax.dot_general(q, k_ref[pl.ds(r0, c), :], (((1,), (1,)), ((), ())),
                                 preferred_element_type=F32)
        p = (scores * dec_ref[...]).astype(BF16)
        y = jnp.dot(p, v_ref[pl.ds(r0, c), :], preferred_element_type=F32)
        y = y + xf_ref[...] * jnp.dot(q, st_scr[...].astype(BF16),
                                      preferred_element_type=F32)
        y = y + xb_ref[...] * jnp.dot(q, sb_scr[n], preferred_element_type=F32)
        st_scr[...] = kv_update(st_scr[...], r0, zf_ref, gcf)
        y = y * lax.rsqrt(jnp.mean(y * y, axis=-1, keepdims=True) + NORM_EPS)
        o_ref[pl.ds(r0, c), :] = (y * sg_ref[pl.ds(r0, c), :].astype(F32)).astype(o_ref.dtype)
        return carry

    lax.fori_loop(0, nc, fwd_body, 0, unroll=2)


def _ret_core(q, k, v, sg, tabs, heads):
    b, seq, qk = q.shape
    vd = v.shape[2]
    dk, dv = qk // heads, vd // heads
    c = RET_CHUNK
    dec, zf, zb, xf, xb, gc = tabs
    bh = lambda i, j: (i, 0, j)
    hh = lambda i, j: (j, 0, 0)
    return pl.pallas_call(
        _ret_core_kernel,
        grid=(b, heads),
        in_specs=[pl.BlockSpec((None, seq, dk), bh), pl.BlockSpec((None, seq, dk), bh),
                  pl.BlockSpec((None, seq, dv), bh), pl.BlockSpec((None, seq, dv), bh),
                  pl.BlockSpec((None, c, c), hh),
                  pl.BlockSpec((None, c, dk), hh), pl.BlockSpec((None, c, dk), hh),
                  pl.BlockSpec((None, c, dv), hh), pl.BlockSpec((None, c, dv), hh),
                  pl.BlockSpec((None, SUBLANES, dv), hh)],
        out_specs=pl.BlockSpec((None, seq, dv), bh),
        out_shape=jax.ShapeDtypeStruct((b, seq, vd), BF16),
        scratch_shapes=[pltpu.VMEM((seq // c, dk, dv), BF16),
                        pltpu.VMEM((dk, dv), F32)],
        compiler_params=_cparams("parallel", "parallel"),
        name="ret_core",
    )(q, k, v, sg, dec, zf, zb, xf, xb, gc)


def _ret_tables(heads, dk, dv):
    c = RET_CHUNK
    log_gf = jnp.log1p(-jnp.exp2(-5.0 - jnp.arange(heads, dtype=F32)))
    log_gb = log_gf[::-1]
    pos = jnp.arange(c, dtype=F32)
    diff = pos[:, None] - pos[None, :]
    lower = diff >= 0
    upper = diff < 0
    dec_f = jnp.where(lower[None], jnp.exp(jnp.where(lower, diff, 0.0)[None]
                                           * log_gf[:, None, None]), 0.0)
    dec_b = jnp.where(upper[None], jnp.exp(jnp.where(upper, -diff, 0.0)[None]
                                           * log_gb[:, None, None]), 0.0)
    dec = dec_f + dec_b
    zeta_f = jnp.exp((c - 1.0 - pos)[None, :] * log_gf[:, None])
    xi_f = jnp.exp((pos + 1.0)[None, :] * log_gf[:, None])
    zeta_b = jnp.exp(pos[None, :] * log_gb[:, None])
    xi_b = jnp.exp((c - pos)[None, :] * log_gb[:, None])
    wide = lambda t, n: jnp.broadcast_to(t[:, :, None], (heads, c, n))
    g_f = jnp.exp(c * log_gf)
    g_b = jnp.exp(c * log_gb)
    gc = jnp.zeros((heads, SUBLANES, dv), F32)
    gc = gc.at[:, 0, :].set(g_f[:, None]).at[:, 1, :].set(g_b[:, None])
    return (dec, wide(zeta_f, dk), wide(zeta_b, dk), wide(xi_f, dv), wide(xi_b, dv), gc)


def _rope_tables(seq, hd):
    half = hd // 2
    inv_freq = 1.0 / (ROPE_BASE ** jnp.linspace(0.0, 1.0, half, dtype=F32))
    ang = jnp.arange(seq, dtype=F32)[:, None] * inv_freq[None, :]
    return jnp.cos(ang), jnp.sin(ang)


def kernel(x, ln_mix, ln_ffn, ln_final, lru_w_in, lru_conv_w, lru_conv_b,
           lru_gate_a_w, lru_gate_a_b, lru_gate_x_w, lru_gate_x_b, lru_lambda,
           lru_w_out, ret_w_in, ret_w_out, ffn_w_gate, ffn_w_up, ffn_w_down):
    b, seq, d = x.shape
    depth = ln_mix.shape[0]
    rows = b * seq
    assert b == SUBLANES and seq % ROW_TILE == 0 and seq % LRU_T == 0
    row = lambda v: v.reshape(1, -1)

    lru_w_in, lru_w_out, ret_w_in, ret_w_out, ffn_w_gate, ffn_w_up, ffn_w_down = (
        w.astype(BF16) for w in (lru_w_in, lru_w_out, ret_w_in, ret_w_out,
                                 ffn_w_gate, ffn_w_up, ffn_w_down))

    h = x.reshape(rows, d)
    for layer in range(depth):
        j = layer // 2
        g_final = row(ln_final) if layer == depth - 1 else None
        proj = None
        if layer % 2 == 0:
            x3 = h.reshape(b, seq, d)
            wg = [jnp.concatenate([lru_gate_a_w[j, dr], lru_gate_x_w[j, dr]],
                                  axis=-1).astype(BF16) for dr in range(2)]
            xh, hf, gb = _lru_fwd(
                x3, row(ln_mix[layer]), lru_w_in, j, lru_conv_w[j],
                row(lru_conv_b[j]), wg[0], row(lru_gate_a_b[j, 0]),
                row(lru_gate_x_b[j, 0]), row(lru_lambda[j, 0]))
            h = _lru_bwd(xh, hf, gb, x3, wg[1], row(lru_gate_a_b[j, 1]),
                         row(lru_gate_x_b[j, 1]), row(lru_lambda[j, 1]),
                         lru_w_out, j).reshape(rows, d)
        else:
            heads = RET_HEADS
            vd = ret_w_out.shape[1]
            qk = (ret_w_in.shape[2] - 2 * vd) // 2
            cos, sin = _rope_tables(seq, qk // heads)
            q, k, v, sg = _ret_in(h, row(ln_mix[layer]), ret_w_in, j,
                                  cos, sin, seq, heads, qk, vd)
            tabs = _ret_tables(heads, qk // heads, vd // heads)
            yg = _ret_core(q.reshape(b, seq, qk), k.reshape(b, seq, qk),
                           v.reshape(b, seq, vd), sg.reshape(b, seq, vd), tabs, heads)
            proj = (yg.reshape(rows, vd), ret_w_out, j)
        h = _ffn(h, row(ln_ffn[layer]), ffn_w_gate, ffn_w_up, ffn_w_down, layer,
                 g_final, proj)
    return h.reshape(b, seq, d)
```

```python
import functools
import math

import jax
import jax.numpy as jnp
from jax import lax
from jax.experimental import pallas as pl
from jax.experimental.pallas import tpu as pltpu

F32 = jnp.float32
BF16 = jnp.bfloat16

NORM_EPS = 1e-6
RSQRT_FLOOR = 1e-30
RG_LRU_C = 8.0
LRU_BLOCKS = 4
CONV_WIDTH = 4
CONV_PAD_LEFT = 2
RET_HEADS = 4
ROPE_BASE = 10000.0

SUBLANES = 8
LANES = 128
VMEM_LIMIT_BYTES = 56 * 1024 * 1024
ROW_TILE = 1024
FUSED_FFN_TILE = 512
LRU_T = 64
GATE_ROWS = 256
PROJ_COLS = 1024
OUT_COLS = 256
RET_CHUNK = 256
STATE_UNROLL = 3


def _cparams(*sem):
    return pltpu.CompilerParams(dimension_semantics=sem,
                                vmem_limit_bytes=VMEM_LIMIT_BYTES)


def _const_spec(shape):
    nd = len(shape)
    return pl.BlockSpec(shape, lambda *_: (0,) * nd, pipeline_mode=pl.Buffered(1))


def _layer_spec(shape, layer):
    nd = len(shape)
    return pl.BlockSpec((None,) + tuple(shape[1:]), lambda *_: (layer,) + (0,) * (nd - 1),
                        pipeline_mode=pl.Buffered(1))


def _rms(x, g):
    ms = jnp.mean(x * x, axis=-1, keepdims=True)
    return x * lax.rsqrt(ms + NORM_EPS) * g


def _sigmoid(x):
    return 0.5 * jnp.tanh(0.5 * x) + 0.5


def _gelu_tanh(x):
    c = math.sqrt(2.0 / math.pi)
    hx = 0.5 * x
    return hx * jnp.tanh(x * ((c * 0.044715) * (x * x) + c)) + hx


def _softplus(x):
    return jnp.maximum(x, 0.0) + jnp.log(1.0 + jnp.exp(-jnp.abs(x)))


def _lru_gate_pieces(xh_ref, wg_ref, ba_ref, bx_ref, lam_ref, a_dst, inp_dst):
    rows, width = xh_ref.shape
    bw = width // LRU_BLOCKS
    c2 = (-0.5 * RG_LRU_C * math.log2(math.e)) * _softplus(-lam_ref[...])
    ba2 = 0.5 * ba_ref[...]
    bx2 = 0.5 * bx_ref[...]

    def piece(r0, n):
        cs = slice(n * bw, (n + 1) * bw)
        xh = xh_ref[pl.ds(r0, GATE_ROWS), cs]
        z = jnp.dot(xh.astype(BF16), wg_ref[n], preferred_element_type=F32)
        tr = jnp.tanh(z[:, :bw] + ba2[:, cs])
        ti = jnp.tanh(z[:, bw:] + bx2[:, cs])
        a = jnp.exp2(c2[:, cs] * (tr + 1.0))
        om = 1.0 - a * a
        root = om * lax.rsqrt(jnp.maximum(om, RSQRT_FLOOR))
        a_dst[pl.ds(r0, GATE_ROWS), cs] = a
        inp_dst[pl.ds(r0, GATE_ROWS), cs] = ((ti + 1.0) * xh.astype(F32)) * root

    return [functools.partial(piece, r0, n)
            for r0 in range(0, rows, GATE_ROWS) for n in range(LRU_BLOCKS)]


def _lru_fwd_kernel(x_ref, g_ref, w_ref, cw_ref, cb_ref, wg_ref, ba_ref, bx_ref,
                    lam_ref, xh_ref, hf_ref, gb_ref, ext_scr, xh_scr,
                    a_scr, inp_scr, carry_scr):
    i = pl.program_id(0)
    nt = pl.num_programs(0) - 2
    nb, t, _ = x_ref.shape
    rows, width = xh_ref.shape
    halo = (CONV_WIDTH - 1) * nb
    nslab = width // LANES

    @pl.when(i == 0)
    def _():
        carry_scr[...] = jnp.zeros_like(carry_scr)
        ext_scr[...] = jnp.zeros_like(ext_scr)

    cwh = 0.5 * cw_ref[...]
    cbh = 0.5 * cb_ref[...]
    for l in range(nslab):
        ls = slice(l * LANES, (l + 1) * LANES)
        acc = cbh[:, ls] + cwh[0:1, ls] * ext_scr[l, pl.ds(0, rows), :]
        for k in range(1, CONV_WIDTH):
            acc = acc + cwh[k:k + 1, ls] * ext_scr[l, pl.ds(k * nb, rows), :]
        xh_scr[:, ls] = acc
    xh_ref[...] = xh_scr[...].astype(xh_ref.dtype)

    ext_scr[:, pl.ds(0, halo), :] = ext_scr[:, pl.ds(rows, halo), :]
    gate_pieces = _lru_gate_pieces(xh_scr, wg_ref, ba_ref, bx_ref, lam_ref,
                                   a_scr, inp_scr)
    nbs = GATE_ROWS // t
    n_dots = (nb // nbs) * (2 * width // PROJ_COLS)
    per_dot = -(-len(gate_pieces) // n_dots)
    for b0 in range(0, nb, nbs):
        hn = jnp.concatenate([_rms(x_ref[b0 + bb], g_ref[...]).astype(BF16)
                              for bb in range(nbs)], axis=0)
        for c0 in range(0, 2 * width, PROJ_COLS):
            u = jnp.dot(hn, w_ref[:, c0:c0 + PROJ_COLS], preferred_element_type=F32)
            for bb in range(nbs):
                ub = u[bb * t:(bb + 1) * t]
                if c0 < width:
                    for l in range(PROJ_COLS // LANES):
                        ext_scr[c0 // LANES + l, pl.ds(halo + b0 + bb, t, stride=nb), :] = (
                            ub[:, l * LANES:(l + 1) * LANES])
                else:
                    gb_ref[b0 + bb, :, c0 - width:c0 - width + PROJ_COLS] = (
                        ub.astype(gb_ref.dtype))
            for _ in range(per_dot):
                if gate_pieces:
                    gate_pieces.pop(0)()
    for piece in gate_pieces:
        piece()


    h = jnp.where(i > 1, carry_scr[...], 0.0)
    for s in range(rows // nb):
        h = a_scr[pl.ds(s * nb, nb), :] * h + inp_scr[pl.ds(s * nb, nb), :]
        if s == 0:
            h = jnp.where(i > 1, h, 0.0)
        inp_scr[pl.ds(s * nb, nb), :] = h
    carry_scr[...] = h
    hf_ref[...] = inp_scr[...].astype(hf_ref.dtype)

    @pl.when(i == nt)
    def _():
        ext_scr[:, pl.ds(halo, nb), :] = jnp.zeros((nslab, nb, LANES), F32)

    @pl.when(i == nt + 1)
    def _():
        xh_ref[pl.ds(2 * nb, rows - 2 * nb), :] = jnp.zeros(
            (rows - 2 * nb, width), xh_ref.dtype)
        xh_ref[pl.ds(0, 2 * nb), :] = jnp.where(
            lax.broadcasted_iota(jnp.int32, (2 * nb, width), 0) < nb,
            xh_ref[pl.ds(0, 2 * nb), :], jnp.zeros((), xh_ref.dtype))


def _lru_fwd(x, g, w_in, j, conv_w, conv_b, wg, ba, bx, lam):
    nb, seq, d = x.shape
    width = w_in.shape[2] // 2
    t = LRU_T
    r = nb * t
    nt = seq // t
    halo = (CONV_WIDTH - 1) * nb
    assert CONV_WIDTH - 1 - CONV_PAD_LEFT == 1
    tile = lambda i: (0, jnp.minimum(i, nt - 1), 0)
    win = lambda i: (jnp.maximum(i - 1, 0), 0)
    return pl.pallas_call(
        _lru_fwd_kernel,
        grid=(nt + 2,),
        in_specs=[pl.BlockSpec((nb, t, d), tile), _const_spec((1, d)),
                  _layer_spec(w_in.shape, j), _const_spec(conv_w.shape),
                  _const_spec(conv_b.shape), _const_spec(wg.shape),
                  _const_spec(ba.shape), _const_spec(bx.shape), _const_spec(lam.shape)],
        out_specs=[pl.BlockSpec((r, width), win), pl.BlockSpec((r, width), win),
                   pl.BlockSpec((nb, t, width), tile)],
        out_shape=[jax.ShapeDtypeStruct(((nt + 1) * r, width), BF16),
                   jax.ShapeDtypeStruct(((nt + 1) * r, width), BF16),
                   jax.ShapeDtypeStruct((nb, seq, width), BF16)],
        scratch_shapes=[pltpu.VMEM((width // LANES, halo + r, LANES), F32),
                        pltpu.VMEM((r, width), F32),
                        pltpu.VMEM((r, width), F32),
                        pltpu.VMEM((r, width), F32),
                        pltpu.VMEM((nb, width), F32)],
        compiler_params=_cparams("arbitrary"),
        name="lru_fwd",
    )(x, g, w_in, conv_w, conv_b, wg, ba, bx, lam)


def _lru_bwd_kernel(xh_ref, hf_ref, gb_ref, x_ref, wg_ref, ba_ref, bx_ref,
                    lam_ref, wo_ref, o_ref, a_scr, inp_scr, hb_scr, carry_scr,
                    ysum_scr, ybm_scr):
    i = pl.program_id(0)
    nb, t, _ = x_ref.shape
    rows, width = xh_ref.shape
    nsteps = rows // nb
    nslab = width // LANES

    @pl.when(i == 0)
    def _():
        carry_scr[...] = jnp.zeros_like(carry_scr)
        hb_scr[...] = jnp.zeros_like(hb_scr)
        ysum_scr[:, pl.ds(rows, nb), :] = jnp.zeros((nslab, nb, LANES), F32)

    for l in range(nslab):
        ls = slice(l * LANES, (l + 1) * LANES)
        ysum_scr[l, pl.ds(0, rows), :] = hb_scr[:, ls] + hf_ref[:, ls].astype(F32)

    for b in range(nb):
        for l in range(nslab):
            ls = slice(l * LANES, (l + 1) * LANES)
            yb = ysum_scr[l, pl.ds(nb + b, t, stride=nb), :]
            ybm_scr[pl.ds(b * t, t), ls] = (
                yb * _gelu_tanh(gb_ref[b, :, ls].astype(F32))).astype(BF16)
    ysum_scr[:, pl.ds(rows, nb), :] = ysum_scr[:, pl.ds(0, nb), :]

    gate_pieces = _lru_gate_pieces(xh_ref, wg_ref, ba_ref, bx_ref, lam_ref,
                                   a_scr, inp_scr)
    gate_pieces = [p for r0 in range(rows - GATE_ROWS, -1, -GATE_ROWS)
                   for p in gate_pieces[(r0 // GATE_ROWS) * LRU_BLOCKS:
                                        (r0 // GATE_ROWS + 1) * LRU_BLOCKS]]
    d_out = wo_ref.shape[1]
    per_dot = -(-len(gate_pieces) * OUT_COLS // d_out)
    for c0 in range(0, d_out, OUT_COLS):
        cs = slice(c0, c0 + OUT_COLS)
        res = jnp.dot(ybm_scr[...], wo_ref[:, cs], preferred_element_type=F32)
        for b in range(nb):
            o_ref[b, :, cs] = x_ref[b, :, cs] + res[b * t:(b + 1) * t]
        for _ in range(per_dot):
            if gate_pieces:
                gate_pieces.pop(0)()
    for piece in gate_pieces:
        piece()

    h = carry_scr[...]
    for s in range(nsteps - 1, -1, -1):
        h = a_scr[pl.ds(s * nb, nb), :] * h + inp_scr[pl.ds(s * nb, nb), :]
        hb_scr[pl.ds(s * nb, nb), :] = h
    carry_scr[...] = h


def _lru_bwd(xh, hf, gb, x, wg, ba, bx, lam, w_out, j):
    nb, seq, d = x.shape
    width = xh.shape[1]
    t = LRU_T
    r = nb * t
    nt = seq // t
    win_g = lambda i: (jnp.maximum(nt - i, 0), 0)
    win_s = lambda i: (jnp.minimum(nt - i + 1, nt), 0)
    tile = lambda i: (0, jnp.minimum(nt - i + 1, nt - 1), 0)
    return pl.pallas_call(
        _lru_bwd_kernel,
        grid=(nt + 2,),
        in_specs=[pl.BlockSpec((r, width), win_g), pl.BlockSpec((r, width), win_s),
                  pl.BlockSpec((nb, t, width), tile), pl.BlockSpec((nb, t, d), tile),
                  _const_spec(wg.shape), _const_spec(ba.shape), _const_spec(bx.shape),
                  _const_spec(lam.shape), _layer_spec(w_out.shape, j)],
        out_specs=pl.BlockSpec((nb, t, d), tile),
        out_shape=jax.ShapeDtypeStruct((nb, seq, d), F32),
        scratch_shapes=[pltpu.VMEM((r, width), F32),
                        pltpu.VMEM((r, width), F32),
                        pltpu.VMEM((r, width), F32),
                        pltpu.VMEM((nb, width), F32),
                        pltpu.VMEM((width // LANES, r + nb, LANES), F32),
                        pltpu.VMEM((r, width), BF16)],
        compiler_params=_cparams("arbitrary"),
        name="lru_bwd",
    )(xh, hf, gb, x, wg, ba, bx, lam, w_out)


def _ffn_chunks(hidden):
    chunks, c = [], 0
    while c < hidden:
        size = 512 if hidden - c >= 512 else hidden - c
        chunks.append((c, size))
        c += size
    return chunks


def _ffn_kernel(*refs, final_norm, fuse_proj):
    refs = list(refs)
    h_ref = refs.pop(0)
    if fuse_proj:
        y_ref, wo_ref = refs.pop(0), refs.pop(0)
    g_ref, wg_ref, wu_ref, wd_ref = (refs.pop(0) for _ in range(4))
    gf_ref = refs.pop(0) if final_norm else None
    o_ref, hn_scr, act_scr = refs[:3]
    if fuse_proj:
        res_ref = refs[3]
        res_ref[...] = h_ref[...] + jnp.dot(y_ref[...], wo_ref[...],
                                            preferred_element_type=F32)
    else:
        res_ref = h_ref
    hn_scr[...] = _rms(res_ref[...], g_ref[...]).astype(BF16)
    for c, size in _ffn_chunks(wg_ref.shape[1]):
        gt = jnp.dot(hn_scr[...], wg_ref[:, c:c + size], preferred_element_type=F32)
        up = jnp.dot(hn_scr[...], wu_ref[:, c:c + size], preferred_element_type=F32)
        act_scr[:, c:c + size] = (gt * _sigmoid(gt) * up).astype(BF16)
    y = res_ref[...] + jnp.dot(act_scr[...], wd_ref[...], preferred_element_type=F32)
    if final_norm:
        y = _rms(y, gf_ref[...])
    o_ref[...] = y


def _ffn(h, g, w_gate, w_up, w_down, layer, g_final=None, proj=None):
    rows, d = h.shape
    hidden = w_gate.shape[2]
    final_norm = g_final is not None
    fuse_proj = proj is not None
    r = FUSED_FFN_TILE if fuse_proj else ROW_TILE
    row = lambda i: (i, 0)
    in_specs = [pl.BlockSpec((r, d), row)]
    args = [h]
    scratch = [pltpu.VMEM((r, d), BF16), pltpu.VMEM((r, hidden), BF16)]
    if fuse_proj:
        y, w_out, j = proj
        in_specs += [pl.BlockSpec((r, y.shape[1]), row), _layer_spec(w_out.shape, j)]
        args += [y, w_out]
        scratch.append(pltpu.VMEM((r, d), F32))
    in_specs += [_const_spec((1, d)), _layer_spec(w_gate.shape, layer),
                 _layer_spec(w_up.shape, layer), _layer_spec(w_down.shape, layer)]
    args += [g, w_gate, w_up, w_down]
    if final_norm:
        in_specs.append(_const_spec((1, d)))
        args.append(g_final)
    return pl.pallas_call(
        functools.partial(_ffn_kernel, final_norm=final_norm, fuse_proj=fuse_proj),
        grid=(rows // r,),
        in_specs=in_specs,
        out_specs=pl.BlockSpec((r, d), row),
        out_shape=jax.ShapeDtypeStruct((rows, d), F32),
        scratch_shapes=scratch,
        compiler_params=_cparams("parallel"),
        name="ffn" + ("_proj" if fuse_proj else "") + ("_final" if final_norm else ""),
    )(*args)


def _ret_in_kernel(h_ref, g_ref, w_ref, cos_ref, sin_ref, q_ref, k_ref, v_ref,
                   sg_ref, hn_scr, *, heads):
    qk = q_ref.shape[1]
    vd = v_ref.shape[1]
    hd = qk // heads
    half = hd // 2
    hn_scr[...] = _rms(h_ref[...], g_ref[...]).astype(BF16)
    cos = cos_ref[...]
    sin = sin_ref[...]
    k_scale = hd ** -0.5
    for h in range(heads):
        for which, dst, scale in ((0, q_ref, 1.0), (1, k_ref, k_scale)):
            c0 = which * qk + h * hd
            u = jnp.dot(hn_scr[...], w_ref[:, c0:c0 + hd], preferred_element_type=F32)
            x1, x2 = u[:, :half], u[:, half:]
            r1 = x1 * cos - x2 * sin
            r2 = x2 * cos + x1 * sin
            if scale != 1.0:
                r1, r2 = r1 * scale, r2 * scale
            dst[:, h * hd:h * hd + half] = r1.astype(dst.dtype)
            dst[:, h * hd + half:(h + 1) * hd] = r2.astype(dst.dtype)
    step = 512
    for c in range(0, vd, step):
        v_ref[:, c:c + step] = jnp.dot(
            hn_scr[...], w_ref[:, 2 * qk + c:2 * qk + c + step],
            preferred_element_type=F32).astype(v_ref.dtype)
    for c in range(0, vd, step):
        gt = jnp.dot(hn_scr[...], w_ref[:, 2 * qk + vd + c:2 * qk + vd + c + step],
                     preferred_element_type=F32)
        sg_ref[:, c:c + step] = (gt * _sigmoid(gt)).astype(sg_ref.dtype)


def _ret_in(h, g, w_in, j, cos, sin, seq, heads, qk, vd):
    rows, d = h.shape
    r = ROW_TILE
    half = cos.shape[1]
    per_seq = seq // r
    pos = lambda i: (i % per_seq, 0)
    row = lambda i: (i, 0)
    return pl.pallas_call(
        functools.partial(_ret_in_kernel, heads=heads),
        grid=(rows // r,),
        in_specs=[pl.BlockSpec((r, d), row), _const_spec((1, d)),
                  _layer_spec(w_in.shape, j),
                  pl.BlockSpec((r, half), pos), pl.BlockSpec((r, half), pos)],
        out_specs=[pl.BlockSpec((r, qk), row), pl.BlockSpec((r, qk), row),
                   pl.BlockSpec((r, vd), row), pl.BlockSpec((r, vd), row)],
        out_shape=[jax.ShapeDtypeStruct((rows, qk), BF16),
                   jax.ShapeDtypeStruct((rows, qk), BF16),
                   jax.ShapeDtypeStruct((rows, vd), BF16),
                   jax.ShapeDtypeStruct((rows, vd), BF16)],
        scratch_shapes=[pltpu.VMEM((r, d), BF16)],
        compiler_params=_cparams("parallel"),
        name="ret_in",
    )(h, g, w_in, cos, sin)


def _ret_core_kernel(q_ref, k_ref, v_ref, sg_ref, dec_ref, zf_ref, zb_ref, xf_ref,
                     xb_ref, gc_ref, o_ref, sf_scr, sb_scr, stf_scr, stb_scr, y_scr):
    seq, dk = q_ref.shape
    dv = v_ref.shape[1]
    c = dec_ref.shape[0]
    nc = seq // c
    gcf = gc_ref[pl.ds(0, 1), :]
    gcb = gc_ref[pl.ds(1, 1), :]

    def kv_update(state, n, z_ref, gcd):
        r0 = pl.multiple_of(n * c, c)
        kz = (k_ref[pl.ds(r0, c), :].astype(F32) * z_ref[...]).astype(BF16)
        upd = lax.dot_general(kz, v_ref[pl.ds(r0, c), :], (((0,), (0,)), ((), ())),
                              preferred_element_type=F32)
        return state * gcd + upd

    stf_scr[...] = jnp.zeros_like(stf_scr)
    stb_scr[...] = jnp.zeros_like(stb_scr)
    sf_scr[0] = jnp.zeros((dk, dv), BF16)
    sb_scr[nc - 1] = jnp.zeros((dk, dv), BF16)

    def state_body(s, carry):
        stf_scr[...] = kv_update(stf_scr[...], s, zf_ref, gcf)
        sf_scr[s + 1] = stf_scr[...].astype(BF16)
        stb_scr[...] = kv_update(stb_scr[...], nc - 1 - s, zb_ref, gcb)
        sb_scr[nc - 2 - s] = stb_scr[...].astype(BF16)
        return carry

    lax.fori_loop(0, nc - 1, state_body, 0, unroll=STATE_UNROLL)

    def finish(n):
        r0 = pl.multiple_of(n * c, c)
        y = y_scr[...]
        y = y * lax.rsqrt(jnp.mean(y * y, axis=-1, keepdims=True) + NORM_EPS)
        o_ref[pl.ds(r0, c), :] = (y * sg_ref[pl.ds(r0, c), :].astype(F32)).astype(o_ref.dtype)

    def out_body(n, carry):
        finish(jnp.maximum(n - 1, 0))
        r0 = pl.multiple_of(n * c, c)
        q = q_ref[pl.ds(r0, c), :]
        scores = lax.dot_general(q, k_ref[pl.ds(r0, c), :], (((1,), (1,)), ((), ())),
                                 preferred_element_type=F32)
        y = xf_ref[...] * jnp.dot(q, sf_scr[n], preferred_element_type=F32)
        y = y + xb_ref[...] * jnp.dot(q, sb_scr[n], preferred_element_type=F32)
        p = (scores * dec_ref[...]).astype(BF16)
        y_scr[...] = y + jnp.dot(p, v_ref[pl.ds(r0, c), :], preferred_element_type=F32)
        return carry

    y_scr[...] = jnp.zeros_like(y_scr)
    lax.fori_loop(0, nc, out_body, 0, unroll=4)
    finish(nc - 1)


def _ret_core(q, k, v, sg, tabs, heads):
    b, seq, qk = q.shape
    vd = v.shape[2]
    dk, dv = qk // heads, vd // heads
    c = RET_CHUNK
    dec, zf, zb, xf, xb, gc = tabs
    bh = lambda i, j: (i, 0, j)
    hh = lambda i, j: (j, 0, 0)
    return pl.pallas_call(
        _ret_core_kernel,
        grid=(b, heads),
        in_specs=[pl.BlockSpec((None, seq, dk), bh), pl.BlockSpec((None, seq, dk), bh),
                  pl.BlockSpec((None, seq, dv), bh), pl.BlockSpec((None, seq, dv), bh),
                  pl.BlockSpec((None, c, c), hh),
                  pl.BlockSpec((None, c, dk), hh), pl.BlockSpec((None, c, dk), hh),
                  pl.BlockSpec((None, c, dv), hh), pl.BlockSpec((None, c, dv), hh),
                  pl.BlockSpec((None, SUBLANES, dv), hh)],
        out_specs=pl.BlockSpec((None, seq, dv), bh),
        out_shape=jax.ShapeDtypeStruct((b, seq, vd), BF16),
        scratch_shapes=[pltpu.VMEM((seq // c, dk, dv), BF16),
                        pltpu.VMEM((seq // c, dk, dv), BF16),
                        pltpu.VMEM((dk, dv), F32),
                        pltpu.VMEM((dk, dv), F32),
                        pltpu.VMEM((c, dv), F32)],
        compiler_params=_cparams("parallel", "parallel"),
        name="ret_core",
    )(q, k, v, sg, dec, zf, zb, xf, xb, gc)


def _ret_tables(heads, dk, dv):
    c = RET_CHUNK
    log_gf = jnp.log1p(-jnp.exp2(-5.0 - jnp.arange(heads, dtype=F32)))
    log_gb = log_gf[::-1]
    pos = jnp.arange(c, dtype=F32)
    diff = pos[:, None] - pos[None, :]
    lower = diff >= 0
    upper = diff < 0
    dec_f = jnp.where(lower[None], jnp.exp(jnp.where(lower, diff, 0.0)[None]
                                           * log_gf[:, None, None]), 0.0)
    dec_b = jnp.where(upper[None], jnp.exp(jnp.where(upper, -diff, 0.0)[None]
                                           * log_gb[:, None, None]), 0.0)
    dec = dec_f + dec_b
    zeta_f = jnp.exp((c - 1.0 - pos)[None, :] * log_gf[:, None])
    xi_f = jnp.exp((pos + 1.0)[None, :] * log_gf[:, None])
    zeta_b = jnp.exp(pos[None, :] * log_gb[:, None])
    xi_b = jnp.exp((c - pos)[None, :] * log_gb[:, None])
    wide = lambda t, n: jnp.broadcast_to(t[:, :, None], (heads, c, n))
    g_f = jnp.exp(c * log_gf)
    g_b = jnp.exp(c * log_gb)
    gc = jnp.zeros((heads, SUBLANES, dv), F32)
    gc = gc.at[:, 0, :].set(g_f[:, None]).at[:, 1, :].set(g_b[:, None])
    return (dec, wide(zeta_f, dk), wide(zeta_b, dk), wide(xi_f, dv), wide(xi_b, dv), gc)


def _rope_tables(seq, hd):
    half = hd // 2
    inv_freq = 1.0 / (ROPE_BASE ** jnp.linspace(0.0, 1.0, half, dtype=F32))
    ang = jnp.arange(seq, dtype=F32)[:, None] * inv_freq[None, :]
    return jnp.cos(ang), jnp.sin(ang)


def kernel(x, ln_mix, ln_ffn, ln_final, lru_w_in, lru_conv_w, lru_conv_b,
           lru_gate_a_w, lru_gate_a_b, lru_gate_x_w, lru_gate_x_b, lru_lambda,
           lru_w_out, ret_w_in, ret_w_out, ffn_w_gate, ffn_w_up, ffn_w_down):
    b, seq, d = x.shape
    depth = ln_mix.shape[0]
    rows = b * seq
    assert b == SUBLANES and seq % ROW_TILE == 0 and seq % LRU_T == 0
    row = lambda v: v.reshape(1, -1)

    lru_w_in, lru_w_out, ret_w_in, ret_w_out, ffn_w_gate, ffn_w_up, ffn_w_down = (
        w.astype(BF16) for w in (lru_w_in, lru_w_out, ret_w_in, ret_w_out,
                                 ffn_w_gate, ffn_w_up, ffn_w_down))

    h = x.reshape(rows, d)
    for layer in range(depth):
        j = layer // 2
        g_final = row(ln_final) if layer == depth - 1 else None
        proj = None
        if layer % 2 == 0:
            x3 = h.reshape(b, seq, d)
            wg = [jnp.concatenate([lru_gate_a_w[j, dr], lru_gate_x_w[j, dr]],
                                  axis=-1).astype(BF16) for dr in range(2)]
            xh, hf, gb = _lru_fwd(
                x3, row(ln_mix[layer]), lru_w_in, j, lru_conv_w[j],
                row(lru_conv_b[j]), wg[0], row(lru_gate_a_b[j, 0]),
                row(lru_gate_x_b[j, 0]), row(lru_lambda[j, 0]))
            h = _lru_bwd(xh, hf, gb, x3, wg[1], row(lru_gate_a_b[j, 1]),
                         row(lru_gate_x_b[j, 1]), row(lru_lambda[j, 1]),
                         lru_w_out, j).reshape(rows, d)
        else:
            heads = RET_HEADS
            vd = ret_w_out.shape[1]
            qk = (ret_w_in.shape[2] - 2 * vd) // 2
            cos, sin = _rope_tables(seq, qk // heads)
            q, k, v, sg = _ret_in(h, row(ln_mix[layer]), ret_w_in, j,
                                  cos, sin, seq, heads, qk, vd)
            tabs = _ret_tables(heads, qk // heads, vd // heads)
            yg = _ret_core(q.reshape(b, seq, qk), k.reshape(b, seq, qk),
                           v.reshape(b, seq, vd), sg.reshape(b, seq, vd), tabs, heads)
            proj = (yg.reshape(rows, vd), ret_w_out, j)
        h = _ffn(h, row(ln_ffn[layer]), ffn_w_gate, ffn_w_up, ffn_w_down, layer,
                 g_final, proj)
    return h.reshape(b, seq, d)
```

```python
import functools
import math

import jax
import jax.numpy as jnp
from jax import lax
from jax.experimental import pallas as pl
from jax.experimental.pallas import tpu as pltpu

F32 = jnp.float32
BF16 = jnp.bfloat16

NORM_EPS = 1e-6
RSQRT_FLOOR = 1e-30
RG_LRU_C = 8.0
LRU_BLOCKS = 4
CONV_WIDTH = 4
CONV_PAD_LEFT = 2
RET_HEADS = 4
ROPE_BASE = 10000.0

SUBLANES = 8
LANES = 128
VMEM_LIMIT_BYTES = 56 * 1024 * 1024
ROW_TILE = 1024
FUSED_FFN_TILE = 512
LRU_T = 128
GATE_ROWS = 256
PROJ_COLS = 1024
OUT_COLS = 256
RET_CHUNK = 256
STATE_UNROLL = 3


def _cparams(*sem):
    return pltpu.CompilerParams(dimension_semantics=sem,
                                vmem_limit_bytes=VMEM_LIMIT_BYTES)


def _const_spec(shape):
    nd = len(shape)
    return pl.BlockSpec(shape, lambda *_: (0,) * nd, pipeline_mode=pl.Buffered(1))


def _layer_spec(shape, layer):
    nd = len(shape)
    return pl.BlockSpec((None,) + tuple(shape[1:]), lambda *_: (layer,) + (0,) * (nd - 1),
                        pipeline_mode=pl.Buffered(1))


def _rms(x, g):
    ms = jnp.mean(x * x, axis=-1, keepdims=True)
    return x * lax.rsqrt(ms + NORM_EPS) * g


def _sigmoid(x):
    return 0.5 * jnp.tanh(0.5 * x) + 0.5


def _gelu_tanh(x):
    c = math.sqrt(2.0 / math.pi)
    hx = 0.5 * x
    return hx * jnp.tanh(x * ((c * 0.044715) * (x * x) + c)) + hx


def _softplus(x):
    return jnp.maximum(x, 0.0) + jnp.log(1.0 + jnp.exp(-jnp.abs(x)))


def _lru_gate_pieces(xh_ref, wg_ref, ba_ref, bx_ref, lam_ref, a_dst, inp_dst):
    rows, width = xh_ref.shape
    bw = width // LRU_BLOCKS
    c2 = (-0.5 * RG_LRU_C * math.log2(math.e)) * _softplus(-lam_ref[...])
    ba2 = 0.5 * ba_ref[...]
    bx2 = 0.5 * bx_ref[...]

    def piece(r0, n):
        cs = slice(n * bw, (n + 1) * bw)
        xh = xh_ref[pl.ds(r0, GATE_ROWS), cs]
        z = jnp.dot(xh.astype(BF16), wg_ref[n], preferred_element_type=F32)
        tr = jnp.tanh(z[:, :bw] + ba2[:, cs])
        ti = jnp.tanh(z[:, bw:] + bx2[:, cs])
        a = jnp.exp2(c2[:, cs] * (tr + 1.0))
        om = 1.0 - a * a
        root = om * lax.rsqrt(jnp.maximum(om, RSQRT_FLOOR))
        a_dst[pl.ds(r0, GATE_ROWS), cs] = a
        inp_dst[pl.ds(r0, GATE_ROWS), cs] = ((ti + 1.0) * xh.astype(F32)) * root

    return [functools.partial(piece, r0, n)
            for r0 in range(0, rows, GATE_ROWS) for n in range(LRU_BLOCKS)]


def _lru_fwd_kernel(x_ref, g_ref, w_ref, cw_ref, cb_ref, wg_ref, ba_ref, bx_ref,
                    lam_ref, xh_ref, hf_ref, gb_ref, ext_scr, xh_scr,
                    a_scr, inp_scr, carry_scr):
    i = pl.program_id(0)
    nt = pl.num_programs(0) - 2
    nb, t, _ = x_ref.shape
    rows, width = xh_ref.shape
    halo = (CONV_WIDTH - 1) * nb
    nslab = width // LANES

    @pl.when(i == 0)
    def _():
        carry_scr[...] = jnp.zeros_like(carry_scr)
        ext_scr[...] = jnp.zeros_like(ext_scr)

    cwh = 0.5 * cw_ref[...]
    cbh = 0.5 * cb_ref[...]
    for l in range(nslab):
        ls = slice(l * LANES, (l + 1) * LANES)
        acc = cbh[:, ls] + cwh[0:1, ls] * ext_scr[l, pl.ds(0, rows), :]
        for k in range(1, CONV_WIDTH):
            acc = acc + cwh[k:k + 1, ls] * ext_scr[l, pl.ds(k * nb, rows), :]
        xh_scr[:, ls] = acc
    xh_ref[...] = xh_scr[...].astype(xh_ref.dtype)

    ext_scr[:, pl.ds(0, halo), :] = ext_scr[:, pl.ds(rows, halo), :]
    gate_pieces = _lru_gate_pieces(xh_scr, wg_ref, ba_ref, bx_ref, lam_ref,
                                   a_scr, inp_scr)
    nbs = GATE_ROWS // t
    n_dots = (nb // nbs) * (2 * width // PROJ_COLS)
    per_dot = -(-len(gate_pieces) // n_dots)
    for b0 in range(0, nb, nbs):
        hn = jnp.concatenate([_rms(x_ref[b0 + bb], g_ref[...]).astype(BF16)
                              for bb in range(nbs)], axis=0)
        for c0 in range(0, 2 * width, PROJ_COLS):
            u = jnp.dot(hn, w_ref[:, c0:c0 + PROJ_COLS], preferred_element_type=F32)
            for bb in range(nbs):
                ub = u[bb * t:(bb + 1) * t]
                if c0 < width:
                    for l in range(PROJ_COLS // LANES):
                        ext_scr[c0 // LANES + l, pl.ds(halo + b0 + bb, t, stride=nb), :] = (
                            ub[:, l * LANES:(l + 1) * LANES])
                else:
                    gb_ref[b0 + bb, :, c0 - width:c0 - width + PROJ_COLS] = (
                        ub.astype(gb_ref.dtype))
            for _ in range(per_dot):
                if gate_pieces:
                    gate_pieces.pop(0)()
    for piece in gate_pieces:
        piece()

    h = jnp.where(i > 1, carry_scr[...], 0.0)
    for s in range(rows // nb):
        h = a_scr[pl.ds(s * nb, nb), :] * h + inp_scr[pl.ds(s * nb, nb), :]
        if s == 0:
            h = jnp.where(i > 1, h, 0.0)
        inp_scr[pl.ds(s * nb, nb), :] = h
    carry_scr[...] = h
    hf_ref[...] = inp_scr[...].astype(hf_ref.dtype)

    @pl.when(i == nt)
    def _():
        ext_scr[:, pl.ds(halo, nb), :] = jnp.zeros((nslab, nb, LANES), F32)

    @pl.when(i == nt + 1)
    def _():
        xh_ref[pl.ds(2 * nb, rows - 2 * nb), :] = jnp.zeros(
            (rows - 2 * nb, width), xh_ref.dtype)
        xh_ref[pl.ds(0, 2 * nb), :] = jnp.where(
            lax.broadcasted_iota(jnp.int32, (2 * nb, width), 0) < nb,
            xh_ref[pl.ds(0, 2 * nb), :], jnp.zeros((), xh_ref.dtype))


def _lru_fwd(x, g, w_in, j, conv_w, conv_b, wg, ba, bx, lam):
    nb, seq, d = x.shape
    width = w_in.shape[2] // 2
    t = LRU_T
    r = nb * t
    nt = seq // t
    halo = (CONV_WIDTH - 1) * nb
    assert CONV_WIDTH - 1 - CONV_PAD_LEFT == 1
    tile = lambda i: (0, jnp.minimum(i, nt - 1), 0)
    win = lambda i: (jnp.maximum(i - 1, 0), 0)
    return pl.pallas_call(
        _lru_fwd_kernel,
        grid=(nt + 2,),
        in_specs=[pl.BlockSpec((nb, t, d), tile), _const_spec((1, d)),
                  _layer_spec(w_in.shape, j), _const_spec(conv_w.shape),
                  _const_spec(conv_b.shape), _const_spec(wg.shape),
                  _const_spec(ba.shape), _const_spec(bx.shape), _const_spec(lam.shape)],
        out_specs=[pl.BlockSpec((r, width), win), pl.BlockSpec((r, width), win),
                   pl.BlockSpec((nb, t, width), tile)],
        out_shape=[jax.ShapeDtypeStruct(((nt + 1) * r, width), BF16),
                   jax.ShapeDtypeStruct(((nt + 1) * r, width), BF16),
                   jax.ShapeDtypeStruct((nb, seq, width), BF16)],
        scratch_shapes=[pltpu.VMEM((width // LANES, halo + r, LANES), F32),
                        pltpu.VMEM((r, width), F32),
                        pltpu.VMEM((r, width), F32),
                        pltpu.VMEM((r, width), F32),
                        pltpu.VMEM((nb, width), F32)],
        compiler_params=_cparams("arbitrary"),
        name="lru_fwd",
    )(x, g, w_in, conv_w, conv_b, wg, ba, bx, lam)


def _lru_bwd_kernel(xh_ref, hf_ref, gb_ref, x_ref, wg_ref, ba_ref, bx_ref,
                    lam_ref, wo_ref, o_ref, a_scr, inp_scr, hb_scr, carry_scr,
                    ysum_scr, ybm_scr):
    i = pl.program_id(0)
    nb, t, _ = x_ref.shape
    rows, width = xh_ref.shape
    nsteps = rows // nb
    nslab = width // LANES

    @pl.when(i == 0)
    def _():
        carry_scr[...] = jnp.zeros_like(carry_scr)
        hb_scr[...] = jnp.zeros_like(hb_scr)
        ysum_scr[:, pl.ds(rows, nb), :] = jnp.zeros((nslab, nb, LANES), F32)

    for l in range(nslab):
        ls = slice(l * LANES, (l + 1) * LANES)
        ysum_scr[l, pl.ds(0, rows), :] = hb_scr[:, ls] + hf_ref[:, ls].astype(F32)

    for b in range(nb):
        for l in range(nslab):
            ls = slice(l * LANES, (l + 1) * LANES)
            yb = ysum_scr[l, pl.ds(nb + b, t, stride=nb), :]
            ybm_scr[pl.ds(b * t, t), ls] = (
                yb * _gelu_tanh(gb_ref[b, :, ls].astype(F32))).astype(BF16)
    ysum_scr[:, pl.ds(rows, nb), :] = ysum_scr[:, pl.ds(0, nb), :]

    gate_pieces = _lru_gate_pieces(xh_ref, wg_ref, ba_ref, bx_ref, lam_ref,
                                   a_scr, inp_scr)
    gate_pieces = [p for r0 in range(rows - GATE_ROWS, -1, -GATE_ROWS)
                   for p in gate_pieces[(r0 // GATE_ROWS) * LRU_BLOCKS:
                                        (r0 // GATE_ROWS + 1) * LRU_BLOCKS]]
    d_out = wo_ref.shape[1]
    per_dot = -(-len(gate_pieces) * OUT_COLS // d_out)
    for c0 in range(0, d_out, OUT_COLS):
        cs = slice(c0, c0 + OUT_COLS)
        res = jnp.dot(ybm_scr[...], wo_ref[:, cs], preferred_element_type=F32)
        for b in range(nb):
            o_ref[b, :, cs] = x_ref[b, :, cs] + res[b * t:(b + 1) * t]
        for _ in range(per_dot):
            if gate_pieces:
                gate_pieces.pop(0)()
    for piece in gate_pieces:
        piece()

    h = carry_scr[...]
    for s in range(nsteps - 1, -1, -1):
        h = a_scr[pl.ds(s * nb, nb), :] * h + inp_scr[pl.ds(s * nb, nb), :]
        hb_scr[pl.ds(s * nb, nb), :] = h
    carry_scr[...] = h


def _lru_bwd(xh, hf, gb, x, wg, ba, bx, lam, w_out, j):
    nb, seq, d = x.shape
    width = xh.shape[1]
    t = LRU_T
    r = nb * t
    nt = seq // t
    win_g = lambda i: (jnp.maximum(nt - i, 0), 0)
    win_s = lambda i: (jnp.minimum(nt - i + 1, nt), 0)
    tile = lambda i: (0, jnp.minimum(nt - i + 1, nt - 1), 0)
    return pl.pallas_call(
        _lru_bwd_kernel,
        grid=(nt + 2,),
        in_specs=[pl.BlockSpec((r, width), win_g), pl.BlockSpec((r, width), win_s),
                  pl.BlockSpec((nb, t, width), tile), pl.BlockSpec((nb, t, d), tile),
                  _const_spec(wg.shape), _const_spec(ba.shape), _const_spec(bx.shape),
                  _const_spec(lam.shape), _layer_spec(w_out.shape, j)],
        out_specs=pl.BlockSpec((nb, t, d), tile),
        out_shape=jax.ShapeDtypeStruct((nb, seq, d), F32),
        scratch_shapes=[pltpu.VMEM((r, width), F32),
                        pltpu.VMEM((r, width), F32),
                        pltpu.VMEM((r, width), F32),
                        pltpu.VMEM((nb, width), F32),
                        pltpu.VMEM((width // LANES, r + nb, LANES), F32),
                        pltpu.VMEM((r, width), BF16)],
        compiler_params=_cparams("arbitrary"),
        name="lru_bwd",
    )(xh, hf, gb, x, wg, ba, bx, lam, w_out)


def _ffn_chunks(hidden):
    chunks, c = [], 0
    while c < hidden:
        size = 512 if hidden - c >= 512 else hidden - c
        chunks.append((c, size))
        c += size
    return chunks


def _ffn_kernel(*refs, final_norm, fuse_proj):
    refs = list(refs)
    h_ref = refs.pop(0)
    if fuse_proj:
        y_ref, wo_ref = refs.pop(0), refs.pop(0)
    g_ref, wg_ref, wu_ref, wd_ref = (refs.pop(0) for _ in range(4))
    gf_ref = refs.pop(0) if final_norm else None
    o_ref, hn_scr, act_scr = refs[:3]
    if fuse_proj:
        res_ref = refs[3]
        res_ref[...] = h_ref[...] + jnp.dot(y_ref[...], wo_ref[...],
                                            preferred_element_type=F32)
    else:
        res_ref = h_ref
    hn_scr[...] = _rms(res_ref[...], g_ref[...]).astype(BF16)
    for c, size in _ffn_chunks(wg_ref.shape[1]):
        gt = jnp.dot(hn_scr[...], wg_ref[:, c:c + size], preferred_element_type=F32)
        up = jnp.dot(hn_scr[...], wu_ref[:, c:c + size], preferred_element_type=F32)
        act_scr[:, c:c + size] = (gt * _sigmoid(gt) * up).astype(BF16)
    y = res_ref[...] + jnp.dot(act_scr[...], wd_ref[...], preferred_element_type=F32)
    if final_norm:
        y = _rms(y, gf_ref[...])
    o_ref[...] = y


def _ffn(h, g, w_gate, w_up, w_down, layer, g_final=None, proj=None):
    rows, d = h.shape
    hidden = w_gate.shape[2]
    final_norm = g_final is not None
    fuse_proj = proj is not None
    r = FUSED_FFN_TILE if fuse_proj else ROW_TILE
    row = lambda i: (i, 0)
    in_specs = [pl.BlockSpec((r, d), row)]
    args = [h]
    scratch = [pltpu.VMEM((r, d), BF16), pltpu.VMEM((r, hidden), BF16)]
    if fuse_proj:
        y, w_out, j = proj
        in_specs += [pl.BlockSpec((r, y.shape[1]), row), _layer_spec(w_out.shape, j)]
        args += [y, w_out]
        scratch.append(pltpu.VMEM((r, d), F32))
    in_specs += [_const_spec((1, d)), _layer_spec(w_gate.shape, layer),
                 _layer_spec(w_up.shape, layer), _layer_spec(w_down.shape, layer)]
    args += [g, w_gate, w_up, w_down]
    if final_norm:
        in_specs.append(_const_spec((1, d)))
        args.append(g_final)
    return pl.pallas_call(
        functools.partial(_ffn_kernel, final_norm=final_norm, fuse_proj=fuse_proj),
        grid=(rows // r,),
        in_specs=in_specs,
        out_specs=pl.BlockSpec((r, d), row),
        out_shape=jax.ShapeDtypeStruct((rows, d), F32),
        scratch_shapes=scratch,
        compiler_params=_cparams("parallel"),
        name="ffn" + ("_proj" if fuse_proj else "") + ("_final" if final_norm else ""),
    )(*args)


def _ret_in_kernel(h_ref, g_ref, w_ref, cos_ref, sin_ref, q_ref, k_ref, v_ref,
                   sg_ref, hn_scr, *, heads):
    qk = q_ref.shape[1]
    vd = v_ref.shape[1]
    hd = qk // heads
    half = hd // 2
    hn_scr[...] = _rms(h_ref[...], g_ref[...]).astype(BF16)
    cos = cos_ref[...]
    sin = sin_ref[...]
    k_scale = hd ** -0.5
    for h in range(heads):
        for which, dst, scale in ((0, q_ref, 1.0), (1, k_ref, k_scale)):
            c0 = which * qk + h * hd
            u = jnp.dot(hn_scr[...], w_ref[:, c0:c0 + hd], preferred_element_type=F32)
            x1, x2 = u[:, :half], u[:, half:]
            r1 = x1 * cos - x2 * sin
            r2 = x2 * cos + x1 * sin
            if scale != 1.0:
                r1, r2 = r1 * scale, r2 * scale
            dst[:, h * hd:h * hd + half] = r1.astype(dst.dtype)
            dst[:, h * hd + half:(h + 1) * hd] = r2.astype(dst.dtype)
    step = 512
    for c in range(0, vd, step):
        v_ref[:, c:c + step] = jnp.dot(
            hn_scr[...], w_ref[:, 2 * qk + c:2 * qk + c + step],
            preferred_element_type=F32).astype(v_ref.dtype)
    for c in range(0, vd, step):
        gt = jnp.dot(hn_scr[...], w_ref[:, 2 * qk + vd + c:2 * qk + vd + c + step],
                     preferred_element_type=F32)
        sg_ref[:, c:c + step] = (gt * _sigmoid(gt)).astype(sg_ref.dtype)


def _ret_in(h, g, w_in, j, cos, sin, seq, heads, qk, vd):
    rows, d = h.shape
    r = ROW_TILE
    half = cos.shape[1]
    per_seq = seq // r
    pos = lambda i: (i % per_seq, 0)
    row = lambda i: (i, 0)
    return pl.pallas_call(
        functools.partial(_ret_in_kernel, heads=heads),
        grid=(rows // r,),
        in_specs=[pl.BlockSpec((r, d), row), _const_spec((1, d)),
                  _layer_spec(w_in.shape, j),
                  pl.BlockSpec((r, half), pos), pl.BlockSpec((r, half), pos)],
        out_specs=[pl.BlockSpec((r, qk), row), pl.BlockSpec((r, qk), row),
                   pl.BlockSpec((r, vd), row), pl.BlockSpec((r, vd), row)],
        out_shape=[jax.ShapeDtypeStruct((rows, qk), BF16),
                   jax.ShapeDtypeStruct((rows, qk), BF16),
                   jax.ShapeDtypeStruct((rows, vd), BF16),
                   jax.ShapeDtypeStruct((rows, vd), BF16)],
        scratch_shapes=[pltpu.VMEM((r, d), BF16)],
        compiler_params=_cparams("parallel"),
        name="ret_in",
    )(h, g, w_in, cos, sin)


def _ret_core_kernel(q_ref, k_ref, v_ref, sg_ref, dec_ref, zf_ref, zb_ref, xf_ref,
                     xb_ref, gc_ref, o_ref, sf_scr, sb_scr, stf_scr, stb_scr, y_scr):
    seq, dk = q_ref.shape
    dv = v_ref.shape[1]
    c = dec_ref.shape[0]
    nc = seq // c
    gcf = gc_ref[pl.ds(0, 1), :]
    gcb = gc_ref[pl.ds(1, 1), :]

    def kv_update(state, n, z_ref, gcd):
        r0 = pl.multiple_of(n * c, c)
        kz = (k_ref[pl.ds(r0, c), :].astype(F32) * z_ref[...]).astype(BF16)
        upd = lax.dot_general(kz, v_ref[pl.ds(r0, c), :], (((0,), (0,)), ((), ())),
                              preferred_element_type=F32)
        return state * gcd + upd

    stf_scr[...] = jnp.zeros_like(stf_scr)
    stb_scr[...] = jnp.zeros_like(stb_scr)
    sf_scr[0] = jnp.zeros((dk, dv), BF16)
    sb_scr[nc - 1] = jnp.zeros((dk, dv), BF16)

    def state_body(s, carry):
        stf_scr[...] = kv_update(stf_scr[...], s, zf_ref, gcf)
        sf_scr[s + 1] = stf_scr[...].astype(BF16)
        stb_scr[...] = kv_update(stb_scr[...], nc - 1 - s, zb_ref, gcb)
        sb_scr[nc - 2 - s] = stb_scr[...].astype(BF16)
        return carry

    lax.fori_loop(0, nc - 1, state_body, 0, unroll=STATE_UNROLL)

    def finish(n):
        r0 = pl.multiple_of(n * c, c)
        y = y_scr[...]
        y = y * lax.rsqrt(jnp.mean(y * y, axis=-1, keepdims=True) + NORM_EPS)
        o_ref[pl.ds(r0, c), :] = (y * sg_ref[pl.ds(r0, c), :].astype(F32)).astype(o_ref.dtype)

    def out_body(n, carry):
        finish(jnp.maximum(n - 1, 0))
        r0 = pl.multiple_of(n * c, c)
        q = q_ref[pl.ds(r0, c), :]
        scores = lax.dot_general(q, k_ref[pl.ds(r0, c), :], (((1,), (1,)), ((), ())),
                                 preferred_element_type=F32)
        y = xf_ref[...] * jnp.dot(q, sf_scr[n], preferred_element_type=F32)
        y = y + xb_ref[...] * jnp.dot(q, sb_scr[n], preferred_element_type=F32)
        p = (scores * dec_ref[...]).astype(BF16)
        y_scr[...] = y + jnp.dot(p, v_ref[pl.ds(r0, c), :], preferred_element_type=F32)
        return carry

    y_scr[...] = jnp.zeros_like(y_scr)
    lax.fori_loop(0, nc, out_body, 0, unroll=4)
    finish(nc - 1)


def _ret_core(q, k, v, sg, tabs, heads):
    b, seq, qk = q.shape
    vd = v.shape[2]
    dk, dv = qk // heads, vd // heads
    c = RET_CHUNK
    dec, zf, zb, xf, xb, gc = tabs
    bh = lambda i, j: (i, 0, j)
    hh = lambda i, j: (j, 0, 0)
    return pl.pallas_call(
        _ret_core_kernel,
        grid=(b, heads),
        in_specs=[pl.BlockSpec((None, seq, dk), bh), pl.BlockSpec((None, seq, dk), bh),
                  pl.BlockSpec((None, seq, dv), bh), pl.BlockSpec((None, seq, dv), bh),
                  pl.BlockSpec((None, c, c), hh),
                  pl.BlockSpec((None, c, dk), hh), pl.BlockSpec((None, c, dk), hh),
                  pl.BlockSpec((None, c, dv), hh), pl.BlockSpec((None, c, dv), hh),
                  pl.BlockSpec((None, SUBLANES, dv), hh)],
        out_specs=pl.BlockSpec((None, seq, dv), bh),
        out_shape=jax.ShapeDtypeStruct((b, seq, vd), BF16),
        scratch_shapes=[pltpu.VMEM((seq // c, dk, dv), BF16),
                        pltpu.VMEM((seq // c, dk, dv), BF16),
                        pltpu.VMEM((dk, dv), F32),
                        pltpu.VMEM((dk, dv), F32),
                        pltpu.VMEM((c, dv), F32)],
        compiler_params=_cparams("parallel", "parallel"),
        name="ret_core",
    )(q, k, v, sg, dec, zf, zb, xf, xb, gc)


def _ret_tables(heads, dk, dv):
    c = RET_CHUNK
    log_gf = jnp.log1p(-jnp.exp2(-5.0 - jnp.arange(heads, dtype=F32)))
    log_gb = log_gf[::-1]
    pos = jnp.arange(c, dtype=F32)
    diff = pos[:, None] - pos[None, :]
    lower = diff >= 0
    upper = diff < 0
    dec_f = jnp.where(lower[None], jnp.exp(jnp.where(lower, diff, 0.0)[None]
                                           * log_gf[:, None, None]), 0.0)
    dec_b = jnp.where(upper[None], jnp.exp(jnp.where(upper, -diff, 0.0)[None]
                                           * log_gb[:, None, None]), 0.0)
    dec = dec_f + dec_b
    zeta_f = jnp.exp((c - 1.0 - pos)[None, :] * log_gf[:, None])
    xi_f = jnp.exp((pos + 1.0)[None, :] * log_gf[:, None])
    zeta_b = jnp.exp(pos[None, :] * log_gb[:, None])
    xi_b = jnp.exp((c - pos)[None, :] * log_gb[:, None])
    wide = lambda t, n: jnp.broadcast_to(t[:, :, None], (heads, c, n))
    g_f = jnp.exp(c * log_gf)
    g_b = jnp.exp(c * log_gb)
    gc = jnp.zeros((heads, SUBLANES, dv), F32)
    gc = gc.at[:, 0, :].set(g_f[:, None]).at[:, 1, :].set(g_b[:, None])
    return (dec, wide(zeta_f, dk), wide(zeta_b, dk), wide(xi_f, dv), wide(xi_b, dv), gc)


def _rope_tables(seq, hd):
    half = hd // 2
    inv_freq = 1.0 / (ROPE_BASE ** jnp.linspace(0.0, 1.0, half, dtype=F32))
    ang = jnp.arange(seq, dtype=F32)[:, None] * inv_freq[None, :]
    return jnp.cos(ang), jnp.sin(ang)


def kernel(x, ln_mix, ln_ffn, ln_final, lru_w_in, lru_conv_w, lru_conv_b,
           lru_gate_a_w, lru_gate_a_b, lru_gate_x_w, lru_gate_x_b, lru_lambda,
           lru_w_out, ret_w_in, ret_w_out, ffn_w_gate, ffn_w_up, ffn_w_down):
    b, seq, d = x.shape
    depth = ln_mix.shape[0]
    rows = b * seq
    assert b == SUBLANES and seq % ROW_TILE == 0 and seq % LRU_T == 0
    row = lambda v: v.reshape(1, -1)

    lru_w_in, lru_w_out, ret_w_in, ret_w_out, ffn_w_gate, ffn_w_up, ffn_w_down = (
        w.astype(BF16) for w in (lru_w_in, lru_w_out, ret_w_in, ret_w_out,
                                 ffn_w_gate, ffn_w_up, ffn_w_down))

    h = x.reshape(rows, d)
    for layer in range(depth):
        j = layer // 2
        g_final = row(ln_final) if layer == depth - 1 else None
        proj = None
        if layer % 2 == 0:
            x3 = h.reshape(b, seq, d)
            wg = [jnp.concatenate([lru_gate_a_w[j, dr], lru_gate_x_w[j, dr]],
                                  axis=-1).astype(BF16) for dr in range(2)]
            xh, hf, gb = _lru_fwd(
                x3, row(ln_mix[layer]), lru_w_in, j, lru_conv_w[j],
                row(lru_conv_b[j]), wg[0], row(lru_gate_a_b[j, 0]),
                row(lru_gate_x_b[j, 0]), row(lru_lambda[j, 0]))
            h = _lru_bwd(xh, hf, gb, x3, wg[1], row(lru_gate_a_b[j, 1]),
                         row(lru_gate_x_b[j, 1]), row(lru_lambda[j, 1]),
                         lru_w_out, j).reshape(rows, d)
        else:
            heads = RET_HEADS
            vd = ret_w_out.shape[1]
            qk = (ret_w_in.shape[2] - 2 * vd) // 2
            cos, sin = _rope_tables(seq, qk // heads)
            q, k, v, sg = _ret_in(h, row(ln_mix[layer]), ret_w_in, j,
                                  cos, sin, seq, heads, qk, vd)
            tabs = _ret_tables(heads, qk // heads, vd // heads)
            yg = _ret_core(q.reshape(b, seq, qk), k.reshape(b, seq, qk),
                           v.reshape(b, seq, vd), sg.reshape(b, seq, vd), tabs, heads)
            proj = (yg.reshape(rows, vd), ret_w_out, j)
        h = _ffn(h, row(ln_ffn[layer]), ffn_w_gate, ffn_w_up, ffn_w_down, layer,
                 g_final, proj)
    return h.reshape(b, seq, d)
```

```python
import functools
import math

import jax
import jax.numpy as jnp
from jax import lax
from jax.experimental import pallas as pl
from jax.experimental.pallas import tpu as pltpu

F32 = jnp.float32
BF16 = jnp.bfloat16

NORM_EPS = 1e-6
RSQRT_FLOOR = 1e-30
RG_LRU_C = 8.0
LRU_BLOCKS = 4
CONV_WIDTH = 4
CONV_PAD_LEFT = 2
RET_HEADS = 4
ROPE_BASE = 10000.0

SUBLANES = 8
LANES = 128
VMEM_LIMIT_BYTES = 56 * 1024 * 1024
ROW_TILE = 1024
LRU_T = 64
GATE_ROWS = 256
PROJ_COLS = 1024
OUT_COLS = 256
RET_CHUNK = 256
STATE_UNROLL = 3


def _cparams(*sem):
    return pltpu.CompilerParams(dimension_semantics=sem,
                                vmem_limit_bytes=VMEM_LIMIT_BYTES)


def _const_spec(shape):
    nd = len(shape)
    return pl.BlockSpec(shape, lambda *_: (0,) * nd, pipeline_mode=pl.Buffered(1))


def _layer_spec(shape, layer):
    nd = len(shape)
    return pl.BlockSpec((None,) + tuple(shape[1:]), lambda *_: (layer,) + (0,) * (nd - 1),
                        pipeline_mode=pl.Buffered(1))


def _rms(x, g):
    ms = jnp.mean(x * x, axis=-1, keepdims=True)
    return x * lax.rsqrt(ms + NORM_EPS) * g


def _sigmoid(x):
    return 0.5 * jnp.tanh(0.5 * x) + 0.5


def _gelu_tanh(x):
    c = math.sqrt(2.0 / math.pi)
    hx = 0.5 * x
    return hx * jnp.tanh(x * ((c * 0.044715) * (x * x) + c)) + hx


def _softplus(x):
    return jnp.maximum(x, 0.0) + jnp.log(1.0 + jnp.exp(-jnp.abs(x)))


def _lru_gate_pieces(xh_ref, wg_ref, ba_ref, bx_ref, lam_ref, a_dst, inp_dst):
    rows, width = xh_ref.shape
    bw = width // LRU_BLOCKS
    c2 = (-0.5 * RG_LRU_C * math.log2(math.e)) * _softplus(-lam_ref[...])
    ba2 = 0.5 * ba_ref[...]
    bx2 = 0.5 * bx_ref[...]

    def piece(r0, n):
        cs = slice(n * bw, (n + 1) * bw)
        xh = xh_ref[pl.ds(r0, GATE_ROWS), cs]
        z = jnp.dot(xh.astype(BF16), wg_ref[n], preferred_element_type=F32)
        tr = jnp.tanh(z[:, :bw] + ba2[:, cs])
        ti = jnp.tanh(z[:, bw:] + bx2[:, cs])
        a = jnp.exp2(c2[:, cs] * (tr + 1.0))
        om = 1.0 - a * a
        root = om * lax.rsqrt(jnp.maximum(om, RSQRT_FLOOR))
        a_dst[pl.ds(r0, GATE_ROWS), cs] = a
        inp_dst[pl.ds(r0, GATE_ROWS), cs] = ((ti + 1.0) * xh.astype(F32)) * root

    return [functools.partial(piece, r0, n)
            for r0 in range(0, rows, GATE_ROWS) for n in range(LRU_BLOCKS)]


def _lru_fwd_kernel(x_ref, g_ref, w_ref, cw_ref, cb_ref, wg_ref, ba_ref, bx_ref,
                    lam_ref, xh_ref, hf_ref, gb_ref, ext_scr, xh_scr,
                    a_scr, inp_scr, carry_scr):
    i = pl.program_id(0)
    nt = pl.num_programs(0) - 2
    nb, t, _ = x_ref.shape
    rows, width = xh_ref.shape
    halo = (CONV_WIDTH - 1) * nb
    nslab = width // LANES

    @pl.when(i == 0)
    def _():
        carry_scr[...] = jnp.zeros_like(carry_scr)
        ext_scr[...] = jnp.zeros_like(ext_scr)

    cwh = 0.5 * cw_ref[...]
    cbh = 0.5 * cb_ref[...]
    for l in range(nslab):
        ls = slice(l * LANES, (l + 1) * LANES)
        acc = cbh[:, ls] + cwh[0:1, ls] * ext_scr[l, pl.ds(0, rows), :]
        for k in range(1, CONV_WIDTH):
            acc = acc + cwh[k:k + 1, ls] * ext_scr[l, pl.ds(k * nb, rows), :]
        xh_scr[:, ls] = acc
    xh_ref[...] = xh_scr[...].astype(xh_ref.dtype)

    ext_scr[:, pl.ds(0, halo), :] = ext_scr[:, pl.ds(rows, halo), :]
    gate_pieces = _lru_gate_pieces(xh_scr, wg_ref, ba_ref, bx_ref, lam_ref,
                                   a_scr, inp_scr)
    nbs = GATE_ROWS // t
    n_dots = (nb // nbs) * (2 * width // PROJ_COLS)
    per_dot = -(-len(gate_pieces) // n_dots)
    for b0 in range(0, nb, nbs):
        hn = jnp.concatenate([_rms(x_ref[b0 + bb], g_ref[...]).astype(BF16)
                              for bb in range(nbs)], axis=0)
        for c0 in range(0, 2 * width, PROJ_COLS):
            u = jnp.dot(hn, w_ref[:, c0:c0 + PROJ_COLS], preferred_element_type=F32)
            for bb in range(nbs):
                ub = u[bb * t:(bb + 1) * t]
                if c0 < width:
                    for l in range(PROJ_COLS // LANES):
                        ext_scr[c0 // LANES + l, pl.ds(halo + b0 + bb, t, stride=nb), :] = (
                            ub[:, l * LANES:(l + 1) * LANES])
                else:
                    gb_ref[b0 + bb, :, c0 - width:c0 - width + PROJ_COLS] = (
                        ub.astype(gb_ref.dtype))
            for _ in range(per_dot):
                if gate_pieces:
                    gate_pieces.pop(0)()
    for piece in gate_pieces:
        piece()

    h = jnp.where(i > 1, carry_scr[...], 0.0)
    for s in range(rows // nb):
        h = a_scr[pl.ds(s * nb, nb), :] * h + inp_scr[pl.ds(s * nb, nb), :]
        if s == 0:
            h = jnp.where(i > 1, h, 0.0)
        inp_scr[pl.ds(s * nb, nb), :] = h
    carry_scr[...] = h
    hf_ref[...] = inp_scr[...].astype(hf_ref.dtype)

    @pl.when(i == nt)
    def _():
        ext_scr[:, pl.ds(halo, nb), :] = jnp.zeros((nslab, nb, LANES), F32)

    @pl.when(i == nt + 1)
    def _():
        xh_ref[pl.ds(2 * nb, rows - 2 * nb), :] = jnp.zeros(
            (rows - 2 * nb, width), xh_ref.dtype)
        xh_ref[pl.ds(0, 2 * nb), :] = jnp.where(
            lax.broadcasted_iota(jnp.int32, (2 * nb, width), 0) < nb,
            xh_ref[pl.ds(0, 2 * nb), :], jnp.zeros((), xh_ref.dtype))


def _lru_fwd(x, g, w_in, j, conv_w, conv_b, wg, ba, bx, lam):
    nb, seq, d = x.shape
    width = w_in.shape[2] // 2
    t = LRU_T
    r = nb * t
    nt = seq // t
    halo = (CONV_WIDTH - 1) * nb
    assert CONV_WIDTH - 1 - CONV_PAD_LEFT == 1
    tile = lambda i: (0, jnp.minimum(i, nt - 1), 0)
    win = lambda i: (jnp.maximum(i - 1, 0), 0)
    return pl.pallas_call(
        _lru_fwd_kernel,
        grid=(nt + 2,),
        in_specs=[pl.BlockSpec((nb, t, d), tile), _const_spec((1, d)),
                  _layer_spec(w_in.shape, j), _const_spec(conv_w.shape),
                  _const_spec(conv_b.shape), _const_spec(wg.shape),
                  _const_spec(ba.shape), _const_spec(bx.shape), _const_spec(lam.shape)],
        out_specs=[pl.BlockSpec((r, width), win), pl.BlockSpec((r, width), win),
                   pl.BlockSpec((nb, t, width), tile)],
        out_shape=[jax.ShapeDtypeStruct(((nt + 1) * r, width), BF16),
                   jax.ShapeDtypeStruct(((nt + 1) * r, width), BF16),
                   jax.ShapeDtypeStruct((nb, seq, width), BF16)],
        scratch_shapes=[pltpu.VMEM((width // LANES, halo + r, LANES), F32),
                        pltpu.VMEM((r, width), F32),
                        pltpu.VMEM((r, width), F32),
                        pltpu.VMEM((r, width), F32),
                        pltpu.VMEM((nb, width), F32)],
        compiler_params=_cparams("arbitrary"),
        name="lru_fwd",
    )(x, g, w_in, conv_w, conv_b, wg, ba, bx, lam)


def _lru_bwd_kernel(xh_ref, hf_ref, gb_ref, x_ref, wg_ref, ba_ref, bx_ref,
                    lam_ref, wo_ref, o_ref, a_scr, inp_scr, hb_scr, carry_scr,
                    ysum_scr, ybm_scr):
    i = pl.program_id(0)
    nb, t, _ = x_ref.shape
    rows, width = xh_ref.shape
    nsteps = rows // nb
    nslab = width // LANES

    @pl.when(i == 0)
    def _():
        carry_scr[...] = jnp.zeros_like(carry_scr)
        hb_scr[...] = jnp.zeros_like(hb_scr)
        ysum_scr[:, pl.ds(rows, nb), :] = jnp.zeros((nslab, nb, LANES), F32)

    for l in range(nslab):
        ls = slice(l * LANES, (l + 1) * LANES)
        ysum_scr[l, pl.ds(0, rows), :] = hb_scr[:, ls] + hf_ref[:, ls].astype(F32)

    for b in range(nb):
        for l in range(nslab):
            ls = slice(l * LANES, (l + 1) * LANES)
            yb = ysum_scr[l, pl.ds(nb + b, t, stride=nb), :]
            ybm_scr[pl.ds(b * t, t), ls] = (
                yb * _gelu_tanh(gb_ref[b, :, ls].astype(F32))).astype(BF16)
    ysum_scr[:, pl.ds(rows, nb), :] = ysum_scr[:, pl.ds(0, nb), :]

    gate_pieces = _lru_gate_pieces(xh_ref, wg_ref, ba_ref, bx_ref, lam_ref,
                                   a_scr, inp_scr)
    gate_pieces = [p for r0 in range(rows - GATE_ROWS, -1, -GATE_ROWS)
                   for p in gate_pieces[(r0 // GATE_ROWS) * LRU_BLOCKS:
                                        (r0 // GATE_ROWS + 1) * LRU_BLOCKS]]
    d_out = wo_ref.shape[1]
    per_dot = -(-len(gate_pieces) * OUT_COLS // d_out)
    for c0 in range(0, d_out, OUT_COLS):
        cs = slice(c0, c0 + OUT_COLS)
        res = jnp.dot(ybm_scr[...], wo_ref[:, cs], preferred_element_type=F32)
        for b in range(nb):
            o_ref[b, :, cs] = x_ref[b, :, cs] + res[b * t:(b + 1) * t]
        for _ in range(per_dot):
            if gate_pieces:
                gate_pieces.pop(0)()
    for piece in gate_pieces:
        piece()

    h = carry_scr[...]
    for s in range(nsteps - 1, -1, -1):
        h = a_scr[pl.ds(s * nb, nb), :] * h + inp_scr[pl.ds(s * nb, nb), :]
        hb_scr[pl.ds(s * nb, nb), :] = h
    carry_scr[...] = h


def _lru_bwd(xh, hf, gb, x, wg, ba, bx, lam, w_out, j):
    nb, seq, d = x.shape
    width = xh.shape[1]
    t = LRU_T
    r = nb * t
    nt = seq // t
    win_g = lambda i: (jnp.maximum(nt - i, 0), 0)
    win_s = lambda i: (jnp.minimum(nt - i + 1, nt), 0)
    tile = lambda i: (0, jnp.minimum(nt - i + 1, nt - 1), 0)
    return pl.pallas_call(
        _lru_bwd_kernel,
        grid=(nt + 2,),
        in_specs=[pl.BlockSpec((r, width), win_g), pl.BlockSpec((r, width), win_s),
                  pl.BlockSpec((nb, t, width), tile), pl.BlockSpec((nb, t, d), tile),
                  _const_spec(wg.shape), _const_spec(ba.shape), _const_spec(bx.shape),
                  _const_spec(lam.shape), _layer_spec(w_out.shape, j)],
        out_specs=pl.BlockSpec((nb, t, d), tile),
        out_shape=jax.ShapeDtypeStruct((nb, seq, d), F32),
        scratch_shapes=[pltpu.VMEM((r, width), F32),
                        pltpu.VMEM((r, width), F32),
                        pltpu.VMEM((r, width), F32),
                        pltpu.VMEM((nb, width), F32),
                        pltpu.VMEM((width // LANES, r + nb, LANES), F32),
                        pltpu.VMEM((r, width), BF16)],
        compiler_params=_cparams("arbitrary"),
        name="lru_bwd",
    )(xh, hf, gb, x, wg, ba, bx, lam, w_out)


def _ffn_chunks(hidden):
    chunks, c = [], 0
    while c < hidden:
        size = 512 if hidden - c >= 512 else hidden - c
        chunks.append((c, size))
        c += size
    return chunks


def _ffn_kernel(*refs, final_norm, fuse_proj):
    refs = list(refs)
    h_ref = refs.pop(0)
    if fuse_proj:
        y_ref, wo_ref = refs.pop(0), refs.pop(0)
    g_ref, wg_ref, wu_ref, wd_ref = (refs.pop(0) for _ in range(4))
    gf_ref = refs.pop(0) if final_norm else None
    o_ref, hn_scr, act_scr = refs
    if fuse_proj:
        res_ref = o_ref
        res_ref[...] = h_ref[...] + jnp.dot(y_ref[...], wo_ref[...],
                                            preferred_element_type=F32)
    else:
        res_ref = h_ref
    hn_scr[...] = _rms(res_ref[...], g_ref[...]).astype(BF16)
    for c, size in _ffn_chunks(wg_ref.shape[1]):
        gt = jnp.dot(hn_scr[...], wg_ref[:, c:c + size], preferred_element_type=F32)
        up = jnp.dot(hn_scr[...], wu_ref[:, c:c + size], preferred_element_type=F32)
        act_scr[:, c:c + size] = (gt * _sigmoid(gt) * up).astype(BF16)
    y = res_ref[...] + jnp.dot(act_scr[...], wd_ref[...], preferred_element_type=F32)
    if final_norm:
        y = _rms(y, gf_ref[...])
    o_ref[...] = y


def _ffn(h, g, w_gate, w_up, w_down, layer, g_final=None, proj=None):
    rows, d = h.shape
    hidden = w_gate.shape[2]
    final_norm = g_final is not None
    fuse_proj = proj is not None
    r = ROW_TILE
    row = lambda i: (i, 0)
    in_specs = [pl.BlockSpec((r, d), row)]
    args = [h]
    scratch = [pltpu.VMEM((r, d), BF16), pltpu.VMEM((r, hidden), BF16)]
    if fuse_proj:
        y, w_out, j = proj
        in_specs += [pl.BlockSpec((r, y.shape[1]), row), _layer_spec(w_out.shape, j)]
        args += [y, w_out]
    in_specs += [_const_spec((1, d)), _layer_spec(w_gate.shape, layer),
                 _layer_spec(w_up.shape, layer), _layer_spec(w_down.shape, layer)]
    args += [g, w_gate, w_up, w_down]
    if final_norm:
        in_specs.append(_const_spec((1, d)))
        args.append(g_final)
    return pl.pallas_call(
        functools.partial(_ffn_kernel, final_norm=final_norm, fuse_proj=fuse_proj),
        grid=(rows // r,),
        in_specs=in_specs,
        out_specs=pl.BlockSpec((r, d), row),
        out_shape=jax.ShapeDtypeStruct((rows, d), F32),
        scratch_shapes=scratch,
        compiler_params=_cparams("parallel"),
        name="ffn" + ("_proj" if fuse_proj else "") + ("_final" if final_norm else ""),
    )(*args)


def _ret_in_kernel(h_ref, g_ref, w_ref, cos_ref, sin_ref, q_ref, k_ref, v_ref,
                   sg_ref, hn_scr, *, heads):
    qk = q_ref.shape[1]
    vd = v_ref.shape[1]
    hd = qk // heads
    half = hd // 2
    hn_scr[...] = _rms(h_ref[...], g_ref[...]).astype(BF16)
    cos = cos_ref[...]
    sin = sin_ref[...]
    k_scale = hd ** -0.5
    for h in range(heads):
        for which, dst, scale in ((0, q_ref, 1.0), (1, k_ref, k_scale)):
            c0 = which * qk + h * hd
            u = jnp.dot(hn_scr[...], w_ref[:, c0:c0 + hd], preferred_element_type=F32)
            x1, x2 = u[:, :half], u[:, half:]
            r1 = x1 * cos - x2 * sin
            r2 = x2 * cos + x1 * sin
            if scale != 1.0:
                r1, r2 = r1 * scale, r2 * scale
            dst[:, h * hd:h * hd + half] = r1.astype(dst.dtype)
            dst[:, h * hd + half:(h + 1) * hd] = r2.astype(dst.dtype)
    step = 512
    for c in range(0, vd, step):
        v_ref[:, c:c + step] = jnp.dot(
            hn_scr[...], w_ref[:, 2 * qk + c:2 * qk + c + step],
            preferred_element_type=F32).astype(v_ref.dtype)
    for c in range(0, vd, step):
        gt = jnp.dot(hn_scr[...], w_ref[:, 2 * qk + vd + c:2 * qk + vd + c + step],
                     preferred_element_type=F32)
        sg_ref[:, c:c + step] = (gt * _sigmoid(gt)).astype(sg_ref.dtype)


def _ret_in(h, g, w_in, j, cos, sin, seq, heads, qk, vd):
    rows, d = h.shape
    r = ROW_TILE
    half = cos.shape[1]
    per_seq = seq // r
    pos = lambda i: (i % per_seq, 0)
    row = lambda i: (i, 0)
    return pl.pallas_call(
        functools.partial(_ret_in_kernel, heads=heads),
        grid=(rows // r,),
        in_specs=[pl.BlockSpec((r, d), row), _const_spec((1, d)),
                  _layer_spec(w_in.shape, j),
                  pl.BlockSpec((r, half), pos), pl.BlockSpec((r, half), pos)],
        out_specs=[pl.BlockSpec((r, qk), row), pl.BlockSpec((r, qk), row),
                   pl.BlockSpec((r, vd), row), pl.BlockSpec((r, vd), row)],
        out_shape=[jax.ShapeDtypeStruct((rows, qk), BF16),
                   jax.ShapeDtypeStruct((rows, qk), BF16),
                   jax.ShapeDtypeStruct((rows, vd), BF16),
                   jax.ShapeDtypeStruct((rows, vd), BF16)],
        scratch_shapes=[pltpu.VMEM((r, d), BF16)],
        compiler_params=_cparams("parallel"),
        name="ret_in",
    )(h, g, w_in, cos, sin)


def _lane_tiled(t, n):
    return jnp.concatenate([t] * (n // LANES), axis=1)


def _ret_core_kernel(q_ref, k_ref, v_ref, sg_ref, dec_ref, zf_ref, zb_ref, xf_ref,
                     xb_ref, gc_ref, o_ref, sf_scr, sb_scr, stf_scr, stb_scr, y_scr):
    seq, dk = q_ref.shape
    dv = v_ref.shape[1]
    c = dec_ref.shape[0]
    nc = seq // c
    gcf = gc_ref[pl.ds(0, 1), :]
    gcb = gc_ref[pl.ds(1, 1), :]

    def kv_update(state, n, z_ref, gcd):
        r0 = pl.multiple_of(n * c, c)
        kz = (k_ref[pl.ds(r0, c), :].astype(F32) * _lane_tiled(z_ref[...], dk)).astype(BF16)
        upd = lax.dot_general(kz, v_ref[pl.ds(r0, c), :], (((0,), (0,)), ((), ())),
                              preferred_element_type=F32)
        return state * gcd + upd

    stf_scr[...] = jnp.zeros_like(stf_scr)
    stb_scr[...] = jnp.zeros_like(stb_scr)
    sf_scr[0] = jnp.zeros((dk, dv), BF16)
    sb_scr[nc - 1] = jnp.zeros((dk, dv), BF16)

    def state_body(s, carry):
        stf_scr[...] = kv_update(stf_scr[...], s, zf_ref, gcf)
        sf_scr[s + 1] = stf_scr[...].astype(BF16)
        stb_scr[...] = kv_update(stb_scr[...], nc - 1 - s, zb_ref, gcb)
        sb_scr[nc - 2 - s] = stb_scr[...].astype(BF16)
        return carry

    lax.fori_loop(0, nc - 1, state_body, 0, unroll=STATE_UNROLL)

    def finish(n):
        r0 = pl.multiple_of(n * c, c)
        y = y_scr[...]
        y = y * lax.rsqrt(jnp.mean(y * y, axis=-1, keepdims=True) + NORM_EPS)
        o_ref[pl.ds(r0, c), :] = (y * sg_ref[pl.ds(r0, c), :].astype(F32)).astype(o_ref.dtype)

    def out_body(n, carry):
        finish(jnp.maximum(n - 1, 0))
        r0 = pl.multiple_of(n * c, c)
        q = q_ref[pl.ds(r0, c), :]
        scores = lax.dot_general(q, k_ref[pl.ds(r0, c), :], (((1,), (1,)), ((), ())),
                                 preferred_element_type=F32)
        qf32 = q.astype(F32)
        qf = (qf32 * _lane_tiled(xf_ref[...], dk)).astype(BF16)
        qb = (qf32 * _lane_tiled(xb_ref[...], dk)).astype(BF16)
        p = (scores * dec_ref[...]).astype(BF16)
        lhs = jnp.concatenate([qf, qb, p], axis=1)
        rhs = jnp.concatenate([sf_scr[n], sb_scr[n], v_ref[pl.ds(r0, c), :]], axis=0)
        y_scr[...] = jnp.dot(lhs, rhs, preferred_element_type=F32)
        return carry

    y_scr[...] = jnp.zeros_like(y_scr)
    lax.fori_loop(0, nc, out_body, 0, unroll=4)
    finish(nc - 1)


def _ret_core(q, k, v, sg, tabs, heads):
    b, seq, qk = q.shape
    vd = v.shape[2]
    dk, dv = qk // heads, vd // heads
    c = RET_CHUNK
    dec, zf, zb, xf, xb, gc = tabs
    bh = lambda i, j: (i, 0, j)
    hh = lambda i, j: (j, 0, 0)
    return pl.pallas_call(
        _ret_core_kernel,
        grid=(b, heads),
        in_specs=[pl.BlockSpec((None, seq, dk), bh), pl.BlockSpec((None, seq, dk), bh),
                  pl.BlockSpec((None, seq, dv), bh), pl.BlockSpec((None, seq, dv), bh),
                  pl.BlockSpec((None, c, c), hh),
                  pl.BlockSpec((None, c, LANES), hh), pl.BlockSpec((None, c, LANES), hh),
                  pl.BlockSpec((None, c, LANES), hh), pl.BlockSpec((None, c, LANES), hh),
                  pl.BlockSpec((None, SUBLANES, dv), hh)],
        out_specs=pl.BlockSpec((None, seq, dv), bh),
        out_shape=jax.ShapeDtypeStruct((b, seq, vd), BF16),
        scratch_shapes=[pltpu.VMEM((seq // c, dk, dv), BF16),
                        pltpu.VMEM((seq // c, dk, dv), BF16),
                        pltpu.VMEM((dk, dv), F32),
                        pltpu.VMEM((dk, dv), F32),
                        pltpu.VMEM((c, dv), F32)],
        compiler_params=_cparams("parallel", "parallel"),
        name="ret_core",
    )(q, k, v, sg, dec, zf, zb, xf, xb, gc)


def _ret_tables(heads, dk, dv):
    c = RET_CHUNK
    log_gf = jnp.log1p(-jnp.exp2(-5.0 - jnp.arange(heads, dtype=F32)))
    log_gb = log_gf[::-1]
    pos = jnp.arange(c, dtype=F32)
    diff = pos[:, None] - pos[None, :]
    lower = diff >= 0
    upper = diff < 0
    dec_f = jnp.where(lower[None], jnp.exp(jnp.where(lower, diff, 0.0)[None]
                                           * log_gf[:, None, None]), 0.0)
    dec_b = jnp.where(upper[None], jnp.exp(jnp.where(upper, -diff, 0.0)[None]
                                           * log_gb[:, None, None]), 0.0)
    dec = dec_f + dec_b
    zeta_f = jnp.exp((c - 1.0 - pos)[None, :] * log_gf[:, None])
    xi_f = jnp.exp((pos + 1.0)[None, :] * log_gf[:, None])
    zeta_b = jnp.exp(pos[None, :] * log_gb[:, None])
    xi_b = jnp.exp((c - pos)[None, :] * log_gb[:, None])
    wide = lambda t: jnp.broadcast_to(t[:, :, None], (heads, c, LANES))
    g_f = jnp.exp(c * log_gf)
    g_b = jnp.exp(c * log_gb)
    gc = jnp.zeros((heads, SUBLANES, dv), F32)
    gc = gc.at[:, 0, :].set(g_f[:, None]).at[:, 1, :].set(g_b[:, None])
    return (dec, wide(zeta_f), wide(zeta_b), wide(xi_f), wide(xi_b), gc)


def _rope_tables(seq, hd):
    half = hd // 2
    inv_freq = 1.0 / (ROPE_BASE ** jnp.linspace(0.0, 1.0, half, dtype=F32))
    ang = jnp.arange(seq, dtype=F32)[:, None] * inv_freq[None, :]
    return jnp.cos(ang), jnp.sin(ang)


def kernel(x, ln_mix, ln_ffn, ln_final, lru_w_in, lru_conv_w, lru_conv_b,
           lru_gate_a_w, lru_gate_a_b, lru_gate_x_w, lru_gate_x_b, lru_lambda,
           lru_w_out, ret_w_in, ret_w_out, ffn_w_gate, ffn_w_up, ffn_w_down):
    b, seq, d = x.shape
    depth = ln_mix.shape[0]
    rows = b * seq
    assert b == SUBLANES and seq % ROW_TILE == 0 and seq % LRU_T == 0
    row = lambda v: v.reshape(1, -1)

    lru_w_in, lru_w_out, ret_w_in, ret_w_out, ffn_w_gate, ffn_w_up, ffn_w_down = (
        w.astype(BF16) for w in (lru_w_in, lru_w_out, ret_w_in, ret_w_out,
                                 ffn_w_gate, ffn_w_up, ffn_w_down))

    h = x.reshape(rows, d)
    for layer in range(depth):
        j = layer // 2
        g_final = row(ln_final) if layer == depth - 1 else None
        proj = None
        if layer % 2 == 0:
            x3 = h.reshape(b, seq, d)
            wg = [jnp.concatenate([lru_gate_a_w[j, dr], lru_gate_x_w[j, dr]],
                                  axis=-1).astype(BF16) for dr in range(2)]
            xh, hf, gb = _lru_fwd(
                x3, row(ln_mix[layer]), lru_w_in, j, lru_conv_w[j],
                row(lru_conv_b[j]), wg[0], row(lru_gate_a_b[j, 0]),
                row(lru_gate_x_b[j, 0]), row(lru_lambda[j, 0]))
            h = _lru_bwd(xh, hf, gb, x3, wg[1], row(lru_gate_a_b[j, 1]),
                         row(lru_gate_x_b[j, 1]), row(lru_lambda[j, 1]),
                         lru_w_out, j).reshape(rows, d)
        else:
            heads = RET_HEADS
            vd = ret_w_out.shape[1]
            qk = (ret_w_in.shape[2] - 2 * vd) // 2
            cos, sin = _rope_tables(seq, qk // heads)
            q, k, v, sg = _ret_in(h, row(ln_mix[layer]), ret_w_in, j,
                                  cos, sin, seq, heads, qk, vd)
            tabs = _ret_tables(heads, qk // heads, vd // heads)
            yg = _ret_core(q.reshape(b, seq, qk), k.reshape(b, seq, qk),
                           v.reshape(b, seq, vd), sg.reshape(b, seq, vd), tabs, heads)
            proj = (yg.reshape(rows, vd), ret_w_out, j)
        h = _ffn(h, row(ln_ffn[layer]), ffn_w_gate, ffn_w_up, ffn_w_down, layer,
                 g_final, proj)
    return h.reshape(b, seq, d)
```

```python
import functools
import math

import jax
import jax.numpy as jnp
from jax import lax
from jax.experimental import pallas as pl
from jax.experimental.pallas import tpu as pltpu

F32 = jnp.float32
BF16 = jnp.bfloat16

NORM_EPS = 1e-6
RSQRT_FLOOR = 1e-30
RG_LRU_C = 8.0
LRU_BLOCKS = 4
CONV_WIDTH = 4
CONV_PAD_LEFT = 2
RET_HEADS = 4
ROPE_BASE = 10000.0

SUBLANES = 8
LANES = 128
VMEM_LIMIT_BYTES = 56 * 1024 * 1024
ROW_TILE = 1024
LRU_T = 64
GATE_ROWS = 256
PROJ_COLS = 1024
OUT_COLS = 256
RET_CHUNK = 256
STATE_UNROLL = 3


def _cparams(*sem):
    return pltpu.CompilerParams(dimension_semantics=sem,
                                vmem_limit_bytes=VMEM_LIMIT_BYTES)


def _const_spec(shape):
    nd = len(shape)
    return pl.BlockSpec(shape, lambda *_: (0,) * nd, pipeline_mode=pl.Buffered(1))


def _layer_spec(shape, layer):
    nd = len(shape)
    return pl.BlockSpec((None,) + tuple(shape[1:]), lambda *_: (layer,) + (0,) * (nd - 1),
                        pipeline_mode=pl.Buffered(1))


def _rms(x, g):
    ms = jnp.mean(x * x, axis=-1, keepdims=True)
    return x * lax.rsqrt(ms + NORM_EPS) * g


def _sigmoid(x):
    return 0.5 * jnp.tanh(0.5 * x) + 0.5


def _gelu_tanh(x):
    c = math.sqrt(2.0 / math.pi)
    hx = 0.5 * x
    return hx * jnp.tanh(x * ((c * 0.044715) * (x * x) + c)) + hx


def _softplus(x):
    return jnp.maximum(x, 0.0) + jnp.log(1.0 + jnp.exp(-jnp.abs(x)))


def _lru_gate_pieces(xh_ref, wg_ref, ba_ref, bx_ref, lam_ref, a_dst, inp_dst):
    rows, width = xh_ref.shape
    bw = width // LRU_BLOCKS
    c2 = (-0.5 * RG_LRU_C * math.log2(math.e)) * _softplus(-lam_ref[...])
    ba2 = 0.5 * ba_ref[...]
    bx2 = 0.5 * bx_ref[...]

    def piece(r0, n):
        cs = slice(n * bw, (n + 1) * bw)
        xh = xh_ref[pl.ds(r0, GATE_ROWS), cs]
        z = jnp.dot(xh.astype(BF16), wg_ref[n], preferred_element_type=F32)
        tr = jnp.tanh(z[:, :bw] + ba2[:, cs])
        ti = jnp.tanh(z[:, bw:] + bx2[:, cs])
        a = jnp.exp2(c2[:, cs] * (tr + 1.0))
        om = 1.0 - a * a
        root = om * lax.rsqrt(jnp.maximum(om, RSQRT_FLOOR))
        a_dst[pl.ds(r0, GATE_ROWS), cs] = a
        inp_dst[pl.ds(r0, GATE_ROWS), cs] = ((ti + 1.0) * xh.astype(F32)) * root

    return [functools.partial(piece, r0, n)
            for r0 in range(0, rows, GATE_ROWS) for n in range(LRU_BLOCKS)]


def _lru_fwd_kernel(x_ref, g_ref, w_ref, cw_ref, cb_ref, wg_ref, ba_ref, bx_ref,
                    lam_ref, xh_ref, hf_ref, gb_ref, ext_scr, xh_scr,
                    a_scr, inp_scr, carry_scr, w_scr, hn_scr):
    i = pl.program_id(0)
    nt = pl.num_programs(0) - 2
    nb, t, _ = x_ref.shape
    rows, width = xh_ref.shape
    halo = (CONV_WIDTH - 1) * nb
    nslab = width // LANES

    @pl.when(i == 0)
    def _():
        carry_scr[...] = jnp.zeros_like(carry_scr)
        ext_scr[...] = jnp.zeros_like(ext_scr)
        w_scr[...] = w_ref[...]

    cwh = 0.5 * cw_ref[...]
    cbh = 0.5 * cb_ref[...]
    for l in range(nslab):
        ls = slice(l * LANES, (l + 1) * LANES)
        acc = cbh[:, ls] + cwh[0:1, ls] * ext_scr[l, pl.ds(0, rows), :]
        for k in range(1, CONV_WIDTH):
            acc = acc + cwh[k:k + 1, ls] * ext_scr[l, pl.ds(k * nb, rows), :]
        xh_scr[:, ls] = acc
    xh_ref[...] = xh_scr[...].astype(xh_ref.dtype)

    ext_scr[:, pl.ds(0, halo), :] = ext_scr[:, pl.ds(rows, halo), :]
    gate_pieces = _lru_gate_pieces(xh_scr, wg_ref, ba_ref, bx_ref, lam_ref,
                                   a_scr, inp_scr)
    nbs = GATE_ROWS // t
    n_dots = (nb // nbs) * (2 * width // PROJ_COLS)
    per_dot = -(-len(gate_pieces) // n_dots)
    for b0 in range(0, nb, nbs):
        hn = hn_scr.at[b0 // nbs]
        for bb in range(nbs):
            hn[pl.ds(bb * t, t), :] = _rms(x_ref[b0 + bb], g_ref[...]).astype(BF16)
        for c0 in range(0, 2 * width, PROJ_COLS):
            u = jnp.dot(hn[...], w_scr[:, c0:c0 + PROJ_COLS], preferred_element_type=F32)
            for bb in range(nbs):
                ub = u[bb * t:(bb + 1) * t]
                if c0 < width:
                    for l in range(PROJ_COLS // LANES):
                        ext_scr[c0 // LANES + l, pl.ds(halo + b0 + bb, t, stride=nb), :] = (
                            ub[:, l * LANES:(l + 1) * LANES])
                else:
                    gb_ref[b0 + bb, :, c0 - width:c0 - width + PROJ_COLS] = (
                        ub.astype(gb_ref.dtype))
            for _ in range(per_dot):
                if gate_pieces:
                    gate_pieces.pop(0)()
    for piece in gate_pieces:
        piece()

    h = jnp.where(i > 1, carry_scr[...], 0.0)
    for s in range(rows // nb):
        h = a_scr[pl.ds(s * nb, nb), :] * h + inp_scr[pl.ds(s * nb, nb), :]
        if s == 0:
            h = jnp.where(i > 1, h, 0.0)
        inp_scr[pl.ds(s * nb, nb), :] = h
    carry_scr[...] = h
    hf_ref[...] = inp_scr[...].astype(hf_ref.dtype)

    @pl.when(i == nt)
    def _():
        ext_scr[:, pl.ds(halo, nb), :] = jnp.zeros((nslab, nb, LANES), F32)

    @pl.when(i == nt + 1)
    def _():
        xh_ref[pl.ds(2 * nb, rows - 2 * nb), :] = jnp.zeros(
            (rows - 2 * nb, width), xh_ref.dtype)
        xh_ref[pl.ds(0, 2 * nb), :] = jnp.where(
            lax.broadcasted_iota(jnp.int32, (2 * nb, width), 0) < nb,
            xh_ref[pl.ds(0, 2 * nb), :], jnp.zeros((), xh_ref.dtype))


def _lru_fwd(x, g, w_in, j, conv_w, conv_b, wg, ba, bx, lam):
    nb, seq, d = x.shape
    width = w_in.shape[2] // 2
    t = LRU_T
    r = nb * t
    nt = seq // t
    halo = (CONV_WIDTH - 1) * nb
    assert CONV_WIDTH - 1 - CONV_PAD_LEFT == 1
    tile = lambda i: (0, jnp.minimum(i, nt - 1), 0)
    win = lambda i: (jnp.maximum(i - 1, 0), 0)
    return pl.pallas_call(
        _lru_fwd_kernel,
        grid=(nt + 2,),
        in_specs=[pl.BlockSpec((nb, t, d), tile), _const_spec((1, d)),
                  _layer_spec(w_in.shape, j), _const_spec(conv_w.shape),
                  _const_spec(conv_b.shape), _const_spec(wg.shape),
                  _const_spec(ba.shape), _const_spec(bx.shape), _const_spec(lam.shape)],
        out_specs=[pl.BlockSpec((r, width), win), pl.BlockSpec((r, width), win),
                   pl.BlockSpec((nb, t, width), tile)],
        out_shape=[jax.ShapeDtypeStruct(((nt + 1) * r, width), BF16),
                   jax.ShapeDtypeStruct(((nt + 1) * r, width), BF16),
                   jax.ShapeDtypeStruct((nb, seq, width), BF16)],
        scratch_shapes=[pltpu.VMEM((width // LANES, halo + r, LANES), F32),
                        pltpu.VMEM((r, width), F32),
                        pltpu.VMEM((r, width), F32),
                        pltpu.VMEM((r, width), F32),
                        pltpu.VMEM((nb, width), F32),
                        pltpu.VMEM(w_in.shape[1:], BF16),
                        pltpu.VMEM((r // GATE_ROWS, GATE_ROWS, d), BF16)],
        compiler_params=_cparams("arbitrary"),
        name="lru_fwd",
    )(x, g, w_in, conv_w, conv_b, wg, ba, bx, lam)


def _lru_bwd_kernel(xh_ref, hf_ref, gb_ref, x_ref, wg_ref, ba_ref, bx_ref,
                    lam_ref, wo_ref, o_ref, a_scr, inp_scr, hb_scr, carry_scr,
                    ysum_scr, ybm_scr, wo_scr):
    i = pl.program_id(0)
    nb, t, _ = x_ref.shape
    rows, width = xh_ref.shape
    nsteps = rows // nb
    nslab = width // LANES

    @pl.when(i == 0)
    def _():
        carry_scr[...] = jnp.zeros_like(carry_scr)
        hb_scr[...] = jnp.zeros_like(hb_scr)
        ysum_scr[:, pl.ds(rows, nb), :] = jnp.zeros((nslab, nb, LANES), F32)
        wo_scr[...] = wo_ref[...]

    for l in range(nslab):
        ls = slice(l * LANES, (l + 1) * LANES)
        ysum_scr[l, pl.ds(0, rows), :] = hb_scr[:, ls] + hf_ref[:, ls].astype(F32)

    for b in range(nb):
        for l in range(nslab):
            ls = slice(l * LANES, (l + 1) * LANES)
            yb = ysum_scr[l, pl.ds(nb + b, t, stride=nb), :]
            ybm_scr[pl.ds(b * t, t), ls] = (
                yb * _gelu_tanh(gb_ref[b, :, ls].astype(F32))).astype(BF16)
    ysum_scr[:, pl.ds(rows, nb), :] = ysum_scr[:, pl.ds(0, nb), :]

    gate_pieces = _lru_gate_pieces(xh_ref, wg_ref, ba_ref, bx_ref, lam_ref,
                                   a_scr, inp_scr)
    gate_pieces = [p for r0 in range(rows - GATE_ROWS, -1, -GATE_ROWS)
                   for p in gate_pieces[(r0 // GATE_ROWS) * LRU_BLOCKS:
                                        (r0 // GATE_ROWS + 1) * LRU_BLOCKS]]
    d_out = wo_ref.shape[1]
    per_dot = -(-len(gate_pieces) * OUT_COLS // d_out)
    for c0 in range(0, d_out, OUT_COLS):
        cs = slice(c0, c0 + OUT_COLS)
        res = jnp.dot(ybm_scr[...], wo_scr[:, cs], preferred_element_type=F32)
        for b in range(nb):
            o_ref[b, :, cs] = x_ref[b, :, cs] + res[b * t:(b + 1) * t]
        for _ in range(per_dot):
            if gate_pieces:
                gate_pieces.pop(0)()
    for piece in gate_pieces:
        piece()

    h = carry_scr[...]
    for s in range(nsteps - 1, -1, -1):
        h = a_scr[pl.ds(s * nb, nb), :] * h + inp_scr[pl.ds(s * nb, nb), :]
        hb_scr[pl.ds(s * nb, nb), :] = h
    carry_scr[...] = h


def _lru_bwd(xh, hf, gb, x, wg, ba, bx, lam, w_out, j):
    nb, seq, d = x.shape
    width = xh.shape[1]
    t = LRU_T
    r = nb * t
    nt = seq // t
    win_g = lambda i: (jnp.maximum(nt - i, 0), 0)
    win_s = lambda i: (jnp.minimum(nt - i + 1, nt), 0)
    tile = lambda i: (0, jnp.minimum(nt - i + 1, nt - 1), 0)
    return pl.pallas_call(
        _lru_bwd_kernel,
        grid=(nt + 2,),
        in_specs=[pl.BlockSpec((r, width), win_g), pl.BlockSpec((r, width), win_s),
                  pl.BlockSpec((nb, t, width), tile), pl.BlockSpec((nb, t, d), tile),
                  _const_spec(wg.shape), _const_spec(ba.shape), _const_spec(bx.shape),
                  _const_spec(lam.shape), _layer_spec(w_out.shape, j)],
        out_specs=pl.BlockSpec((nb, t, d), tile),
        out_shape=jax.ShapeDtypeStruct((nb, seq, d), F32),
        scratch_shapes=[pltpu.VMEM((r, width), F32),
                        pltpu.VMEM((r, width), F32),
                        pltpu.VMEM((r, width), F32),
                        pltpu.VMEM((nb, width), F32),
                        pltpu.VMEM((width // LANES, r + nb, LANES), F32),
                        pltpu.VMEM((r, width), BF16),
                        pltpu.VMEM(w_out.shape[1:], BF16)],
        compiler_params=_cparams("arbitrary"),
        name="lru_bwd",
    )(xh, hf, gb, x, wg, ba, bx, lam, w_out)


def _ffn_chunks(hidden):
    chunks, c = [], 0
    while c < hidden:
        size = 512 if hidden - c >= 512 else hidden - c
        chunks.append((c, size))
        c += size
    return chunks


def _ffn_kernel(*refs, final_norm, fuse_proj):
    refs = list(refs)
    h_ref = refs.pop(0)
    if fuse_proj:
        y_ref, wo_ref = refs.pop(0), refs.pop(0)
    g_ref, wg_ref, wu_ref, wd_ref = (refs.pop(0) for _ in range(4))
    gf_ref = refs.pop(0) if final_norm else None
    o_ref, hn_scr, act_scr = refs
    if fuse_proj:
        res_ref = o_ref
        res_ref[...] = h_ref[...] + jnp.dot(y_ref[...], wo_ref[...],
                                            preferred_element_type=F32)
    else:
        res_ref = h_ref
    hn_scr[...] = _rms(res_ref[...], g_ref[...]).astype(BF16)
    for c, size in _ffn_chunks(wg_ref.shape[1]):
        gt = jnp.dot(hn_scr[...], wg_ref[:, c:c + size], preferred_element_type=F32)
        up = jnp.dot(hn_scr[...], wu_ref[:, c:c + size], preferred_element_type=F32)
        act_scr[:, c:c + size] = (gt * _sigmoid(gt) * up).astype(BF16)
    y = res_ref[...] + jnp.dot(act_scr[...], wd_ref[...], preferred_element_type=F32)
    if final_norm:
        y = _rms(y, gf_ref[...])
    o_ref[...] = y


def _ffn(h, g, w_gate, w_up, w_down, layer, g_final=None, proj=None):
    rows, d = h.shape
    hidden = w_gate.shape[2]
    final_norm = g_final is not None
    fuse_proj = proj is not None
    r = ROW_TILE
    row = lambda i: (i, 0)
    in_specs = [pl.BlockSpec((r, d), row)]
    args = [h]
    scratch = [pltpu.VMEM((r, d), BF16), pltpu.VMEM((r, hidden), BF16)]
    if fuse_proj:
        y, w_out, j = proj
        in_specs += [pl.BlockSpec((r, y.shape[1]), row), _layer_spec(w_out.shape, j)]
        args += [y, w_out]
    in_specs += [_const_spec((1, d)), _layer_spec(w_gate.shape, layer),
                 _layer_spec(w_up.shape, layer), _layer_spec(w_down.shape, layer)]
    args += [g, w_gate, w_up, w_down]
    if final_norm:
        in_specs.append(_const_spec((1, d)))
        args.append(g_final)
    return pl.pallas_call(
        functools.partial(_ffn_kernel, final_norm=final_norm, fuse_proj=fuse_proj),
        grid=(rows // r,),
        in_specs=in_specs,
        out_specs=pl.BlockSpec((r, d), row),
        out_shape=jax.ShapeDtypeStruct((rows, d), F32),
        scratch_shapes=scratch,
        compiler_params=_cparams("parallel"),
        name="ffn" + ("_proj" if fuse_proj else "") + ("_final" if final_norm else ""),
    )(*args)


def _ret_in_kernel(h_ref, g_ref, w_ref, cos_ref, sin_ref, q_ref, k_ref, v_ref,
                   sg_ref, hn_scr, *, heads):
    qk = q_ref.shape[1]
    vd = v_ref.shape[1]
    hd = qk // heads
    half = hd // 2
    hn_scr[...] = _rms(h_ref[...], g_ref[...]).astype(BF16)
    cos = cos_ref[...]
    sin = sin_ref[...]
    k_scale = hd ** -0.5
    for h in range(heads):
        for which, dst, scale in ((0, q_ref, 1.0), (1, k_ref, k_scale)):
            c0 = which * qk + h * hd
            u = jnp.dot(hn_scr[...], w_ref[:, c0:c0 + hd], preferred_element_type=F32)
            x1, x2 = u[:, :half], u[:, half:]
            r1 = x1 * cos - x2 * sin
            r2 = x2 * cos + x1 * sin
            if scale != 1.0:
                r1, r2 = r1 * scale, r2 * scale
            dst[:, h * hd:h * hd + half] = r1.astype(dst.dtype)
            dst[:, h * hd + half:(h + 1) * hd] = r2.astype(dst.dtype)
    step = 512
    for c in range(0, vd, step):
        v_ref[:, c:c + step] = jnp.dot(
            hn_scr[...], w_ref[:, 2 * qk + c:2 * qk + c + step],
            preferred_element_type=F32).astype(v_ref.dtype)
    for c in range(0, vd, step):
        gt = jnp.dot(hn_scr[...], w_ref[:, 2 * qk + vd + c:2 * qk + vd + c + step],
                     preferred_element_type=F32)
        sg_ref[:, c:c + step] = (gt * _sigmoid(gt)).astype(sg_ref.dtype)


def _ret_in(h, g, w_in, j, cos, sin, seq, heads, qk, vd):
    rows, d = h.shape
    r = ROW_TILE
    half = cos.shape[1]
    per_seq = seq // r
    pos = lambda i: (i % per_seq, 0)
    row = lambda i: (i, 0)
    return pl.pallas_call(
        functools.partial(_ret_in_kernel, heads=heads),
        grid=(rows // r,),
        in_specs=[pl.BlockSpec((r, d), row), _const_spec((1, d)),
                  _layer_spec(w_in.shape, j),
                  pl.BlockSpec((r, half), pos), pl.BlockSpec((r, half), pos)],
        out_specs=[pl.BlockSpec((r, qk), row), pl.BlockSpec((r, qk), row),
                   pl.BlockSpec((r, vd), row), pl.BlockSpec((r, vd), row)],
        out_shape=[jax.ShapeDtypeStruct((rows, qk), BF16),
                   jax.ShapeDtypeStruct((rows, qk), BF16),
                   jax.ShapeDtypeStruct((rows, vd), BF16),
                   jax.ShapeDtypeStruct((rows, vd), BF16)],
        scratch_shapes=[pltpu.VMEM((r, d), BF16)],
        compiler_params=_cparams("parallel"),
        name="ret_in",
    )(h, g, w_in, cos, sin)


def _lane_tiled(t, n):
    return jnp.concatenate([t] * (n // LANES), axis=1)


def _ret_core_kernel(q_ref, k_ref, v_ref, sg_ref, dec_ref, zf_ref, zb_ref, xf_ref,
                     xb_ref, gc_ref, o_ref, sf_scr, sb_scr, stf_scr, stb_scr, y_scr):
    seq, dk = q_ref.shape
    dv = v_ref.shape[1]
    c = dec_ref.shape[0]
    nc = seq // c
    gcf = gc_ref[pl.ds(0, 1), :]
    gcb = gc_ref[pl.ds(1, 1), :]

    def kv_update(state, n, z_ref, gcd):
        r0 = pl.multiple_of(n * c, c)
        kz = (k_ref[pl.ds(r0, c), :].astype(F32) * _lane_tiled(z_ref[...], dk)).astype(BF16)
        upd = lax.dot_general(kz, v_ref[pl.ds(r0, c), :], (((0,), (0,)), ((), ())),
                              preferred_element_type=F32)
        return state * gcd + upd

    stf_scr[...] = jnp.zeros_like(stf_scr)
    stb_scr[...] = jnp.zeros_like(stb_scr)
    sf_scr[0] = jnp.zeros((dk, dv), BF16)
    sb_scr[nc - 1] = jnp.zeros((dk, dv), BF16)

    def state_body(s, carry):
        stf_scr[...] = kv_update(stf_scr[...], s, zf_ref, gcf)
        sf_scr[s + 1] = stf_scr[...].astype(BF16)
        stb_scr[...] = kv_update(stb_scr[...], nc - 1 - s, zb_ref, gcb)
        sb_scr[nc - 2 - s] = stb_scr[...].astype(BF16)
        return carry

    lax.fori_loop(0, nc - 1, state_body, 0, unroll=STATE_UNROLL)

    def finish(n):
        r0 = pl.multiple_of(n * c, c)
        y = y_scr[...]
        y = y * lax.rsqrt(jnp.mean(y * y, axis=-1, keepdims=True) + NORM_EPS)
        o_ref[pl.ds(r0, c), :] = (y * sg_ref[pl.ds(r0, c), :].astype(F32)).astype(o_ref.dtype)

    def out_body(n, carry):
        finish(jnp.maximum(n - 1, 0))
        r0 = pl.multiple_of(n * c, c)
        q = q_ref[pl.ds(r0, c), :]
        scores = lax.dot_general(q, k_ref[pl.ds(r0, c), :], (((1,), (1,)), ((), ())),
                                 preferred_element_type=F32)
        qf32 = q.astype(F32)
        qf = (qf32 * _lane_tiled(xf_ref[...], dk)).astype(BF16)
        qb = (qf32 * _lane_tiled(xb_ref[...], dk)).astype(BF16)
        p = (scores * dec_ref[...]).astype(BF16)
        lhs = jnp.concatenate([qf, qb, p], axis=1)
        rhs = jnp.concatenate([sf_scr[n], sb_scr[n], v_ref[pl.ds(r0, c), :]], axis=0)
        y_scr[...] = jnp.dot(lhs, rhs, preferred_element_type=F32)
        return carry

    y_scr[...] = jnp.zeros_like(y_scr)
    lax.fori_loop(0, nc, out_body, 0, unroll=4)
    finish(nc - 1)


def _ret_core(q, k, v, sg, tabs, heads):
    b, seq, qk = q.shape
    vd = v.shape[2]
    dk, dv = qk // heads, vd // heads
    c = RET_CHUNK
    dec, zf, zb, xf, xb, gc = tabs
    bh = lambda i, j: (i, 0, j)
    hh = lambda i, j: (j, 0, 0)
    return pl.pallas_call(
        _ret_core_kernel,
        grid=(b, heads),
        in_specs=[pl.BlockSpec((None, seq, dk), bh), pl.BlockSpec((None, seq, dk), bh),
                  pl.BlockSpec((None, seq, dv), bh), pl.BlockSpec((None, seq, dv), bh),
                  pl.BlockSpec((None, c, c), hh),
                  pl.BlockSpec((None, c, LANES), hh), pl.BlockSpec((None, c, LANES), hh),
                  pl.BlockSpec((None, c, LANES), hh), pl.BlockSpec((None, c, LANES), hh),
                  pl.BlockSpec((None, SUBLANES, dv), hh)],
        out_specs=pl.BlockSpec((None, seq, dv), bh),
        out_shape=jax.ShapeDtypeStruct((b, seq, vd), BF16),
        scratch_shapes=[pltpu.VMEM((seq // c, dk, dv), BF16),
                        pltpu.VMEM((seq // c, dk, dv), BF16),
                        pltpu.VMEM((dk, dv), F32),
                        pltpu.VMEM((dk, dv), F32),
                        pltpu.VMEM((c, dv), F32)],
        compiler_params=_cparams("parallel", "parallel"),
        name="ret_core",
    )(q, k, v, sg, dec, zf, zb, xf, xb, gc)


def _ret_tables(heads, dk, dv):
    c = RET_CHUNK
    log_gf = jnp.log1p(-jnp.exp2(-5.0 - jnp.arange(heads, dtype=F32)))
    log_gb = log_gf[::-1]
    pos = jnp.arange(c, dtype=F32)
    diff = pos[:, None] - pos[None, :]
    lower = diff >= 0
    upper = diff < 0
    dec_f = jnp.where(lower[None], jnp.exp(jnp.where(lower, diff, 0.0)[None]
                                           * log_gf[:, None, None]), 0.0)
    dec_b = jnp.where(upper[None], jnp.exp(jnp.where(upper, -diff, 0.0)[None]
                                           * log_gb[:, None, None]), 0.0)
    dec = dec_f + dec_b
    zeta_f = jnp.exp((c - 1.0 - pos)[None, :] * log_gf[:, None])
    xi_f = jnp.exp((pos + 1.0)[None, :] * log_gf[:, None])
    zeta_b = jnp.exp(pos[None, :] * log_gb[:, None])
    xi_b = jnp.exp((c - pos)[None, :] * log_gb[:, None])
    wide = lambda t: jnp.broadcast_to(t[:, :, None], (heads, c, LANES))
    g_f = jnp.exp(c * log_gf)
    g_b = jnp.exp(c * log_gb)
    gc = jnp.zeros((heads, SUBLANES, dv), F32)
    gc = gc.at[:, 0, :].set(g_f[:, None]).at[:, 1, :].set(g_b[:, None])
    return (dec, wide(zeta_f), wide(zeta_b), wide(xi_f), wide(xi_b), gc)


def _rope_tables(seq, hd):
    half = hd // 2
    inv_freq = 1.0 / (ROPE_BASE ** jnp.linspace(0.0, 1.0, half, dtype=F32))
    ang = jnp.arange(seq, dtype=F32)[:, None] * inv_freq[None, :]
    return jnp.cos(ang), jnp.sin(ang)


def kernel(x, ln_mix, ln_ffn, ln_final, lru_w_in, lru_conv_w, lru_conv_b,
           lru_gate_a_w, lru_gate_a_b, lru_gate_x_w, lru_gate_x_b, lru_lambda,
           lru_w_out, ret_w_in, ret_w_out, ffn_w_gate, ffn_w_up, ffn_w_down):
    b, seq, d = x.shape
    depth = ln_mix.shape[0]
    rows = b * seq
    assert b == SUBLANES and seq % ROW_TILE == 0 and seq % LRU_T == 0
    row = lambda v: v.reshape(1, -1)

    lru_w_in, lru_w_out, ret_w_in, ret_w_out, ffn_w_gate, ffn_w_up, ffn_w_down = (
        w.astype(BF16) for w in (lru_w_in, lru_w_out, ret_w_in, ret_w_out,
                                 ffn_w_gate, ffn_w_up, ffn_w_down))

    h = x.reshape(rows, d)
    for layer in range(depth):
        j = layer // 2
        g_final = row(ln_final) if layer == depth - 1 else None
        proj = None
        if layer % 2 == 0:
            x3 = h.reshape(b, seq, d)
            wg = [jnp.concatenate([lru_gate_a_w[j, dr], lru_gate_x_w[j, dr]],
                                  axis=-1).astype(BF16) for dr in range(2)]
            xh, hf, gb = _lru_fwd(
                x3, row(ln_mix[layer]), lru_w_in, j, lru_conv_w[j],
                row(lru_conv_b[j]), wg[0], row(lru_gate_a_b[j, 0]),
                row(lru_gate_x_b[j, 0]), row(lru_lambda[j, 0]))
            h = _lru_bwd(xh, hf, gb, x3, wg[1], row(lru_gate_a_b[j, 1]),
                         row(lru_gate_x_b[j, 1]), row(lru_lambda[j, 1]),
                         lru_w_out, j).reshape(rows, d)
        else:
            heads = RET_HEADS
            vd = ret_w_out.shape[1]
            qk = (ret_w_in.shape[2] - 2 * vd) // 2
            cos, sin = _rope_tables(seq, qk // heads)
            q, k, v, sg = _ret_in(h, row(ln_mix[layer]), ret_w_in, j,
                                  cos, sin, seq, heads, qk, vd)
            tabs = _ret_tables(heads, qk // heads, vd // heads)
            yg = _ret_core(q.reshape(b, seq, qk), k.reshape(b, seq, qk),
                           v.reshape(b, seq, vd), sg.reshape(b, seq, vd), tabs, heads)
            proj = (yg.reshape(rows, vd), ret_w_out, j)
        h = _ffn(h, row(ln_ffn[layer]), ffn_w_gate, ffn_w_up, ffn_w_down, layer,
                 g_final, proj)
    return h.reshape(b, seq, d)
```

```python
import functools
import math

import jax
import jax.numpy as jnp
from jax import lax
from jax.experimental import pallas as pl
from jax.experimental.pallas import tpu as pltpu

F32 = jnp.float32
BF16 = jnp.bfloat16

NORM_EPS = 1e-6
RSQRT_FLOOR = 1e-30
RG_LRU_C = 8.0
LRU_BLOCKS = 4
CONV_WIDTH = 4
CONV_PAD_LEFT = 2
RET_HEADS = 4
ROPE_BASE = 10000.0

SUBLANES = 8
LANES = 128
VMEM_LIMIT_BYTES = 56 * 1024 * 1024
ROW_TILE = 1024
LRU_T = 64
GATE_ROWS = 256
PROJ_COLS = 1024
OUT_COLS = 256
RET_CHUNK = 256
STATE_UNROLL = 3


def _cparams(*sem):
    return pltpu.CompilerParams(dimension_semantics=sem,
                                vmem_limit_bytes=VMEM_LIMIT_BYTES)


def _const_spec(shape):
    nd = len(shape)
    return pl.BlockSpec(shape, lambda *_: (0,) * nd, pipeline_mode=pl.Buffered(1))


def _layer_spec(shape, layer):
    nd = len(shape)
    return pl.BlockSpec((None,) + tuple(shape[1:]), lambda *_: (layer,) + (0,) * (nd - 1),
                        pipeline_mode=pl.Buffered(1))


def _gate_spec(shape, layer, direction):
    return pl.BlockSpec((None, None) + tuple(shape[2:]),
                        lambda *_: (layer, direction, 0, 0, 0),
                        pipeline_mode=pl.Buffered(1))


def _gate_scratch_shape(shape):
    blocks, k, n = shape[2:]
    return (blocks, k, 2 * n)


def _fill_gate_weights(wg_scr, ga_ref, gx_ref):
    n = ga_ref.shape[2]
    wg_scr[:, :, :n] = ga_ref[...].astype(BF16)
    wg_scr[:, :, n:] = gx_ref[...].astype(BF16)


def _rms(x, g):
    ms = jnp.mean(x * x, axis=-1, keepdims=True)
    return x * lax.rsqrt(ms + NORM_EPS) * g


def _sigmoid(x):
    return 0.5 * jnp.tanh(0.5 * x) + 0.5


def _gelu_tanh(x):
    c = math.sqrt(2.0 / math.pi)
    hx = 0.5 * x
    return hx * jnp.tanh(x * ((c * 0.044715) * (x * x) + c)) + hx


def _softplus(x):
    return jnp.maximum(x, 0.0) + jnp.log(1.0 + jnp.exp(-jnp.abs(x)))


def _lru_gate_pieces(xh_ref, wg_ref, ba_ref, bx_ref, lam_ref, a_dst, inp_dst):
    rows, width = xh_ref.shape
    bw = width // LRU_BLOCKS
    c2 = (-0.5 * RG_LRU_C * math.log2(math.e)) * _softplus(-lam_ref[...])
    ba2 = 0.5 * ba_ref[...]
    bx2 = 0.5 * bx_ref[...]

    def piece(r0, n):
        cs = slice(n * bw, (n + 1) * bw)
        xh = xh_ref[pl.ds(r0, GATE_ROWS), cs]
        z = jnp.dot(xh.astype(BF16), wg_ref[n], preferred_element_type=F32)
        tr = jnp.tanh(z[:, :bw] + ba2[:, cs])
        ti = jnp.tanh(z[:, bw:] + bx2[:, cs])
        a = jnp.exp2(c2[:, cs] * (tr + 1.0))
        om = 1.0 - a * a
        root = om * lax.rsqrt(jnp.maximum(om, RSQRT_FLOOR))
        a_dst[pl.ds(r0, GATE_ROWS), cs] = a
        inp_dst[pl.ds(r0, GATE_ROWS), cs] = ((ti + 1.0) * xh.astype(F32)) * root

    return [functools.partial(piece, r0, n)
            for r0 in range(0, rows, GATE_ROWS) for n in range(LRU_BLOCKS)]


def _lru_fwd_kernel(x_ref, g_ref, w_ref, cw_ref, cb_ref, ga_ref, gx_ref, ba_ref, bx_ref,
                    lam_ref, xh_ref, hf_ref, gb_ref, ext_scr, xh_scr,
                    a_scr, inp_scr, carry_scr, w_scr, wg_scr, hn_scr):
    i = pl.program_id(0)
    nt = pl.num_programs(0) - 2
    nb, t, _ = x_ref.shape
    rows, width = xh_ref.shape
    halo = (CONV_WIDTH - 1) * nb
    nslab = width // LANES

    @pl.when(i == 0)
    def _():
        carry_scr[...] = jnp.zeros_like(carry_scr)
        ext_scr[...] = jnp.zeros_like(ext_scr)
        w_scr[...] = w_ref[...].astype(BF16)
        _fill_gate_weights(wg_scr, ga_ref, gx_ref)

    cwh = 0.5 * cw_ref[...]
    cbh = 0.5 * cb_ref[...]
    for l in range(nslab):
        ls = slice(l * LANES, (l + 1) * LANES)
        acc = cbh[:, ls] + cwh[0:1, ls] * ext_scr[l, pl.ds(0, rows), :]
        for k in range(1, CONV_WIDTH):
            acc = acc + cwh[k:k + 1, ls] * ext_scr[l, pl.ds(k * nb, rows), :]
        xh_scr[:, ls] = acc
    xh_ref[...] = xh_scr[...].astype(xh_ref.dtype)

    ext_scr[:, pl.ds(0, halo), :] = ext_scr[:, pl.ds(rows, halo), :]
    gate_pieces = _lru_gate_pieces(xh_scr, wg_scr, ba_ref, bx_ref, lam_ref,
                                   a_scr, inp_scr)
    nbs = GATE_ROWS // t
    n_dots = (nb // nbs) * (2 * width // PROJ_COLS)
    per_dot = -(-len(gate_pieces) // n_dots)
    for b0 in range(0, nb, nbs):
        hn = hn_scr.at[b0 // nbs]
        for bb in range(nbs):
            hn[pl.ds(bb * t, t), :] = _rms(x_ref[b0 + bb], g_ref[...]).astype(BF16)
        for c0 in range(0, 2 * width, PROJ_COLS):
            u = jnp.dot(hn[...], w_scr[:, c0:c0 + PROJ_COLS], preferred_element_type=F32)
            for bb in range(nbs):
                ub = u[bb * t:(bb + 1) * t]
                if c0 < width:
                    for l in range(PROJ_COLS // LANES):
                        ext_scr[c0 // LANES + l, pl.ds(halo + b0 + bb, t, stride=nb), :] = (
                            ub[:, l * LANES:(l + 1) * LANES])
                else:
                    gb_ref[b0 + bb, :, c0 - width:c0 - width + PROJ_COLS] = (
                        ub.astype(gb_ref.dtype))
            for _ in range(per_dot):
                if gate_pieces:
                    gate_pieces.pop(0)()
    for piece in gate_pieces:
        piece()

    h = jnp.where(i > 1, carry_scr[...], 0.0)
    for s in range(rows // nb):
        h = a_scr[pl.ds(s * nb, nb), :] * h + inp_scr[pl.ds(s * nb, nb), :]
        if s == 0:
            h = jnp.where(i > 1, h, 0.0)
        inp_scr[pl.ds(s * nb, nb), :] = h
    carry_scr[...] = h
    hf_ref[...] = inp_scr[...].astype(hf_ref.dtype)

    @pl.when(i == nt)
    def _():
        ext_scr[:, pl.ds(halo, nb), :] = jnp.zeros((nslab, nb, LANES), F32)

    @pl.when(i == nt + 1)
    def _():
        xh_ref[pl.ds(2 * nb, rows - 2 * nb), :] = jnp.zeros(
            (rows - 2 * nb, width), xh_ref.dtype)
        xh_ref[pl.ds(0, 2 * nb), :] = jnp.where(
            lax.broadcasted_iota(jnp.int32, (2 * nb, width), 0) < nb,
            xh_ref[pl.ds(0, 2 * nb), :], jnp.zeros((), xh_ref.dtype))


def _lru_fwd(x, g, w_in, j, conv_w, conv_b, ga_w, gx_w, ba, bx, lam):
    nb, seq, d = x.shape
    width = w_in.shape[2] // 2
    t = LRU_T
    r = nb * t
    nt = seq // t
    halo = (CONV_WIDTH - 1) * nb
    assert CONV_WIDTH - 1 - CONV_PAD_LEFT == 1
    tile = lambda i: (0, jnp.minimum(i, nt - 1), 0)
    win = lambda i: (jnp.maximum(i - 1, 0), 0)
    return pl.pallas_call(
        _lru_fwd_kernel,
        grid=(nt + 2,),
        in_specs=[pl.BlockSpec((nb, t, d), tile), _const_spec((1, d)),
                  _layer_spec(w_in.shape, j), _const_spec(conv_w.shape),
                  _const_spec(conv_b.shape), _gate_spec(ga_w.shape, j, 0),
                  _gate_spec(gx_w.shape, j, 0),
                  _const_spec(ba.shape), _const_spec(bx.shape), _const_spec(lam.shape)],
        out_specs=[pl.BlockSpec((r, width), win), pl.BlockSpec((r, width), win),
                   pl.BlockSpec((nb, t, width), tile)],
        out_shape=[jax.ShapeDtypeStruct(((nt + 1) * r, width), BF16),
                   jax.ShapeDtypeStruct(((nt + 1) * r, width), BF16),
                   jax.ShapeDtypeStruct((nb, seq, width), BF16)],
        scratch_shapes=[pltpu.VMEM((width // LANES, halo + r, LANES), F32),
                        pltpu.VMEM((r, width), F32),
                        pltpu.VMEM((r, width), F32),
                        pltpu.VMEM((r, width), F32),
                        pltpu.VMEM((nb, width), F32),
                        pltpu.VMEM(w_in.shape[1:], BF16),
                        pltpu.VMEM(_gate_scratch_shape(ga_w.shape), BF16),
                        pltpu.VMEM((r // GATE_ROWS, GATE_ROWS, d), BF16)],
        compiler_params=_cparams("arbitrary"),
        name="lru_fwd",
    )(x, g, w_in, conv_w, conv_b, ga_w, gx_w, ba, bx, lam)


def _lru_bwd_kernel(xh_ref, hf_ref, gb_ref, x_ref, ga_ref, gx_ref, ba_ref, bx_ref,
                    lam_ref, wo_ref, o_ref, a_scr, inp_scr, hb_scr, carry_scr,
                    ysum_scr, ybm_scr, wo_scr, wg_scr):
    i = pl.program_id(0)
    nb, t, _ = x_ref.shape
    rows, width = xh_ref.shape
    nsteps = rows // nb
    nslab = width // LANES

    @pl.when(i == 0)
    def _():
        carry_scr[...] = jnp.zeros_like(carry_scr)
        hb_scr[...] = jnp.zeros_like(hb_scr)
        ysum_scr[:, pl.ds(rows, nb), :] = jnp.zeros((nslab, nb, LANES), F32)
        wo_scr[...] = wo_ref[...].astype(BF16)
        _fill_gate_weights(wg_scr, ga_ref, gx_ref)

    for l in range(nslab):
        ls = slice(l * LANES, (l + 1) * LANES)
        ysum_scr[l, pl.ds(0, rows), :] = hb_scr[:, ls] + hf_ref[:, ls].astype(F32)

    for b in range(nb):
        for l in range(nslab):
            ls = slice(l * LANES, (l + 1) * LANES)
            yb = ysum_scr[l, pl.ds(nb + b, t, stride=nb), :]
            ybm_scr[pl.ds(b * t, t), ls] = (
                yb * _gelu_tanh(gb_ref[b, :, ls].astype(F32))).astype(BF16)
    ysum_scr[:, pl.ds(rows, nb), :] = ysum_scr[:, pl.ds(0, nb), :]

    gate_pieces = _lru_gate_pieces(xh_ref, wg_scr, ba_ref, bx_ref, lam_ref,
                                   a_scr, inp_scr)
    gate_pieces = [p for r0 in range(rows - GATE_ROWS, -1, -GATE_ROWS)
                   for p in gate_pieces[(r0 // GATE_ROWS) * LRU_BLOCKS:
                                        (r0 // GATE_ROWS + 1) * LRU_BLOCKS]]
    d_out = wo_ref.shape[1]
    per_dot = -(-len(gate_pieces) * OUT_COLS // d_out)
    for c0 in range(0, d_out, OUT_COLS):
        cs = slice(c0, c0 + OUT_COLS)
        res = jnp.dot(ybm_scr[...], wo_scr[:, cs], preferred_element_type=F32)
        for b in range(nb):
            o_ref[b, :, cs] = x_ref[b, :, cs] + res[b * t:(b + 1) * t]
        for _ in range(per_dot):
            if gate_pieces:
                gate_pieces.pop(0)()
    for piece in gate_pieces:
        piece()

    h = carry_scr[...]
    for s in range(nsteps - 1, -1, -1):
        h = a_scr[pl.ds(s * nb, nb), :] * h + inp_scr[pl.ds(s * nb, nb), :]
        hb_scr[pl.ds(s * nb, nb), :] = h
    carry_scr[...] = h


def _lru_bwd(xh, hf, gb, x, ga_w, gx_w, ba, bx, lam, w_out, j):
    nb, seq, d = x.shape
    width = xh.shape[1]
    t = LRU_T
    r = nb * t
    nt = seq // t
    win_g = lambda i: (jnp.maximum(nt - i, 0), 0)
    win_s = lambda i: (jnp.minimum(nt - i + 1, nt), 0)
    tile = lambda i: (0, jnp.minimum(nt - i + 1, nt - 1), 0)
    return pl.pallas_call(
        _lru_bwd_kernel,
        grid=(nt + 2,),
        in_specs=[pl.BlockSpec((r, width), win_g), pl.BlockSpec((r, width), win_s),
                  pl.BlockSpec((nb, t, width), tile), pl.BlockSpec((nb, t, d), tile),
                  _gate_spec(ga_w.shape, j, 1), _gate_spec(gx_w.shape, j, 1),
                  _const_spec(ba.shape), _const_spec(bx.shape),
                  _const_spec(lam.shape), _layer_spec(w_out.shape, j)],
        out_specs=pl.BlockSpec((nb, t, d), tile),
        out_shape=jax.ShapeDtypeStruct((nb, seq, d), F32),
        scratch_shapes=[pltpu.VMEM((r, width), F32),
                        pltpu.VMEM((r, width), F32),
                        pltpu.VMEM((r, width), F32),
                        pltpu.VMEM((nb, width), F32),
                        pltpu.VMEM((width // LANES, r + nb, LANES), F32),
                        pltpu.VMEM((r, width), BF16),
                        pltpu.VMEM(w_out.shape[1:], BF16),
                        pltpu.VMEM(_gate_scratch_shape(ga_w.shape), BF16)],
        compiler_params=_cparams("arbitrary"),
        name="lru_bwd",
    )(xh, hf, gb, x, ga_w, gx_w, ba, bx, lam, w_out)


def _ffn_chunks(hidden):
    chunks, c = [], 0
    while c < hidden:
        size = 512 if hidden - c >= 512 else hidden - c
        chunks.append((c, size))
        c += size
    return chunks


def _ffn_kernel(*refs, final_norm, fuse_proj):
    refs = list(refs)
    h_ref = refs.pop(0)
    if fuse_proj:
        y_ref, wo_ref = refs.pop(0), refs.pop(0)
    g_ref, wg_ref, wu_ref, wd_ref = (refs.pop(0) for _ in range(4))
    gf_ref = refs.pop(0) if final_norm else None
    o_ref, hn_scr, act_scr = refs
    if fuse_proj:
        res_ref = o_ref
        res_ref[...] = h_ref[...] + jnp.dot(y_ref[...], wo_ref[...],
                                            preferred_element_type=F32)
    else:
        res_ref = h_ref
    hn_scr[...] = _rms(res_ref[...], g_ref[...]).astype(BF16)
    for c, size in _ffn_chunks(wg_ref.shape[1]):
        gt = jnp.dot(hn_scr[...], wg_ref[:, c:c + size], preferred_element_type=F32)
        up = jnp.dot(hn_scr[...], wu_ref[:, c:c + size], preferred_element_type=F32)
        act_scr[:, c:c + size] = (gt * _sigmoid(gt) * up).astype(BF16)
    y = res_ref[...] + jnp.dot(act_scr[...], wd_ref[...], preferred_element_type=F32)
    if final_norm:
        y = _rms(y, gf_ref[...])
    o_ref[...] = y


def _ffn(h, g, w_gate, w_up, w_down, layer, g_final=None, proj=None):
    rows, d = h.shape
    hidden = w_gate.shape[2]
    final_norm = g_final is not None
    fuse_proj = proj is not None
    r = ROW_TILE
    row = lambda i: (i, 0)
    in_specs = [pl.BlockSpec((r, d), row)]
    args = [h]
    scratch = [pltpu.VMEM((r, d), BF16), pltpu.VMEM((r, hidden), BF16)]
    if fuse_proj:
        y, w_out, j = proj
        in_specs += [pl.BlockSpec((r, y.shape[1]), row), _layer_spec(w_out.shape, j)]
        args += [y, w_out]
    in_specs += [_const_spec((1, d)), _layer_spec(w_gate.shape, layer),
                 _layer_spec(w_up.shape, layer), _layer_spec(w_down.shape, layer)]
    args += [g, w_gate, w_up, w_down]
    if final_norm:
        in_specs.append(_const_spec((1, d)))
        args.append(g_final)
    return pl.pallas_call(
        functools.partial(_ffn_kernel, final_norm=final_norm, fuse_proj=fuse_proj),
        grid=(rows // r,),
        in_specs=in_specs,
        out_specs=pl.BlockSpec((r, d), row),
        out_shape=jax.ShapeDtypeStruct((rows, d), F32),
        scratch_shapes=scratch,
        compiler_params=_cparams("parallel"),
        name="ffn" + ("_proj" if fuse_proj else "") + ("_final" if final_norm else ""),
    )(*args)


def _ret_in_kernel(h_ref, g_ref, w_ref, cos_ref, sin_ref, q_ref, k_ref, v_ref,
                   sg_ref, hn_scr, *, heads):
    qk = q_ref.shape[1]
    vd = v_ref.shape[1]
    hd = qk // heads
    half = hd // 2
    hn_scr[...] = _rms(h_ref[...], g_ref[...]).astype(BF16)
    cos = cos_ref[...]
    sin = sin_ref[...]
    k_scale = hd ** -0.5
    for h in range(heads):
        for which, dst, scale in ((0, q_ref, 1.0), (1, k_ref, k_scale)):
            c0 = which * qk + h * hd
            u = jnp.dot(hn_scr[...], w_ref[:, c0:c0 + hd], preferred_element_type=F32)
            x1, x2 = u[:, :half], u[:, half:]
            r1 = x1 * cos - x2 * sin
            r2 = x2 * cos + x1 * sin
            if scale != 1.0:
                r1, r2 = r1 * scale, r2 * scale
            dst[:, h * hd:h * hd + half] = r1.astype(dst.dtype)
            dst[:, h * hd + half:(h + 1) * hd] = r2.astype(dst.dtype)
    step = 512
    for c in range(0, vd, step):
        v_ref[:, c:c + step] = jnp.dot(
            hn_scr[...], w_ref[:, 2 * qk + c:2 * qk + c + step],
            preferred_element_type=F32).astype(v_ref.dtype)
    for c in range(0, vd, step):
        gt = jnp.dot(hn_scr[...], w_ref[:, 2 * qk + vd + c:2 * qk + vd + c + step],
                     preferred_element_type=F32)
        sg_ref[:, c:c + step] = (gt * _sigmoid(gt)).astype(sg_ref.dtype)


def _ret_in(h, g, w_in, j, cos, sin, seq, heads, qk, vd):
    rows, d = h.shape
    r = ROW_TILE
    half = cos.shape[1]
    per_seq = seq // r
    pos = lambda i: (i % per_seq, 0)
    row = lambda i: (i, 0)
    return pl.pallas_call(
        functools.partial(_ret_in_kernel, heads=heads),
        grid=(rows // r,),
        in_specs=[pl.BlockSpec((r, d), row), _const_spec((1, d)),
                  _layer_spec(w_in.shape, j),
                  pl.BlockSpec((r, half), pos), pl.BlockSpec((r, half), pos)],
        out_specs=[pl.BlockSpec((r, qk), row), pl.BlockSpec((r, qk), row),
                   pl.BlockSpec((r, vd), row), pl.BlockSpec((r, vd), row)],
        out_shape=[jax.ShapeDtypeStruct((rows, qk), BF16),
                   jax.ShapeDtypeStruct((rows, qk), BF16),
                   jax.ShapeDtypeStruct((rows, vd), BF16),
                   jax.ShapeDtypeStruct((rows, vd), BF16)],
        scratch_shapes=[pltpu.VMEM((r, d), BF16)],
        compiler_params=_cparams("parallel"),
        name="ret_in",
    )(h, g, w_in, cos, sin)


def _lane_tiled(t, n):
    return jnp.concatenate([t] * (n // LANES), axis=1)


def _ret_core_kernel(q_ref, k_ref, v_ref, sg_ref, dec_ref, zf_ref, zb_ref, xf_ref,
                     xb_ref, gc_ref, o_ref, sf_scr, sb_scr, stf_scr, stb_scr, y_scr):
    seq, dk = q_ref.shape
    dv = v_ref.shape[1]
    c = dec_ref.shape[0]
    nc = seq // c
    gcf = gc_ref[pl.ds(0, 1), :]
    gcb = gc_ref[pl.ds(1, 1), :]

    def kv_update(state, n, z_ref, gcd):
        r0 = pl.multiple_of(n * c, c)
        kz = (k_ref[pl.ds(r0, c), :].astype(F32) * _lane_tiled(z_ref[...], dk)).astype(BF16)
        upd = lax.dot_general(kz, v_ref[pl.ds(r0, c), :], (((0,), (0,)), ((), ())),
                              preferred_element_type=F32)
        return state * gcd + upd

    stf_scr[...] = jnp.zeros_like(stf_scr)
    stb_scr[...] = jnp.zeros_like(stb_scr)
    sf_scr[0] = jnp.zeros((dk, dv), BF16)
    sb_scr[nc - 1] = jnp.zeros((dk, dv), BF16)

    def state_body(s, carry):
        stf_scr[...] = kv_update(stf_scr[...], s, zf_ref, gcf)
        sf_scr[s + 1] = stf_scr[...].astype(BF16)
        stb_scr[...] = kv_update(stb_scr[...], nc - 1 - s, zb_ref, gcb)
        sb_scr[nc - 2 - s] = stb_scr[...].astype(BF16)
        return carry

    lax.fori_loop(0, nc - 1, state_body, 0, unroll=STATE_UNROLL)

    def finish(n):
        r0 = pl.multiple_of(n * c, c)
        y = y_scr[...]
        y = y * lax.rsqrt(jnp.mean(y * y, axis=-1, keepdims=True) + NORM_EPS)
        o_ref[pl.ds(r0, c), :] = (y * sg_ref[pl.ds(r0, c), :].astype(F32)).astype(o_ref.dtype)

    def out_body(n, carry):
        finish(jnp.maximum(n - 1, 0))
        r0 = pl.multiple_of(n * c, c)
        q = q_ref[pl.ds(r0, c), :]
        scores = lax.dot_general(q, k_ref[pl.ds(r0, c), :], (((1,), (1,)), ((), ())),
                                 preferred_element_type=F32)
        qf32 = q.astype(F32)
        qf = (qf32 * _lane_tiled(xf_ref[...], dk)).astype(BF16)
        qb = (qf32 * _lane_tiled(xb_ref[...], dk)).astype(BF16)
        p = (scores * dec_ref[...]).astype(BF16)
        lhs = jnp.concatenate([qf, qb, p], axis=1)
        rhs = jnp.concatenate([sf_scr[n], sb_scr[n], v_ref[pl.ds(r0, c), :]], axis=0)
        y_scr[...] = jnp.dot(lhs, rhs, preferred_element_type=F32)
        return carry

    y_scr[...] = jnp.zeros_like(y_scr)
    lax.fori_loop(0, nc, out_body, 0, unroll=4)
    finish(nc - 1)


def _ret_core(q, k, v, sg, tabs, heads):
    b, seq, qk = q.shape
    vd = v.shape[2]
    dk, dv = qk // heads, vd // heads
    c = RET_CHUNK
    dec, zf, zb, xf, xb, gc = tabs
    bh = lambda i, j: (i, 0, j)
    hh = lambda i, j: (j, 0, 0)
    return pl.pallas_call(
        _ret_core_kernel,
        grid=(b, heads),
        in_specs=[pl.BlockSpec((None, seq, dk), bh), pl.BlockSpec((None, seq, dk), bh),
                  pl.BlockSpec((None, seq, dv), bh), pl.BlockSpec((None, seq, dv), bh),
                  pl.BlockSpec((None, c, c), hh),
                  pl.BlockSpec((None, c, LANES), hh), pl.BlockSpec((None, c, LANES), hh),
                  pl.BlockSpec((None, c, LANES), hh), pl.BlockSpec((None, c, LANES), hh),
                  pl.BlockSpec((None, SUBLANES, dv), hh)],
        out_specs=pl.BlockSpec((None, seq, dv), bh),
        out_shape=jax.ShapeDtypeStruct((b, seq, vd), BF16),
        scratch_shapes=[pltpu.VMEM((seq // c, dk, dv), BF16),
                        pltpu.VMEM((seq // c, dk, dv), BF16),
                        pltpu.VMEM((dk, dv), F32),
                        pltpu.VMEM((dk, dv), F32),
                        pltpu.VMEM((c, dv), F32)],
        compiler_params=_cparams("parallel", "parallel"),
        name="ret_core",
    )(q, k, v, sg, dec, zf, zb, xf, xb, gc)


def _ret_tables(heads, dk, dv):
    c = RET_CHUNK
    log_gf = jnp.log1p(-jnp.exp2(-5.0 - jnp.arange(heads, dtype=F32)))
    log_gb = log_gf[::-1]
    pos = jnp.arange(c, dtype=F32)
    diff = pos[:, None] - pos[None, :]
    lower = diff >= 0
    upper = diff < 0
    dec_f = jnp.where(lower[None], jnp.exp(jnp.where(lower, diff, 0.0)[None]
                                           * log_gf[:, None, None]), 0.0)
    dec_b = jnp.where(upper[None], jnp.exp(jnp.where(upper, -diff, 0.0)[None]
                                           * log_gb[:, None, None]), 0.0)
    dec = dec_f + dec_b
    zeta_f = jnp.exp((c - 1.0 - pos)[None, :] * log_gf[:, None])
    xi_f = jnp.exp((pos + 1.0)[None, :] * log_gf[:, None])
    zeta_b = jnp.exp(pos[None, :] * log_gb[:, None])
    xi_b = jnp.exp((c - pos)[None, :] * log_gb[:, None])
    wide = lambda t: jnp.broadcast_to(t[:, :, None], (heads, c, LANES))
    g_f = jnp.exp(c * log_gf)
    g_b = jnp.exp(c * log_gb)
    gc = jnp.zeros((heads, SUBLANES, dv), F32)
    gc = gc.at[:, 0, :].set(g_f[:, None]).at[:, 1, :].set(g_b[:, None])
    return (dec, wide(zeta_f), wide(zeta_b), wide(xi_f), wide(xi_b), gc)


def _rope_tables(seq, hd):
    half = hd // 2
    inv_freq = 1.0 / (ROPE_BASE ** jnp.linspace(0.0, 1.0, half, dtype=F32))
    ang = jnp.arange(seq, dtype=F32)[:, None] * inv_freq[None, :]
    return jnp.cos(ang), jnp.sin(ang)


def kernel(x, ln_mix, ln_ffn, ln_final, lru_w_in, lru_conv_w, lru_conv_b,
           lru_gate_a_w, lru_gate_a_b, lru_gate_x_w, lru_gate_x_b, lru_lambda,
           lru_w_out, ret_w_in, ret_w_out, ffn_w_gate, ffn_w_up, ffn_w_down):
    b, seq, d = x.shape
    depth = ln_mix.shape[0]
    rows = b * seq
    assert b == SUBLANES and seq % ROW_TILE == 0 and seq % LRU_T == 0
    row = lambda v: v.reshape(1, -1)

    ret_w_in, ret_w_out, ffn_w_gate, ffn_w_up, ffn_w_down = (
        w.astype(BF16) for w in (ret_w_in, ret_w_out, ffn_w_gate, ffn_w_up, ffn_w_down))

    h = x.reshape(rows, d)
    for layer in range(depth):
        j = layer // 2
        g_final = row(ln_final) if layer == depth - 1 else None
        proj = None
        if layer % 2 == 0:
            x3 = h.reshape(b, seq, d)
            xh, hf, gb = _lru_fwd(
                x3, row(ln_mix[layer]), lru_w_in, j, lru_conv_w[j],
                row(lru_conv_b[j]), lru_gate_a_w, lru_gate_x_w,
                row(lru_gate_a_b[j, 0]), row(lru_gate_x_b[j, 0]),
                row(lru_lambda[j, 0]))
            h = _lru_bwd(xh, hf, gb, x3, lru_gate_a_w, lru_gate_x_w,
                         row(lru_gate_a_b[j, 1]), row(lru_gate_x_b[j, 1]),
                         row(lru_lambda[j, 1]), lru_w_out, j).reshape(rows, d)
        else:
            heads = RET_HEADS
            vd = ret_w_out.shape[1]
            qk = (ret_w_in.shape[2] - 2 * vd) // 2
            cos, sin = _rope_tables(seq, qk // heads)
            q, k, v, sg = _ret_in(h, row(ln_mix[layer]), ret_w_in, j,
                                  cos, sin, seq, heads, qk, vd)
            tabs = _ret_tables(heads, qk // heads, vd // heads)
            yg = _ret_core(q.reshape(b, seq, qk), k.reshape(b, seq, qk),
                           v.reshape(b, seq, vd), sg.reshape(b, seq, vd), tabs, heads)
            proj = (yg.reshape(rows, vd), ret_w_out, j)
        h = _ffn(h, row(ln_ffn[layer]), ffn_w_gate, ffn_w_up, ffn_w_down, layer,
                 g_final, proj)
    return h.reshape(b, seq, d)
```

```python
import functools
import math

import jax
import jax.numpy as jnp
from jax import lax
from jax.experimental import pallas as pl
from jax.experimental.pallas import tpu as pltpu

F32 = jnp.float32
BF16 = jnp.bfloat16

NORM_EPS = 1e-6
RSQRT_FLOOR = 1e-30
RG_LRU_C = 8.0
LRU_BLOCKS = 4
CONV_WIDTH = 4
CONV_PAD_LEFT = 2
RET_HEADS = 4
ROPE_BASE = 10000.0

SUBLANES = 8
LANES = 128
VMEM_LIMIT_BYTES = 56 * 1024 * 1024
ROW_TILE = 1024
LRU_T = 64
GATE_ROWS = 256
PROJ_COLS = 512
OUT_COLS = 256
RET_CHUNK = 256
STATE_UNROLL = 3


def _cparams(*sem):
    return pltpu.CompilerParams(dimension_semantics=sem,
                                vmem_limit_bytes=VMEM_LIMIT_BYTES)


def _const_spec(shape):
    nd = len(shape)
    return pl.BlockSpec(shape, lambda *_: (0,) * nd, pipeline_mode=pl.Buffered(1))


def _layer_spec(shape, layer):
    nd = len(shape)
    return pl.BlockSpec((None,) + tuple(shape[1:]), lambda *_: (layer,) + (0,) * (nd - 1),
                        pipeline_mode=pl.Buffered(1))


def _gate_spec(shape, layer, direction):
    return pl.BlockSpec((None, None) + tuple(shape[2:]),
                        lambda *_: (layer, direction, 0, 0, 0),
                        pipeline_mode=pl.Buffered(1))


def _gate_scratch_shape(shape):
    blocks, k, n = shape[2:]
    return (blocks, k, 2 * n)


def _fill_gate_weights(wg_scr, ga_ref, gx_ref):
    n = ga_ref.shape[2]
    wg_scr[:, :, :n] = ga_ref[...].astype(BF16)
    wg_scr[:, :, n:] = gx_ref[...].astype(BF16)


def _rms(x, g):
    ms = jnp.mean(x * x, axis=-1, keepdims=True)
    return x * lax.rsqrt(ms + NORM_EPS) * g


def _sigmoid(x):
    return 0.5 * jnp.tanh(0.5 * x) + 0.5


def _gelu_tanh(x):
    c = math.sqrt(2.0 / math.pi)
    hx = 0.5 * x
    return hx * jnp.tanh(x * ((c * 0.044715) * (x * x) + c)) + hx


def _softplus(x):
    return jnp.maximum(x, 0.0) + jnp.log(1.0 + jnp.exp(-jnp.abs(x)))


def _lru_gate_pieces(xh_ref, wg_ref, ba_ref, bx_ref, lam_ref, a_dst, inp_dst):
    rows, width = xh_ref.shape
    bw = width // LRU_BLOCKS
    c2 = (-0.5 * RG_LRU_C * math.log2(math.e)) * _softplus(-lam_ref[...])
    ba2 = 0.5 * ba_ref[...]
    bx2 = 0.5 * bx_ref[...]

    def piece(r0, n):
        cs = slice(n * bw, (n + 1) * bw)
        xh = xh_ref[pl.ds(r0, GATE_ROWS), cs]
        z = jnp.dot(xh.astype(BF16), wg_ref[n], preferred_element_type=F32)
        tr = jnp.tanh(z[:, :bw] + ba2[:, cs])
        ti = jnp.tanh(z[:, bw:] + bx2[:, cs])
        a = jnp.exp2(c2[:, cs] * (tr + 1.0))
        om = 1.0 - a * a
        root = om * lax.rsqrt(jnp.maximum(om, RSQRT_FLOOR))
        a_dst[pl.ds(r0, GATE_ROWS), cs] = a
        inp_dst[pl.ds(r0, GATE_ROWS), cs] = ((ti + 1.0) * xh.astype(F32)) * root

    return [functools.partial(piece, r0, n)
            for r0 in range(0, rows, GATE_ROWS) for n in range(LRU_BLOCKS)]


def _lru_fwd_kernel(x_ref, g_ref, w_ref, cw_ref, cb_ref, ga_ref, gx_ref, ba_ref, bx_ref,
                    lam_ref, xh_ref, hf_ref, gb_ref, ext_scr, xh_scr,
                    a_scr, inp_scr, carry_scr, w_scr, wg_scr, hn_scr):
    i = pl.program_id(0)
    nt = pl.num_programs(0) - 2
    nb, t, _ = x_ref.shape
    rows, width = xh_ref.shape
    halo = (CONV_WIDTH - 1) * nb
    nslab = width // LANES

    @pl.when(i == 0)
    def _():
        carry_scr[...] = jnp.zeros_like(carry_scr)
        ext_scr[...] = jnp.zeros_like(ext_scr)
        w_scr[...] = w_ref[...].astype(BF16)
        _fill_gate_weights(wg_scr, ga_ref, gx_ref)

    cwh = 0.5 * cw_ref[...]
    cbh = 0.5 * cb_ref[...]
    for l in range(nslab):
        ls = slice(l * LANES, (l + 1) * LANES)
        acc = cbh[:, ls] + cwh[0:1, ls] * ext_scr[l, pl.ds(0, rows), :]
        for k in range(1, CONV_WIDTH):
            acc = acc + cwh[k:k + 1, ls] * ext_scr[l, pl.ds(k * nb, rows), :]
        xh_scr[:, ls] = acc
    xh_ref[...] = xh_scr[...].astype(xh_ref.dtype)

    ext_scr[:, pl.ds(0, halo), :] = ext_scr[:, pl.ds(rows, halo), :]
    gate_pieces = _lru_gate_pieces(xh_scr, wg_scr, ba_ref, bx_ref, lam_ref,
                                   a_scr, inp_scr)
    nbs = GATE_ROWS // t
    n_dots = (nb // nbs) * (2 * width // PROJ_COLS)
    per_dot = -(-len(gate_pieces) // n_dots)
    for b0 in range(0, nb, nbs):
        hn = hn_scr.at[b0 // nbs]
        for bb in range(nbs):
            hn[pl.ds(bb * t, t), :] = _rms(x_ref[b0 + bb], g_ref[...]).astype(BF16)
        for c0 in range(0, 2 * width, PROJ_COLS):
            u = jnp.dot(hn[...], w_scr[:, c0:c0 + PROJ_COLS], preferred_element_type=F32)
            for bb in range(nbs):
                ub = u[bb * t:(bb + 1) * t]
                if c0 < width:
                    for l in range(PROJ_COLS // LANES):
                        ext_scr[c0 // LANES + l, pl.ds(halo + b0 + bb, t, stride=nb), :] = (
                            ub[:, l * LANES:(l + 1) * LANES])
                else:
                    gb_ref[b0 + bb, :, c0 - width:c0 - width + PROJ_COLS] = (
                        ub.astype(gb_ref.dtype))
            for _ in range(per_dot):
                if gate_pieces:
                    gate_pieces.pop(0)()
    for piece in gate_pieces:
        piece()

    h = jnp.where(i > 1, carry_scr[...], 0.0)
    for s in range(rows // nb):
        h = a_scr[pl.ds(s * nb, nb), :] * h + inp_scr[pl.ds(s * nb, nb), :]
        if s == 0:
            h = jnp.where(i > 1, h, 0.0)
        inp_scr[pl.ds(s * nb, nb), :] = h
    carry_scr[...] = h
    hf_ref[...] = inp_scr[...].astype(hf_ref.dtype)

    @pl.when(i == nt)
    def _():
        ext_scr[:, pl.ds(halo, nb), :] = jnp.zeros((nslab, nb, LANES), F32)

    @pl.when(i == nt + 1)
    def _():
        xh_ref[pl.ds(2 * nb, rows - 2 * nb), :] = jnp.zeros(
            (rows - 2 * nb, width), xh_ref.dtype)
        xh_ref[pl.ds(0, 2 * nb), :] = jnp.where(
            lax.broadcasted_iota(jnp.int32, (2 * nb, width), 0) < nb,
            xh_ref[pl.ds(0, 2 * nb), :], jnp.zeros((), xh_ref.dtype))


def _lru_fwd(x, g, w_in, j, conv_w, conv_b, ga_w, gx_w, ba, bx, lam):
    nb, seq, d = x.shape
    width = w_in.shape[2] // 2
    t = LRU_T
    r = nb * t
    nt = seq // t
    halo = (CONV_WIDTH - 1) * nb
    assert CONV_WIDTH - 1 - CONV_PAD_LEFT == 1
    tile = lambda i: (0, jnp.minimum(i, nt - 1), 0)
    win = lambda i: (jnp.maximum(i - 1, 0), 0)
    return pl.pallas_call(
        _lru_fwd_kernel,
        grid=(nt + 2,),
        in_specs=[pl.BlockSpec((nb, t, d), tile), _const_spec((1, d)),
                  _layer_spec(w_in.shape, j), _const_spec(conv_w.shape),
                  _const_spec(conv_b.shape), _gate_spec(ga_w.shape, j, 0),
                  _gate_spec(gx_w.shape, j, 0),
                  _const_spec(ba.shape), _const_spec(bx.shape), _const_spec(lam.shape)],
        out_specs=[pl.BlockSpec((r, width), win), pl.BlockSpec((r, width), win),
                   pl.BlockSpec((nb, t, width), tile)],
        out_shape=[jax.ShapeDtypeStruct(((nt + 1) * r, width), BF16),
                   jax.ShapeDtypeStruct(((nt + 1) * r, width), BF16),
                   jax.ShapeDtypeStruct((nb, seq, width), BF16)],
        scratch_shapes=[pltpu.VMEM((width // LANES, halo + r, LANES), F32),
                        pltpu.VMEM((r, width), F32),
                        pltpu.VMEM((r, width), F32),
                        pltpu.VMEM((r, width), F32),
                        pltpu.VMEM((nb, width), F32),
                        pltpu.VMEM(w_in.shape[1:], BF16),
                        pltpu.VMEM(_gate_scratch_shape(ga_w.shape), BF16),
                        pltpu.VMEM((r // GATE_ROWS, GATE_ROWS, d), BF16)],
        compiler_params=_cparams("arbitrary"),
        name="lru_fwd",
    )(x, g, w_in, conv_w, conv_b, ga_w, gx_w, ba, bx, lam)


def _lru_bwd_kernel(xh_ref, hf_ref, gb_ref, x_ref, ga_ref, gx_ref, ba_ref, bx_ref,
                    lam_ref, wo_ref, o_ref, a_scr, inp_scr, hb_scr, carry_scr,
                    ysum_scr, ybm_scr, wo_scr, wg_scr):
    i = pl.program_id(0)
    nb, t, _ = x_ref.shape
    rows, width = xh_ref.shape
    nsteps = rows // nb
    nslab = width // LANES

    @pl.when(i == 0)
    def _():
        carry_scr[...] = jnp.zeros_like(carry_scr)
        hb_scr[...] = jnp.zeros_like(hb_scr)
        ysum_scr[:, pl.ds(rows, nb), :] = jnp.zeros((nslab, nb, LANES), F32)
        wo_scr[...] = wo_ref[...].astype(BF16)
        _fill_gate_weights(wg_scr, ga_ref, gx_ref)

    for l in range(nslab):
        ls = slice(l * LANES, (l + 1) * LANES)
        ysum_scr[l, pl.ds(0, rows), :] = hb_scr[:, ls] + hf_ref[:, ls].astype(F32)

    for b in range(nb):
        for l in range(nslab):
            ls = slice(l * LANES, (l + 1) * LANES)
            yb = ysum_scr[l, pl.ds(nb + b, t, stride=nb), :]
            ybm_scr[pl.ds(b * t, t), ls] = (
                yb * _gelu_tanh(gb_ref[b, :, ls].astype(F32))).astype(BF16)
    ysum_scr[:, pl.ds(rows, nb), :] = ysum_scr[:, pl.ds(0, nb), :]

    gate_pieces = _lru_gate_pieces(xh_ref, wg_scr, ba_ref, bx_ref, lam_ref,
                                   a_scr, inp_scr)
    gate_pieces = [p for r0 in range(rows - GATE_ROWS, -1, -GATE_ROWS)
                   for p in gate_pieces[(r0 // GATE_ROWS) * LRU_BLOCKS:
                                        (r0 // GATE_ROWS + 1) * LRU_BLOCKS]]
    d_out = wo_ref.shape[1]
    per_dot = -(-len(gate_pieces) * OUT_COLS // d_out)
    for c0 in range(0, d_out, OUT_COLS):
        cs = slice(c0, c0 + OUT_COLS)
        res = jnp.dot(ybm_scr[...], wo_scr[:, cs], preferred_element_type=F32)
        for b in range(nb):
            o_ref[b, :, cs] = x_ref[b, :, cs] + res[b * t:(b + 1) * t]
        for _ in range(per_dot):
            if gate_pieces:
                gate_pieces.pop(0)()
    for piece in gate_pieces:
        piece()

    h = carry_scr[...]
    for s in range(nsteps - 1, -1, -1):
        h = a_scr[pl.ds(s * nb, nb), :] * h + inp_scr[pl.ds(s * nb, nb), :]
        hb_scr[pl.ds(s * nb, nb), :] = h
    carry_scr[...] = h


def _lru_bwd(xh, hf, gb, x, ga_w, gx_w, ba, bx, lam, w_out, j):
    nb, seq, d = x.shape
    width = xh.shape[1]
    t = LRU_T
    r = nb * t
    nt = seq // t
    win_g = lambda i: (jnp.maximum(nt - i, 0), 0)
    win_s = lambda i: (jnp.minimum(nt - i + 1, nt), 0)
    tile = lambda i: (0, jnp.minimum(nt - i + 1, nt - 1), 0)
    return pl.pallas_call(
        _lru_bwd_kernel,
        grid=(nt + 2,),
        in_specs=[pl.BlockSpec((r, width), win_g), pl.BlockSpec((r, width), win_s),
                  pl.BlockSpec((nb, t, width), tile), pl.BlockSpec((nb, t, d), tile),
                  _gate_spec(ga_w.shape, j, 1), _gate_spec(gx_w.shape, j, 1),
                  _const_spec(ba.shape), _const_spec(bx.shape),
                  _const_spec(lam.shape), _layer_spec(w_out.shape, j)],
        out_specs=pl.BlockSpec((nb, t, d), tile),
        out_shape=jax.ShapeDtypeStruct((nb, seq, d), F32),
        scratch_shapes=[pltpu.VMEM((r, width), F32),
                        pltpu.VMEM((r, width), F32),
                        pltpu.VMEM((r, width), F32),
                        pltpu.VMEM((nb, width), F32),
                        pltpu.VMEM((width // LANES, r + nb, LANES), F32),
                        pltpu.VMEM((r, width), BF16),
                        pltpu.VMEM(w_out.shape[1:], BF16),
                        pltpu.VMEM(_gate_scratch_shape(ga_w.shape), BF16)],
        compiler_params=_cparams("arbitrary"),
        name="lru_bwd",
    )(xh, hf, gb, x, ga_w, gx_w, ba, bx, lam, w_out)


def _ffn_chunks(hidden):
    chunks, c = [], 0
    while c < hidden:
        size = 512 if hidden - c >= 512 else hidden - c
        chunks.append((c, size))
        c += size
    return chunks


def _ffn_kernel(*refs, final_norm, fuse_proj):
    refs = list(refs)
    h_ref = refs.pop(0)
    if fuse_proj:
        y_ref, wo_ref = refs.pop(0), refs.pop(0)
    g_ref, wg_ref, wu_ref, wd_ref = (refs.pop(0) for _ in range(4))
    gf_ref = refs.pop(0) if final_norm else None
    o_ref, hn_scr, act_scr = refs
    if fuse_proj:
        res_ref = o_ref
        res_ref[...] = h_ref[...] + jnp.dot(y_ref[...], wo_ref[...],
                                            preferred_element_type=F32)
    else:
        res_ref = h_ref
    hn_scr[...] = _rms(res_ref[...], g_ref[...]).astype(BF16)
    for c, size in _ffn_chunks(wg_ref.shape[1]):
        gt = jnp.dot(hn_scr[...], wg_ref[:, c:c + size], preferred_element_type=F32)
        up = jnp.dot(hn_scr[...], wu_ref[:, c:c + size], preferred_element_type=F32)
        act_scr[:, c:c + size] = (gt * _sigmoid(gt) * up).astype(BF16)
    y = res_ref[...] + jnp.dot(act_scr[...], wd_ref[...], preferred_element_type=F32)
    if final_norm:
        y = _rms(y, gf_ref[...])
    o_ref[...] = y


def _ffn(h, g, w_gate, w_up, w_down, layer, g_final=None, proj=None):
    rows, d = h.shape
    hidden = w_gate.shape[2]
    final_norm = g_final is not None
    fuse_proj = proj is not None
    r = ROW_TILE
    row = lambda i: (i, 0)
    in_specs = [pl.BlockSpec((r, d), row)]
    args = [h]
    scratch = [pltpu.VMEM((r, d), BF16), pltpu.VMEM((r, hidden), BF16)]
    if fuse_proj:
        y, w_out, j = proj
        in_specs += [pl.BlockSpec((r, y.shape[1]), row), _layer_spec(w_out.shape, j)]
        args += [y, w_out]
    in_specs += [_const_spec((1, d)), _layer_spec(w_gate.shape, layer),
                 _layer_spec(w_up.shape, layer), _layer_spec(w_down.shape, layer)]
    args += [g, w_gate, w_up, w_down]
    if final_norm:
        in_specs.append(_const_spec((1, d)))
        args.append(g_final)
    return pl.pallas_call(
        functools.partial(_ffn_kernel, final_norm=final_norm, fuse_proj=fuse_proj),
        grid=(rows // r,),
        in_specs=in_specs,
        out_specs=pl.BlockSpec((r, d), row),
        out_shape=jax.ShapeDtypeStruct((rows, d), F32),
        scratch_shapes=scratch,
        compiler_params=_cparams("parallel"),
        name="ffn" + ("_proj" if fuse_proj else "") + ("_final" if final_norm else ""),
    )(*args)


def _ret_in_kernel(h_ref, g_ref, w_ref, cos_ref, sin_ref, q_ref, k_ref, v_ref,
                   sg_ref, hn_scr, *, heads):
    qk = q_ref.shape[1]
    vd = v_ref.shape[1]
    hd = qk // heads
    half = hd // 2
    hn_scr[...] = _rms(h_ref[...], g_ref[...]).astype(BF16)
    cos = cos_ref[...]
    sin = sin_ref[...]
    k_scale = hd ** -0.5
    for h in range(heads):
        for which, dst, scale in ((0, q_ref, 1.0), (1, k_ref, k_scale)):
            c0 = which * qk + h * hd
            u = jnp.dot(hn_scr[...], w_ref[:, c0:c0 + hd], preferred_element_type=F32)
            x1, x2 = u[:, :half], u[:, half:]
            r1 = x1 * cos - x2 * sin
            r2 = x2 * cos + x1 * sin
            if scale != 1.0:
                r1, r2 = r1 * scale, r2 * scale
            dst[:, h * hd:h * hd + half] = r1.astype(dst.dtype)
            dst[:, h * hd + half:(h + 1) * hd] = r2.astype(dst.dtype)
    step = 512
    for c in range(0, vd, step):
        v_ref[:, c:c + step] = jnp.dot(
            hn_scr[...], w_ref[:, 2 * qk + c:2 * qk + c + step],
            preferred_element_type=F32).astype(v_ref.dtype)
    for c in range(0, vd, step):
        gt = jnp.dot(hn_scr[...], w_ref[:, 2 * qk + vd + c:2 * qk + vd + c + step],
                     preferred_element_type=F32)
        sg_ref[:, c:c + step] = (gt * _sigmoid(gt)).astype(sg_ref.dtype)


def _ret_in(h, g, w_in, j, cos, sin, seq, heads, qk, vd):
    rows, d = h.shape
    r = ROW_TILE
    half = cos.shape[1]
    per_seq = seq // r
    pos = lambda i: (i % per_seq, 0)
    row = lambda i: (i, 0)
    return pl.pallas_call(
        functools.partial(_ret_in_kernel, heads=heads),
        grid=(rows // r,),
        in_specs=[pl.BlockSpec((r, d), row), _const_spec((1, d)),
                  _layer_spec(w_in.shape, j),
                  pl.BlockSpec((r, half), pos), pl.BlockSpec((r, half), pos)],
        out_specs=[pl.BlockSpec((r, qk), row), pl.BlockSpec((r, qk), row),
                   pl.BlockSpec((r, vd), row), pl.BlockSpec((r, vd), row)],
        out_shape=[jax.ShapeDtypeStruct((rows, qk), BF16),
                   jax.ShapeDtypeStruct((rows, qk), BF16),
                   jax.ShapeDtypeStruct((rows, vd), BF16),
                   jax.ShapeDtypeStruct((rows, vd), BF16)],
        scratch_shapes=[pltpu.VMEM((r, d), BF16)],
        compiler_params=_cparams("parallel"),
        name="ret_in",
    )(h, g, w_in, cos, sin)


def _lane_tiled(t, n):
    return jnp.concatenate([t] * (n // LANES), axis=1)


def _ret_core_kernel(q_ref, k_ref, v_ref, sg_ref, dec_ref, zf_ref, zb_ref, xf_ref,
                     xb_ref, gc_ref, o_ref, sf_scr, sb_scr, stf_scr, stb_scr, y_scr):
    seq, dk = q_ref.shape
    dv = v_ref.shape[1]
    c = dec_ref.shape[0]
    nc = seq // c
    gcf = gc_ref[pl.ds(0, 1), :]
    gcb = gc_ref[pl.ds(1, 1), :]

    def kv_update(state, n, z_ref, gcd):
        r0 = pl.multiple_of(n * c, c)
        kz = (k_ref[pl.ds(r0, c), :].astype(F32) * _lane_tiled(z_ref[...], dk)).astype(BF16)
        upd = lax.dot_general(kz, v_ref[pl.ds(r0, c), :], (((0,), (0,)), ((), ())),
                              preferred_element_type=F32)
        return state * gcd + upd

    stf_scr[...] = jnp.zeros_like(stf_scr)
    stb_scr[...] = jnp.zeros_like(stb_scr)
    sf_scr[0] = jnp.zeros((dk, dv), BF16)
    sb_scr[nc - 1] = jnp.zeros((dk, dv), BF16)

    def state_body(s, carry):
        stf_scr[...] = kv_update(stf_scr[...], s, zf_ref, gcf)
        sf_scr[s + 1] = stf_scr[...].astype(BF16)
        stb_scr[...] = kv_update(stb_scr[...], nc - 1 - s, zb_ref, gcb)
        sb_scr[nc - 2 - s] = stb_scr[...].astype(BF16)
        return carry

    lax.fori_loop(0, nc - 1, state_body, 0, unroll=STATE_UNROLL)

    def finish(n):
        r0 = pl.multiple_of(n * c, c)
        y = y_scr[...]
        y = y * lax.rsqrt(jnp.mean(y * y, axis=-1, keepdims=True) + NORM_EPS)
        o_ref[pl.ds(r0, c), :] = (y * sg_ref[pl.ds(r0, c), :].astype(F32)).astype(o_ref.dtype)

    def out_body(n, carry):
        finish(jnp.maximum(n - 1, 0))
        r0 = pl.multiple_of(n * c, c)
        q = q_ref[pl.ds(r0, c), :]
        scores = lax.dot_general(q, k_ref[pl.ds(r0, c), :], (((1,), (1,)), ((), ())),
                                 preferred_element_type=F32)
        qf32 = q.astype(F32)
        qf = (qf32 * _lane_tiled(xf_ref[...], dk)).astype(BF16)
        qb = (qf32 * _lane_tiled(xb_ref[...], dk)).astype(BF16)
        p = (scores * dec_ref[...]).astype(BF16)
        lhs = jnp.concatenate([qf, qb, p], axis=1)
        rhs = jnp.concatenate([sf_scr[n], sb_scr[n], v_ref[pl.ds(r0, c), :]], axis=0)
        y_scr[...] = jnp.dot(lhs, rhs, preferred_element_type=F32)
        return carry

    y_scr[...] = jnp.zeros_like(y_scr)
    lax.fori_loop(0, nc, out_body, 0, unroll=4)
    finish(nc - 1)


def _ret_core(q, k, v, sg, tabs, heads):
    b, seq, qk = q.shape
    vd = v.shape[2]
    dk, dv = qk // heads, vd // heads
    c = RET_CHUNK
    dec, zf, zb, xf, xb, gc = tabs
    bh = lambda i, j: (i, 0, j)
    hh = lambda i, j: (j, 0, 0)
    return pl.pallas_call(
        _ret_core_kernel,
        grid=(b, heads),
        in_specs=[pl.BlockSpec((None, seq, dk), bh), pl.BlockSpec((None, seq, dk), bh),
                  pl.BlockSpec((None, seq, dv), bh), pl.BlockSpec((None, seq, dv), bh),
                  pl.BlockSpec((None, c, c), hh),
                  pl.BlockSpec((None, c, LANES), hh), pl.BlockSpec((None, c, LANES), hh),
                  pl.BlockSpec((None, c, LANES), hh), pl.BlockSpec((None, c, LANES), hh),
                  pl.BlockSpec((None, SUBLANES, dv), hh)],
        out_specs=pl.BlockSpec((None, seq, dv), bh),
        out_shape=jax.ShapeDtypeStruct((b, seq, vd), BF16),
        scratch_shapes=[pltpu.VMEM((seq // c, dk, dv), BF16),
                        pltpu.VMEM((seq // c, dk, dv), BF16),
                        pltpu.VMEM((dk, dv), F32),
                        pltpu.VMEM((dk, dv), F32),
                        pltpu.VMEM((c, dv), F32)],
        compiler_params=_cparams("parallel", "parallel"),
        name="ret_core",
    )(q, k, v, sg, dec, zf, zb, xf, xb, gc)


def _ret_tables(heads, dk, dv):
    c = RET_CHUNK
    log_gf = jnp.log1p(-jnp.exp2(-5.0 - jnp.arange(heads, dtype=F32)))
    log_gb = log_gf[::-1]
    pos = jnp.arange(c, dtype=F32)
    diff = pos[:, None] - pos[None, :]
    lower = diff >= 0
    upper = diff < 0
    dec_f = jnp.where(lower[None], jnp.exp(jnp.where(lower, diff, 0.0)[None]
                                           * log_gf[:, None, None]), 0.0)
    dec_b = jnp.where(upper[None], jnp.exp(jnp.where(upper, -diff, 0.0)[None]
                                           * log_gb[:, None, None]), 0.0)
    dec = dec_f + dec_b
    zeta_f = jnp.exp((c - 1.0 - pos)[None, :] * log_gf[:, None])
    xi_f = jnp.exp((pos + 1.0)[None, :] * log_gf[:, None])
    zeta_b = jnp.exp(pos[None, :] * log_gb[:, None])
    xi_b = jnp.exp((c - pos)[None, :] * log_gb[:, None])
    wide = lambda t: jnp.broadcast_to(t[:, :, None], (heads, c, LANES))
    g_f = jnp.exp(c * log_gf)
    g_b = jnp.exp(c * log_gb)
    gc = jnp.zeros((heads, SUBLANES, dv), F32)
    gc = gc.at[:, 0, :].set(g_f[:, None]).at[:, 1, :].set(g_b[:, None])
    return (dec, wide(zeta_f), wide(zeta_b), wide(xi_f), wide(xi_b), gc)


def _rope_tables(seq, hd):
    half = hd // 2
    inv_freq = 1.0 / (ROPE_BASE ** jnp.linspace(0.0, 1.0, half, dtype=F32))
    ang = jnp.arange(seq, dtype=F32)[:, None] * inv_freq[None, :]
    return jnp.cos(ang), jnp.sin(ang)


def kernel(x, ln_mix, ln_ffn, ln_final, lru_w_in, lru_conv_w, lru_conv_b,
           lru_gate_a_w, lru_gate_a_b, lru_gate_x_w, lru_gate_x_b, lru_lambda,
           lru_w_out, ret_w_in, ret_w_out, ffn_w_gate, ffn_w_up, ffn_w_down):
    b, seq, d = x.shape
    depth = ln_mix.shape[0]
    rows = b * seq
    assert b == SUBLANES and seq % ROW_TILE == 0 and seq % LRU_T == 0
    row = lambda v: v.reshape(1, -1)

    ret_w_in, ret_w_out, ffn_w_gate, ffn_w_up, ffn_w_down = (
        w.astype(BF16) for w in (ret_w_in, ret_w_out, ffn_w_gate, ffn_w_up, ffn_w_down))

    h = x.reshape(rows, d)
    for layer in range(depth):
        j = layer // 2
        g_final = row(ln_final) if layer == depth - 1 else None
        proj = None
        if layer % 2 == 0:
            x3 = h.reshape(b, seq, d)
            xh, hf, gb = _lru_fwd(
                x3, row(ln_mix[layer]), lru_w_in, j, lru_conv_w[j],
                row(lru_conv_b[j]), lru_gate_a_w, lru_gate_x_w,
                row(lru_gate_a_b[j, 0]), row(lru_gate_x_b[j, 0]),
                row(lru_lambda[j, 0]))
            h = _lru_bwd(xh, hf, gb, x3, lru_gate_a_w, lru_gate_x_w,
                         row(lru_gate_a_b[j, 1]), row(lru_gate_x_b[j, 1]),
                         row(lru_lambda[j, 1]), lru_w_out, j).reshape(rows, d)
        else:
            heads = RET_HEADS
            vd = ret_w_out.shape[1]
            qk = (ret_w_in.shape[2] - 2 * vd) // 2
            cos, sin = _rope_tables(seq, qk // heads)
            q, k, v, sg = _ret_in(h, row(ln_mix[layer]), ret_w_in, j,
                                  cos, sin, seq, heads, qk, vd)
            tabs = _ret_tables(heads, qk // heads, vd // heads)
            yg = _ret_core(q.reshape(b, seq, qk), k.reshape(b, seq, qk),
                           v.reshape(b, seq, vd), sg.reshape(b, seq, vd), tabs, heads)
            proj = (yg.reshape(rows, vd), ret_w_out, j)
        h = _ffn(h, row(ln_ffn[layer]), ffn_w_gate, ffn_w_up, ffn_w_down, layer,
                 g_final, proj)
    return h.reshape(b, seq, d)
```

```python
import functools
import math

import jax
import jax.numpy as jnp
from jax import lax
from jax.experimental import pallas as pl
from jax.experimental.pallas import tpu as pltpu

F32 = jnp.float32
BF16 = jnp.bfloat16

NORM_EPS = 1e-6
RSQRT_FLOOR = 1e-30
RG_LRU_C = 8.0
LRU_BLOCKS = 4
CONV_WIDTH = 4
CONV_PAD_LEFT = 2
RET_HEADS = 4
ROPE_BASE = 10000.0

SUBLANES = 8
LANES = 128
VMEM_LIMIT_BYTES = 56 * 1024 * 1024
ROW_TILE = 1024
LRU_T = 128
GATE_ROWS = 256
PROJ_ROWS = 1024
PROJ_COLS = 256
OUT_COLS = 256
RET_CHUNK = 256
STATE_UNROLL = 3


def _cparams(*sem):
    return pltpu.CompilerParams(dimension_semantics=sem,
                                vmem_limit_bytes=VMEM_LIMIT_BYTES)


def _const_spec(shape):
    nd = len(shape)
    return pl.BlockSpec(shape, lambda *_: (0,) * nd, pipeline_mode=pl.Buffered(1))


def _layer_spec(shape, layer):
    nd = len(shape)
    return pl.BlockSpec((None,) + tuple(shape[1:]), lambda *_: (layer,) + (0,) * (nd - 1),
                        pipeline_mode=pl.Buffered(1))


def _gate_spec(shape, layer, direction):
    return pl.BlockSpec((None, None) + tuple(shape[2:]),
                        lambda *_: (layer, direction, 0, 0, 0),
                        pipeline_mode=pl.Buffered(1))


def _gate_scratch_shape(shape):
    blocks, k, n = shape[2:]
    return (blocks, k, 2 * n)


def _fill_gate_weights(wg_scr, ga_ref, gx_ref):
    n = ga_ref.shape[2]
    wg_scr[:, :, :n] = ga_ref[...].astype(BF16)
    wg_scr[:, :, n:] = gx_ref[...].astype(BF16)


def _rms(x, g):
    ms = jnp.mean(x * x, axis=-1, keepdims=True)
    return x * lax.rsqrt(ms + NORM_EPS) * g


def _sigmoid(x):
    return 0.5 * jnp.tanh(0.5 * x) + 0.5


def _gelu_tanh(x):
    c = math.sqrt(2.0 / math.pi)
    hx = 0.5 * x
    return hx * jnp.tanh(x * ((c * 0.044715) * (x * x) + c)) + hx


def _softplus(x):
    return jnp.maximum(x, 0.0) + jnp.log(1.0 + jnp.exp(-jnp.abs(x)))


def _lru_gate_pieces(xh_ref, wg_ref, ba_ref, bx_ref, lam_ref, a_dst, inp_dst):
    rows, width = xh_ref.shape
    bw = width // LRU_BLOCKS
    c2 = (-0.5 * RG_LRU_C * math.log2(math.e)) * _softplus(-lam_ref[...])
    ba2 = 0.5 * ba_ref[...]
    bx2 = 0.5 * bx_ref[...]

    def piece(r0, n):
        cs = slice(n * bw, (n + 1) * bw)
        xh = xh_ref[pl.ds(r0, GATE_ROWS), cs]
        z = jnp.dot(xh.astype(BF16), wg_ref[n], preferred_element_type=F32)
        tr = jnp.tanh(z[:, :bw] + ba2[:, cs])
        ti = jnp.tanh(z[:, bw:] + bx2[:, cs])
        a = jnp.exp2(c2[:, cs] * (tr + 1.0))
        om = 1.0 - a * a
        root = om * lax.rsqrt(jnp.maximum(om, RSQRT_FLOOR))
        a_dst[pl.ds(r0, GATE_ROWS), cs] = a
        inp_dst[pl.ds(r0, GATE_ROWS), cs] = ((ti + 1.0) * xh.astype(F32)) * root

    return [functools.partial(piece, r0, n)
            for r0 in range(0, rows, GATE_ROWS) for n in range(LRU_BLOCKS)]


def _lru_fwd_kernel(x_ref, g_ref, w_ref, cw_ref, cb_ref, ga_ref, gx_ref, ba_ref, bx_ref,
                    lam_ref, xh_ref, hf_ref, gb_ref, ext_scr, xh_scr,
                    a_scr, inp_scr, carry_scr, w_scr, wg_scr, hn_scr):
    i = pl.program_id(0)
    nt = pl.num_programs(0) - 2
    nb, t, _ = x_ref.shape
    rows, width = xh_ref.shape
    halo = (CONV_WIDTH - 1) * nb
    nslab = width // LANES

    @pl.when(i == 0)
    def _():
        carry_scr[...] = jnp.zeros_like(carry_scr)
        ext_scr[...] = jnp.zeros_like(ext_scr)
        w_scr[...] = w_ref[...].astype(BF16)
        _fill_gate_weights(wg_scr, ga_ref, gx_ref)

    cwh = 0.5 * cw_ref[...]
    cbh = 0.5 * cb_ref[...]
    for l in range(nslab):
        ls = slice(l * LANES, (l + 1) * LANES)
        acc = cbh[:, ls] + cwh[0:1, ls] * ext_scr[l, pl.ds(0, rows), :]
        for k in range(1, CONV_WIDTH):
            acc = acc + cwh[k:k + 1, ls] * ext_scr[l, pl.ds(k * nb, rows), :]
        xh_scr[:, ls] = acc
    xh_ref[...] = xh_scr[...].astype(xh_ref.dtype)

    ext_scr[:, pl.ds(0, halo), :] = ext_scr[:, pl.ds(rows, halo), :]
    gate_pieces = _lru_gate_pieces(xh_scr, wg_scr, ba_ref, bx_ref, lam_ref,
                                   a_scr, inp_scr)
    nbs = PROJ_ROWS // t
    n_dots = (nb // nbs) * (2 * width // PROJ_COLS)
    per_dot = -(-len(gate_pieces) // n_dots)
    for b0 in range(0, nb, nbs):
        hn = hn_scr.at[b0 // nbs]
        for bb in range(nbs):
            hn[pl.ds(bb * t, t), :] = _rms(x_ref[b0 + bb], g_ref[...]).astype(BF16)
        for c0 in range(0, 2 * width, PROJ_COLS):
            u = jnp.dot(hn[...], w_scr[:, c0:c0 + PROJ_COLS], preferred_element_type=F32)
            for bb in range(nbs):
                ub = u[bb * t:(bb + 1) * t]
                if c0 < width:
                    for l in range(PROJ_COLS // LANES):
                        ext_scr[c0 // LANES + l, pl.ds(halo + b0 + bb, t, stride=nb), :] = (
                            ub[:, l * LANES:(l + 1) * LANES])
                else:
                    gb_ref[b0 + bb, :, c0 - width:c0 - width + PROJ_COLS] = (
                        ub.astype(gb_ref.dtype))
            for _ in range(per_dot):
                if gate_pieces:
                    gate_pieces.pop(0)()
    for piece in gate_pieces:
        piece()

    h = jnp.where(i > 1, carry_scr[...], 0.0)
    for s in range(rows // nb):
        h = a_scr[pl.ds(s * nb, nb), :] * h + inp_scr[pl.ds(s * nb, nb), :]
        if s == 0:
            h = jnp.where(i > 1, h, 0.0)
        inp_scr[pl.ds(s * nb, nb), :] = h
    carry_scr[...] = h
    hf_ref[...] = inp_scr[...].astype(hf_ref.dtype)

    @pl.when(i == nt)
    def _():
        ext_scr[:, pl.ds(halo, nb), :] = jnp.zeros((nslab, nb, LANES), F32)

    @pl.when(i == nt + 1)
    def _():
        xh_ref[pl.ds(2 * nb, rows - 2 * nb), :] = jnp.zeros(
            (rows - 2 * nb, width), xh_ref.dtype)
        xh_ref[pl.ds(0, 2 * nb), :] = jnp.where(
            lax.broadcasted_iota(jnp.int32, (2 * nb, width), 0) < nb,
            xh_ref[pl.ds(0, 2 * nb), :], jnp.zeros((), xh_ref.dtype))


def _lru_fwd(x, g, w_in, j, conv_w, conv_b, ga_w, gx_w, ba, bx, lam):
    nb, seq, d = x.shape
    width = w_in.shape[2] // 2
    t = LRU_T
    r = nb * t
    nt = seq // t
    halo = (CONV_WIDTH - 1) * nb
    assert CONV_WIDTH - 1 - CONV_PAD_LEFT == 1
    tile = lambda i: (0, jnp.minimum(i, nt - 1), 0)
    win = lambda i: (jnp.maximum(i - 1, 0), 0)
    return pl.pallas_call(
        _lru_fwd_kernel,
        grid=(nt + 2,),
        in_specs=[pl.BlockSpec((nb, t, d), tile), _const_spec((1, d)),
                  _layer_spec(w_in.shape, j), _const_spec(conv_w.shape),
                  _const_spec(conv_b.shape), _gate_spec(ga_w.shape, j, 0),
                  _gate_spec(gx_w.shape, j, 0),
                  _const_spec(ba.shape), _const_spec(bx.shape), _const_spec(lam.shape)],
        out_specs=[pl.BlockSpec((r, width), win), pl.BlockSpec((r, width), win),
                   pl.BlockSpec((nb, t, width), tile)],
        out_shape=[jax.ShapeDtypeStruct(((nt + 1) * r, width), BF16),
                   jax.ShapeDtypeStruct(((nt + 1) * r, width), BF16),
                   jax.ShapeDtypeStruct((nb, seq, width), BF16)],
        scratch_shapes=[pltpu.VMEM((width // LANES, halo + r, LANES), F32),
                        pltpu.VMEM((r, width), F32),
                        pltpu.VMEM((r, width), F32),
                        pltpu.VMEM((r, width), F32),
                        pltpu.VMEM((nb, width), F32),
                        pltpu.VMEM(w_in.shape[1:], BF16),
                        pltpu.VMEM(_gate_scratch_shape(ga_w.shape), BF16),
                        pltpu.VMEM((r // PROJ_ROWS, PROJ_ROWS, d), BF16)],
        compiler_params=_cparams("arbitrary"),
        name="lru_fwd",
    )(x, g, w_in, conv_w, conv_b, ga_w, gx_w, ba, bx, lam)


def _lru_bwd_kernel(xh_ref, hf_ref, gb_ref, x_ref, ga_ref, gx_ref, ba_ref, bx_ref,
                    lam_ref, wo_ref, o_ref, a_scr, inp_scr, hb_scr, carry_scr,
                    ysum_scr, ybm_scr, wo_scr, wg_scr):
    i = pl.program_id(0)
    nb, t, _ = x_ref.shape
    rows, width = xh_ref.shape
    nsteps = rows // nb
    nslab = width // LANES

    @pl.when(i == 0)
    def _():
        carry_scr[...] = jnp.zeros_like(carry_scr)
        hb_scr[...] = jnp.zeros_like(hb_scr)
        ysum_scr[:, pl.ds(rows, nb), :] = jnp.zeros((nslab, nb, LANES), F32)
        wo_scr[...] = wo_ref[...].astype(BF16)
        _fill_gate_weights(wg_scr, ga_ref, gx_ref)

    for l in range(nslab):
        ls = slice(l * LANES, (l + 1) * LANES)
        ysum_scr[l, pl.ds(0, rows), :] = hb_scr[:, ls] + hf_ref[:, ls].astype(F32)

    for b in range(nb):
        for l in range(nslab):
            ls = slice(l * LANES, (l + 1) * LANES)
            yb = ysum_scr[l, pl.ds(nb + b, t, stride=nb), :]
            ybm_scr[pl.ds(b * t, t), ls] = (
                yb * _gelu_tanh(gb_ref[b, :, ls].astype(F32))).astype(BF16)
    ysum_scr[:, pl.ds(rows, nb), :] = ysum_scr[:, pl.ds(0, nb), :]

    gate_pieces = _lru_gate_pieces(xh_ref, wg_scr, ba_ref, bx_ref, lam_ref,
                                   a_scr, inp_scr)
    gate_pieces = [p for r0 in range(rows - GATE_ROWS, -1, -GATE_ROWS)
                   for p in gate_pieces[(r0 // GATE_ROWS) * LRU_BLOCKS:
                                        (r0 // GATE_ROWS + 1) * LRU_BLOCKS]]
    d_out = wo_ref.shape[1]
    per_dot = -(-len(gate_pieces) * OUT_COLS // d_out)
    for c0 in range(0, d_out, OUT_COLS):
        cs = slice(c0, c0 + OUT_COLS)
        res = jnp.dot(ybm_scr[...], wo_scr[:, cs], preferred_element_type=F32)
        for b in range(nb):
            o_ref[b, :, cs] = x_ref[b, :, cs] + res[b * t:(b + 1) * t]
        for _ in range(per_dot):
            if gate_pieces:
                gate_pieces.pop(0)()
    for piece in gate_pieces:
        piece()

    h = carry_scr[...]
    for s in range(nsteps - 1, -1, -1):
        h = a_scr[pl.ds(s * nb, nb), :] * h + inp_scr[pl.ds(s * nb, nb), :]
        hb_scr[pl.ds(s * nb, nb), :] = h
    carry_scr[...] = h


def _lru_bwd(xh, hf, gb, x, ga_w, gx_w, ba, bx, lam, w_out, j):
    nb, seq, d = x.shape
    width = xh.shape[1]
    t = LRU_T
    r = nb * t
    nt = seq // t
    win_g = lambda i: (jnp.maximum(nt - i, 0), 0)
    win_s = lambda i: (jnp.minimum(nt - i + 1, nt), 0)
    tile = lambda i: (0, jnp.minimum(nt - i + 1, nt - 1), 0)
    return pl.pallas_call(
        _lru_bwd_kernel,
        grid=(nt + 2,),
        in_specs=[pl.BlockSpec((r, width), win_g), pl.BlockSpec((r, width), win_s),
                  pl.BlockSpec((nb, t, width), tile), pl.BlockSpec((nb, t, d), tile),
                  _gate_spec(ga_w.shape, j, 1), _gate_spec(gx_w.shape, j, 1),
                  _const_spec(ba.shape), _const_spec(bx.shape),
                  _const_spec(lam.shape), _layer_spec(w_out.shape, j)],
        out_specs=pl.BlockSpec((nb, t, d), tile),
        out_shape=jax.ShapeDtypeStruct((nb, seq, d), F32),
        scratch_shapes=[pltpu.VMEM((r, width), F32),
                        pltpu.VMEM((r, width), F32),
                        pltpu.VMEM((r, width), F32),
                        pltpu.VMEM((nb, width), F32),
                        pltpu.VMEM((width // LANES, r + nb, LANES), F32),
                        pltpu.VMEM((r, width), BF16),
                        pltpu.VMEM(w_out.shape[1:], BF16),
                        pltpu.VMEM(_gate_scratch_shape(ga_w.shape), BF16)],
        compiler_params=_cparams("arbitrary"),
        name="lru_bwd",
    )(xh, hf, gb, x, ga_w, gx_w, ba, bx, lam, w_out)


def _ffn_chunks(hidden):
    chunks, c = [], 0
    while c < hidden:
        size = 512 if hidden - c >= 512 else hidden - c
        chunks.append((c, size))
        c += size
    return chunks


def _ffn_kernel(*refs, final_norm, fuse_proj):
    refs = list(refs)
    h_ref = refs.pop(0)
    if fuse_proj:
        y_ref, wo_ref = refs.pop(0), refs.pop(0)
    g_ref, wg_ref, wu_ref, wd_ref = (refs.pop(0) for _ in range(4))
    gf_ref = refs.pop(0) if final_norm else None
    o_ref, hn_scr, act_scr = refs
    if fuse_proj:
        res_ref = o_ref
        res_ref[...] = h_ref[...] + jnp.dot(y_ref[...], wo_ref[...],
                                            preferred_element_type=F32)
    else:
        res_ref = h_ref
    hn_scr[...] = _rms(res_ref[...], g_ref[...]).astype(BF16)
    for c, size in _ffn_chunks(wg_ref.shape[1]):
        gt = jnp.dot(hn_scr[...], wg_ref[:, c:c + size], preferred_element_type=F32)
        up = jnp.dot(hn_scr[...], wu_ref[:, c:c + size], preferred_element_type=F32)
        act_scr[:, c:c + size] = (gt * _sigmoid(gt) * up).astype(BF16)
    y = res_ref[...] + jnp.dot(act_scr[...], wd_ref[...], preferred_element_type=F32)
    if final_norm:
        y = _rms(y, gf_ref[...])
    o_ref[...] = y


def _ffn(h, g, w_gate, w_up, w_down, layer, g_final=None, proj=None):
    rows, d = h.shape
    hidden = w_gate.shape[2]
    final_norm = g_final is not None
    fuse_proj = proj is not None
    r = ROW_TILE
    row = lambda i: (i, 0)
    in_specs = [pl.BlockSpec((r, d), row)]
    args = [h]
    scratch = [pltpu.VMEM((r, d), BF16), pltpu.VMEM((r, hidden), BF16)]
    if fuse_proj:
        y, w_out, j = proj
        in_specs += [pl.BlockSpec((r, y.shape[1]), row), _layer_spec(w_out.shape, j)]
        args += [y, w_out]
    in_specs += [_const_spec((1, d)), _layer_spec(w_gate.shape, layer),
                 _layer_spec(w_up.shape, layer), _layer_spec(w_down.shape, layer)]
    args += [g, w_gate, w_up, w_down]
    if final_norm:
        in_specs.append(_const_spec((1, d)))
        args.append(g_final)
    return pl.pallas_call(
        functools.partial(_ffn_kernel, final_norm=final_norm, fuse_proj=fuse_proj),
        grid=(rows // r,),
        in_specs=in_specs,
        out_specs=pl.BlockSpec((r, d), row),
        out_shape=jax.ShapeDtypeStruct((rows, d), F32),
        scratch_shapes=scratch,
        compiler_params=_cparams("parallel"),
        name="ffn" + ("_proj" if fuse_proj else "") + ("_final" if final_norm else ""),
    )(*args)


def _ret_in_kernel(h_ref, g_ref, w_ref, cos_ref, sin_ref, q_ref, k_ref, v_ref,
                   sg_ref, hn_scr, *, heads):
    qk = q_ref.shape[1]
    vd = v_ref.shape[1]
    hd = qk // heads
    half = hd // 2
    hn_scr[...] = _rms(h_ref[...], g_ref[...]).astype(BF16)
    cos = cos_ref[...]
    sin = sin_ref[...]
    k_scale = hd ** -0.5
    for h in range(heads):
        for which, dst, scale in ((0, q_ref, 1.0), (1, k_ref, k_scale)):
            c0 = which * qk + h * hd
            u = jnp.dot(hn_scr[...], w_ref[:, c0:c0 + hd], preferred_element_type=F32)
            x1, x2 = u[:, :half], u[:, half:]
            r1 = x1 * cos - x2 * sin
            r2 = x2 * cos + x1 * sin
            if scale != 1.0:
                r1, r2 = r1 * scale, r2 * scale
            dst[:, h * hd:h * hd + half] = r1.astype(dst.dtype)
            dst[:, h * hd + half:(h + 1) * hd] = r2.astype(dst.dtype)
    step = 512
    for c in range(0, vd, step):
        v_ref[:, c:c + step] = jnp.dot(
            hn_scr[...], w_ref[:, 2 * qk + c:2 * qk + c + step],
            preferred_element_type=F32).astype(v_ref.dtype)
    for c in range(0, vd, step):
        gt = jnp.dot(hn_scr[...], w_ref[:, 2 * qk + vd + c:2 * qk + vd + c + step],
                     preferred_element_type=F32)
        sg_ref[:, c:c + step] = (gt * _sigmoid(gt)).astype(sg_ref.dtype)


def _ret_in(h, g, w_in, j, cos, sin, seq, heads, qk, vd):
    rows, d = h.shape
    r = ROW_TILE
    half = cos.shape[1]
    per_seq = seq // r
    pos = lambda i: (i % per_seq, 0)
    row = lambda i: (i, 0)
    return pl.pallas_call(
        functools.partial(_ret_in_kernel, heads=heads),
        grid=(rows // r,),
        in_specs=[pl.BlockSpec((r, d), row), _const_spec((1, d)),
                  _layer_spec(w_in.shape, j),
                  pl.BlockSpec((r, half), pos), pl.BlockSpec((r, half), pos)],
        out_specs=[pl.BlockSpec((r, qk), row), pl.BlockSpec((r, qk), row),
                   pl.BlockSpec((r, vd), row), pl.BlockSpec((r, vd), row)],
        out_shape=[jax.ShapeDtypeStruct((rows, qk), BF16),
                   jax.ShapeDtypeStruct((rows, qk), BF16),
                   jax.ShapeDtypeStruct((rows, vd), BF16),
                   jax.ShapeDtypeStruct((rows, vd), BF16)],
        scratch_shapes=[pltpu.VMEM((r, d), BF16)],
        compiler_params=_cparams("parallel"),
        name="ret_in",
    )(h, g, w_in, cos, sin)


def _lane_tiled(t, n):
    return jnp.concatenate([t] * (n // LANES), axis=1)


def _ret_core_kernel(q_ref, k_ref, v_ref, sg_ref, dec_ref, zf_ref, zb_ref, xf_ref,
                     xb_ref, gc_ref, o_ref, sf_scr, sb_scr, stf_scr, stb_scr, y_scr):
    seq, dk = q_ref.shape
    dv = v_ref.shape[1]
    c = dec_ref.shape[0]
    nc = seq // c
    gcf = gc_ref[pl.ds(0, 1), :]
    gcb = gc_ref[pl.ds(1, 1), :]

    def kv_update(state, n, z_ref, gcd):
        r0 = pl.multiple_of(n * c, c)
        kz = (k_ref[pl.ds(r0, c), :].astype(F32) * _lane_tiled(z_ref[...], dk)).astype(BF16)
        upd = lax.dot_general(kz, v_ref[pl.ds(r0, c), :], (((0,), (0,)), ((), ())),
                              preferred_element_type=F32)
        return state * gcd + upd

    stf_scr[...] = jnp.zeros_like(stf_scr)
    stb_scr[...] = jnp.zeros_like(stb_scr)
    sf_scr[0] = jnp.zeros((dk, dv), BF16)
    sb_scr[nc - 1] = jnp.zeros((dk, dv), BF16)

    def state_body(s, carry):
        stf_scr[...] = kv_update(stf_scr[...], s, zf_ref, gcf)
        sf_scr[s + 1] = stf_scr[...].astype(BF16)
        stb_scr[...] = kv_update(stb_scr[...], nc - 1 - s, zb_ref, gcb)
        sb_scr[nc - 2 - s] = stb_scr[...].astype(BF16)
        return carry

    lax.fori_loop(0, nc - 1, state_body, 0, unroll=STATE_UNROLL)

    def finish(n):
        r0 = pl.multiple_of(n * c, c)
        y = y_scr[...]
        y = y * lax.rsqrt(jnp.mean(y * y, axis=-1, keepdims=True) + NORM_EPS)
        o_ref[pl.ds(r0, c), :] = (y * sg_ref[pl.ds(r0, c), :].astype(F32)).astype(o_ref.dtype)

    def out_body(n, carry):
        finish(jnp.maximum(n - 1, 0))
        r0 = pl.multiple_of(n * c, c)
        q = q_ref[pl.ds(r0, c), :]
        scores = lax.dot_general(q, k_ref[pl.ds(r0, c), :], (((1,), (1,)), ((), ())),
                                 preferred_element_type=F32)
        qf32 = q.astype(F32)
        qf = (qf32 * _lane_tiled(xf_ref[...], dk)).astype(BF16)
        qb = (qf32 * _lane_tiled(xb_ref[...], dk)).astype(BF16)
        p = (scores * dec_ref[...]).astype(BF16)
        lhs = jnp.concatenate([qf, qb, p], axis=1)
        rhs = jnp.concatenate([sf_scr[n], sb_scr[n], v_ref[pl.ds(r0, c), :]], axis=0)
        y_scr[...] = jnp.dot(lhs, rhs, preferred_element_type=F32)
        return carry

    y_scr[...] = jnp.zeros_like(y_scr)
    lax.fori_loop(0, nc, out_body, 0, unroll=4)
    finish(nc - 1)


def _ret_core(q, k, v, sg, tabs, heads):
    b, seq, qk = q.shape
    vd = v.shape[2]
    dk, dv = qk // heads, vd // heads
    c = RET_CHUNK
    dec, zf, zb, xf, xb, gc = tabs
    bh = lambda i, j: (i, 0, j)
    hh = lambda i, j: (j, 0, 0)
    return pl.pallas_call(
        _ret_core_kernel,
        grid=(b, heads),
        in_specs=[pl.BlockSpec((None, seq, dk), bh), pl.BlockSpec((None, seq, dk), bh),
                  pl.BlockSpec((None, seq, dv), bh), pl.BlockSpec((None, seq, dv), bh),
                  pl.BlockSpec((None, c, c), hh),
                  pl.BlockSpec((None, c, LANES), hh), pl.BlockSpec((None, c, LANES), hh),
                  pl.BlockSpec((None, c, LANES), hh), pl.BlockSpec((None, c, LANES), hh),
                  pl.BlockSpec((None, SUBLANES, dv), hh)],
        out_specs=pl.BlockSpec((None, seq, dv), bh),
        out_shape=jax.ShapeDtypeStruct((b, seq, vd), BF16),
        scratch_shapes=[pltpu.VMEM((seq // c, dk, dv), BF16),
                        pltpu.VMEM((seq // c, dk, dv), BF16),
                        pltpu.VMEM((dk, dv), F32),
                        pltpu.VMEM((dk, dv), F32),
                        pltpu.VMEM((c, dv), F32)],
        compiler_params=_cparams("parallel", "parallel"),
        name="ret_core",
    )(q, k, v, sg, dec, zf, zb, xf, xb, gc)


def _ret_tables(heads, dk, dv):
    c = RET_CHUNK
    log_gf = jnp.log1p(-jnp.exp2(-5.0 - jnp.arange(heads, dtype=F32)))
    log_gb = log_gf[::-1]
    pos = jnp.arange(c, dtype=F32)
    diff = pos[:, None] - pos[None, :]
    lower = diff >= 0
    upper = diff < 0
    dec_f = jnp.where(lower[None], jnp.exp(jnp.where(lower, diff, 0.0)[None]
                                           * log_gf[:, None, None]), 0.0)
    dec_b = jnp.where(upper[None], jnp.exp(jnp.where(upper, -diff, 0.0)[None]
                                           * log_gb[:, None, None]), 0.0)
    dec = dec_f + dec_b
    zeta_f = jnp.exp((c - 1.0 - pos)[None, :] * log_gf[:, None])
    xi_f = jnp.exp((pos + 1.0)[None, :] * log_gf[:, None])
    zeta_b = jnp.exp(pos[None, :] * log_gb[:, None])
    xi_b = jnp.exp((c - pos)[None, :] * log_gb[:, None])
    wide = lambda t: jnp.broadcast_to(t[:, :, None], (heads, c, LANES))
    g_f = jnp.exp(c * log_gf)
    g_b = jnp.exp(c * log_gb)
    gc = jnp.zeros((heads, SUBLANES, dv), F32)
    gc = gc.at[:, 0, :].set(g_f[:, None]).at[:, 1, :].set(g_b[:, None])
    return (dec, wide(zeta_f), wide(zeta_b), wide(xi_f), wide(xi_b), gc)


def _rope_tables(seq, hd):
    half = hd // 2
    inv_freq = 1.0 / (ROPE_BASE ** jnp.linspace(0.0, 1.0, half, dtype=F32))
    ang = jnp.arange(seq, dtype=F32)[:, None] * inv_freq[None, :]
    return jnp.cos(ang), jnp.sin(ang)


def kernel(x, ln_mix, ln_ffn, ln_final, lru_w_in, lru_conv_w, lru_conv_b,
           lru_gate_a_w, lru_gate_a_b, lru_gate_x_w, lru_gate_x_b, lru_lambda,
           lru_w_out, ret_w_in, ret_w_out, ffn_w_gate, ffn_w_up, ffn_w_down):
    b, seq, d = x.shape
    depth = ln_mix.shape[0]
    rows = b * seq
    assert b == SUBLANES and seq % ROW_TILE == 0 and seq % LRU_T == 0
    row = lambda v: v.reshape(1, -1)

    ret_w_in, ret_w_out, ffn_w_gate, ffn_w_up, ffn_w_down = (
        w.astype(BF16) for w in (ret_w_in, ret_w_out, ffn_w_gate, ffn_w_up, ffn_w_down))

    h = x.reshape(rows, d)
    for layer in range(depth):
        j = layer // 2
        g_final = row(ln_final) if layer == depth - 1 else None
        proj = None
        if layer % 2 == 0:
            x3 = h.reshape(b, seq, d)
            xh, hf, gb = _lru_fwd(
                x3, row(ln_mix[layer]), lru_w_in, j, lru_conv_w[j],
                row(lru_conv_b[j]), lru_gate_a_w, lru_gate_x_w,
                row(lru_gate_a_b[j, 0]), row(lru_gate_x_b[j, 0]),
                row(lru_lambda[j, 0]))
            h = _lru_bwd(xh, hf, gb, x3, lru_gate_a_w, lru_gate_x_w,
                         row(lru_gate_a_b[j, 1]), row(lru_gate_x_b[j, 1]),
                         row(lru_lambda[j, 1]), lru_w_out, j).reshape(rows, d)
        else:
            heads = RET_HEADS
            vd = ret_w_out.shape[1]
            qk = (ret_w_in.shape[2] - 2 * vd) // 2
            cos, sin = _rope_tables(seq, qk // heads)
            q, k, v, sg = _ret_in(h, row(ln_mix[layer]), ret_w_in, j,
                                  cos, sin, seq, heads, qk, vd)
            tabs = _ret_tables(heads, qk // heads, vd // heads)
            yg = _ret_core(q.reshape(b, seq, qk), k.reshape(b, seq, qk),
                           v.reshape(b, seq, vd), sg.reshape(b, seq, vd), tabs, heads)
            proj = (yg.reshape(rows, vd), ret_w_out, j)
        h = _ffn(h, row(ln_ffn[layer]), ffn_w_gate, ffn_w_up, ffn_w_down, layer,
                 g_final, proj)
    return h.reshape(b, seq, d)
```

```python
import functools
import math

import jax
import jax.numpy as jnp
from jax import lax
from jax.experimental import pallas as pl
from jax.experimental.pallas import tpu as pltpu

F32 = jnp.float32
BF16 = jnp.bfloat16

NORM_EPS = 1e-6
RSQRT_FLOOR = 1e-30
RG_LRU_C = 8.0
LRU_BLOCKS = 4
CONV_WIDTH = 4
CONV_PAD_LEFT = 2
RET_HEADS = 4
ROPE_BASE = 10000.0

SUBLANES = 8
LANES = 128
VMEM_LIMIT_BYTES = 56 * 1024 * 1024
ROW_TILE = 1024
LRU_T = 128
GATE_ROWS = 256
PROJ_ROWS = 1024
PROJ_COLS = 256
OUT_COLS = 256
RET_CHUNK = 256


def _cparams(*sem):
    return pltpu.CompilerParams(dimension_semantics=sem,
                                vmem_limit_bytes=VMEM_LIMIT_BYTES)


def _const_spec(shape):
    nd = len(shape)
    return pl.BlockSpec(shape, lambda *_: (0,) * nd, pipeline_mode=pl.Buffered(1))


def _layer_spec(shape, layer):
    nd = len(shape)
    return pl.BlockSpec((None,) + tuple(shape[1:]), lambda *_: (layer,) + (0,) * (nd - 1),
                        pipeline_mode=pl.Buffered(1))


def _gate_spec(shape, layer, direction):
    return pl.BlockSpec((None, None) + tuple(shape[2:]),
                        lambda *_: (layer, direction, 0, 0, 0),
                        pipeline_mode=pl.Buffered(1))


def _gate_scratch_shape(shape):
    blocks, k, n = shape[2:]
    return (blocks, k, 2 * n)


def _fill_gate_weights(wg_scr, ga_ref, gx_ref):
    n = ga_ref.shape[2]
    wg_scr[:, :, :n] = ga_ref[...].astype(BF16)
    wg_scr[:, :, n:] = gx_ref[...].astype(BF16)


def _rms(x, g):
    ms = jnp.mean(x * x, axis=-1, keepdims=True)
    return x * lax.rsqrt(ms + NORM_EPS) * g


def _sigmoid(x):
    return 0.5 * jnp.tanh(0.5 * x) + 0.5


def _gelu_tanh(x):
    c = math.sqrt(2.0 / math.pi)
    hx = 0.5 * x
    return hx * jnp.tanh(x * ((c * 0.044715) * (x * x) + c)) + hx


def _softplus(x):
    return jnp.maximum(x, 0.0) + jnp.log(1.0 + jnp.exp(-jnp.abs(x)))


def _lru_gate_pieces(xh_ref, wg_ref, ba_ref, bx_ref, lam_ref, a_dst, inp_dst):
    rows, width = xh_ref.shape
    bw = width // LRU_BLOCKS
    c2 = (-0.5 * RG_LRU_C * math.log2(math.e)) * _softplus(-lam_ref[...])
    ba2 = 0.5 * ba_ref[...]
    bx2 = 0.5 * bx_ref[...]

    def piece(r0, n):
        cs = slice(n * bw, (n + 1) * bw)
        xh = xh_ref[pl.ds(r0, GATE_ROWS), cs]
        z = jnp.dot(xh.astype(BF16), wg_ref[n], preferred_element_type=F32)
        tr = jnp.tanh(z[:, :bw] + ba2[:, cs])
        ti = jnp.tanh(z[:, bw:] + bx2[:, cs])
        a = jnp.exp2(c2[:, cs] * (tr + 1.0))
        om = 1.0 - a * a
        root = om * lax.rsqrt(jnp.maximum(om, RSQRT_FLOOR))
        a_dst[pl.ds(r0, GATE_ROWS), cs] = a
        inp_dst[pl.ds(r0, GATE_ROWS), cs] = ((ti + 1.0) * xh.astype(F32)) * root

    return [functools.partial(piece, r0, n)
            for r0 in range(0, rows, GATE_ROWS) for n in range(LRU_BLOCKS)]


def _lru_fwd_kernel(x_ref, g_ref, w_ref, cw_ref, cb_ref, ga_ref, gx_ref, ba_ref, bx_ref,
                    lam_ref, xh_ref, hf_ref, gb_ref, ext_scr, xh_scr,
                    a_scr, inp_scr, carry_scr, w_scr, wg_scr, hn_scr):
    i = pl.program_id(0)
    nt = pl.num_programs(0) - 2
    nb, t, _ = x_ref.shape
    rows, width = xh_ref.shape
    halo = (CONV_WIDTH - 1) * nb
    nslab = width // LANES

    @pl.when(i == 0)
    def _():
        carry_scr[...] = jnp.zeros_like(carry_scr)
        ext_scr[...] = jnp.zeros_like(ext_scr)
        w_scr[...] = w_ref[...].astype(BF16)
        _fill_gate_weights(wg_scr, ga_ref, gx_ref)

    cwh = 0.5 * cw_ref[...]
    cbh = 0.5 * cb_ref[...]
    for l in range(nslab):
        ls = slice(l * LANES, (l + 1) * LANES)
        acc = cbh[:, ls] + cwh[0:1, ls] * ext_scr[l, pl.ds(0, rows), :]
        for k in range(1, CONV_WIDTH):
            acc = acc + cwh[k:k + 1, ls] * ext_scr[l, pl.ds(k * nb, rows), :]
        xh_scr[:, ls] = acc
    xh_ref[...] = xh_scr[...].astype(xh_ref.dtype)

    ext_scr[:, pl.ds(0, halo), :] = ext_scr[:, pl.ds(rows, halo), :]
    gate_pieces = _lru_gate_pieces(xh_scr, wg_scr, ba_ref, bx_ref, lam_ref,
                                   a_scr, inp_scr)
    nbs = PROJ_ROWS // t
    n_dots = (nb // nbs) * (2 * width // PROJ_COLS)
    per_dot = -(-len(gate_pieces) // n_dots)
    for b0 in range(0, nb, nbs):
        hn = hn_scr.at[b0 // nbs]
        for bb in range(nbs):
            hn[pl.ds(bb * t, t), :] = _rms(x_ref[b0 + bb], g_ref[...]).astype(BF16)
        for c0 in range(0, 2 * width, PROJ_COLS):
            u = jnp.dot(hn[...], w_scr[:, c0:c0 + PROJ_COLS], preferred_element_type=F32)
            for bb in range(nbs):
                ub = u[bb * t:(bb + 1) * t]
                if c0 < width:
                    for l in range(PROJ_COLS // LANES):
                        ext_scr[c0 // LANES + l, pl.ds(halo + b0 + bb, t, stride=nb), :] = (
                            ub[:, l * LANES:(l + 1) * LANES])
                else:
                    gb_ref[b0 + bb, :, c0 - width:c0 - width + PROJ_COLS] = (
                        ub.astype(gb_ref.dtype))
            for _ in range(per_dot):
                if gate_pieces:
                    gate_pieces.pop(0)()
    for piece in gate_pieces:
        piece()

    h = jnp.where(i > 1, carry_scr[...], 0.0)
    for s in range(rows // nb):
        h = a_scr[pl.ds(s * nb, nb), :] * h + inp_scr[pl.ds(s * nb, nb), :]
        if s == 0:
            h = jnp.where(i > 1, h, 0.0)
        inp_scr[pl.ds(s * nb, nb), :] = h
    carry_scr[...] = h
    hf_ref[...] = inp_scr[...].astype(hf_ref.dtype)

    @pl.when(i == nt)
    def _():
        ext_scr[:, pl.ds(halo, nb), :] = jnp.zeros((nslab, nb, LANES), F32)

    @pl.when(i == nt + 1)
    def _():
        xh_ref[pl.ds(2 * nb, rows - 2 * nb), :] = jnp.zeros(
            (rows - 2 * nb, width), xh_ref.dtype)
        xh_ref[pl.ds(0, 2 * nb), :] = jnp.where(
            lax.broadcasted_iota(jnp.int32, (2 * nb, width), 0) < nb,
            xh_ref[pl.ds(0, 2 * nb), :], jnp.zeros((), xh_ref.dtype))


def _lru_fwd(x, g, w_in, j, conv_w, conv_b, ga_w, gx_w, ba, bx, lam):
    nb, seq, d = x.shape
    width = w_in.shape[2] // 2
    t = LRU_T
    r = nb * t
    nt = seq // t
    halo = (CONV_WIDTH - 1) * nb
    assert CONV_WIDTH - 1 - CONV_PAD_LEFT == 1
    tile = lambda i: (0, jnp.minimum(i, nt - 1), 0)
    win = lambda i: (jnp.maximum(i - 1, 0), 0)
    return pl.pallas_call(
        _lru_fwd_kernel,
        grid=(nt + 2,),
        in_specs=[pl.BlockSpec((nb, t, d), tile), _const_spec((1, d)),
                  _layer_spec(w_in.shape, j), _const_spec(conv_w.shape),
                  _const_spec(conv_b.shape), _gate_spec(ga_w.shape, j, 0),
                  _gate_spec(gx_w.shape, j, 0),
                  _const_spec(ba.shape), _const_spec(bx.shape), _const_spec(lam.shape)],
        out_specs=[pl.BlockSpec((r, width), win), pl.BlockSpec((r, width), win),
                   pl.BlockSpec((nb, t, width), tile)],
        out_shape=[jax.ShapeDtypeStruct(((nt + 1) * r, width), BF16),
                   jax.ShapeDtypeStruct(((nt + 1) * r, width), BF16),
                   jax.ShapeDtypeStruct((nb, seq, width), BF16)],
        scratch_shapes=[pltpu.VMEM((width // LANES, halo + r, LANES), F32),
                        pltpu.VMEM((r, width), F32),
                        pltpu.VMEM((r, width), F32),
                        pltpu.VMEM((r, width), F32),
                        pltpu.VMEM((nb, width), F32),
                        pltpu.VMEM(w_in.shape[1:], BF16),
                        pltpu.VMEM(_gate_scratch_shape(ga_w.shape), BF16),
                        pltpu.VMEM((r // PROJ_ROWS, PROJ_ROWS, d), BF16)],
        compiler_params=_cparams("arbitrary"),
        name="lru_fwd",
    )(x, g, w_in, conv_w, conv_b, ga_w, gx_w, ba, bx, lam)


def _lru_bwd_kernel(xh_ref, hf_ref, gb_ref, x_ref, ga_ref, gx_ref, ba_ref, bx_ref,
                    lam_ref, wo_ref, o_ref, a_scr, inp_scr, hb_scr, carry_scr,
                    ysum_scr, ybm_scr, wo_scr, wg_scr):
    i = pl.program_id(0)
    nb, t, _ = x_ref.shape
    rows, width = xh_ref.shape
    nsteps = rows // nb
    nslab = width // LANES

    @pl.when(i == 0)
    def _():
        carry_scr[...] = jnp.zeros_like(carry_scr)
        hb_scr[...] = jnp.zeros_like(hb_scr)
        ysum_scr[:, pl.ds(rows, nb), :] = jnp.zeros((nslab, nb, LANES), F32)
        wo_scr[...] = wo_ref[...].astype(BF16)
        _fill_gate_weights(wg_scr, ga_ref, gx_ref)

    for l in range(nslab):
        ls = slice(l * LANES, (l + 1) * LANES)
        ysum_scr[l, pl.ds(0, rows), :] = hb_scr[:, ls] + hf_ref[:, ls].astype(F32)

    for b in range(nb):
        for l in range(nslab):
            ls = slice(l * LANES, (l + 1) * LANES)
            yb = ysum_scr[l, pl.ds(nb + b, t, stride=nb), :]
            ybm_scr[pl.ds(b * t, t), ls] = (
                yb * _gelu_tanh(gb_ref[b, :, ls].astype(F32))).astype(BF16)
    ysum_scr[:, pl.ds(rows, nb), :] = ysum_scr[:, pl.ds(0, nb), :]

    gate_pieces = _lru_gate_pieces(xh_ref, wg_scr, ba_ref, bx_ref, lam_ref,
                                   a_scr, inp_scr)
    gate_pieces = [p for r0 in range(rows - GATE_ROWS, -1, -GATE_ROWS)
                   for p in gate_pieces[(r0 // GATE_ROWS) * LRU_BLOCKS:
                                        (r0 // GATE_ROWS + 1) * LRU_BLOCKS]]
    d_out = wo_ref.shape[1]
    per_dot = -(-len(gate_pieces) * OUT_COLS // d_out)
    for c0 in range(0, d_out, OUT_COLS):
        cs = slice(c0, c0 + OUT_COLS)
        res = jnp.dot(ybm_scr[...], wo_scr[:, cs], preferred_element_type=F32)
        for b in range(nb):
            o_ref[b, :, cs] = x_ref[b, :, cs] + res[b * t:(b + 1) * t]
        for _ in range(per_dot):
            if gate_pieces:
                gate_pieces.pop(0)()
    for piece in gate_pieces:
        piece()

    h = carry_scr[...]
    for s in range(nsteps - 1, -1, -1):
        h = a_scr[pl.ds(s * nb, nb), :] * h + inp_scr[pl.ds(s * nb, nb), :]
        hb_scr[pl.ds(s * nb, nb), :] = h
    carry_scr[...] = h


def _lru_bwd(xh, hf, gb, x, ga_w, gx_w, ba, bx, lam, w_out, j):
    nb, seq, d = x.shape
    width = xh.shape[1]
    t = LRU_T
    r = nb * t
    nt = seq // t
    win_g = lambda i: (jnp.maximum(nt - i, 0), 0)
    win_s = lambda i: (jnp.minimum(nt - i + 1, nt), 0)
    tile = lambda i: (0, jnp.minimum(nt - i + 1, nt - 1), 0)
    return pl.pallas_call(
        _lru_bwd_kernel,
        grid=(nt + 2,),
        in_specs=[pl.BlockSpec((r, width), win_g), pl.BlockSpec((r, width), win_s),
                  pl.BlockSpec((nb, t, width), tile), pl.BlockSpec((nb, t, d), tile),
                  _gate_spec(ga_w.shape, j, 1), _gate_spec(gx_w.shape, j, 1),
                  _const_spec(ba.shape), _const_spec(bx.shape),
                  _const_spec(lam.shape), _layer_spec(w_out.shape, j)],
        out_specs=pl.BlockSpec((nb, t, d), tile),
        out_shape=jax.ShapeDtypeStruct((nb, seq, d), F32),
        scratch_shapes=[pltpu.VMEM((r, width), F32),
                        pltpu.VMEM((r, width), F32),
                        pltpu.VMEM((r, width), F32),
                        pltpu.VMEM((nb, width), F32),
                        pltpu.VMEM((width // LANES, r + nb, LANES), F32),
                        pltpu.VMEM((r, width), BF16),
                        pltpu.VMEM(w_out.shape[1:], BF16),
                        pltpu.VMEM(_gate_scratch_shape(ga_w.shape), BF16)],
        compiler_params=_cparams("arbitrary"),
        name="lru_bwd",
    )(xh, hf, gb, x, ga_w, gx_w, ba, bx, lam, w_out)


def _ffn_chunks(hidden):
    chunks, c = [], 0
    while c < hidden:
        size = 512 if hidden - c >= 512 else hidden - c
        chunks.append((c, size))
        c += size
    return chunks


def _ffn_kernel(*refs, final_norm, fuse_proj):
    refs = list(refs)
    h_ref = refs.pop(0)
    if fuse_proj:
        y_ref, wo_ref = refs.pop(0), refs.pop(0)
    g_ref, wg_ref, wu_ref, wd_ref = (refs.pop(0) for _ in range(4))
    gf_ref = refs.pop(0) if final_norm else None
    o_ref, hn_scr, act_scr = refs
    if fuse_proj:
        res_ref = o_ref
        res_ref[...] = h_ref[...] + jnp.dot(y_ref[...], wo_ref[...],
                                            preferred_element_type=F32)
    else:
        res_ref = h_ref
    hn_scr[...] = _rms(res_ref[...], g_ref[...]).astype(BF16)
    for c, size in _ffn_chunks(wg_ref.shape[1]):
        gt = jnp.dot(hn_scr[...], wg_ref[:, c:c + size], preferred_element_type=F32)
        up = jnp.dot(hn_scr[...], wu_ref[:, c:c + size], preferred_element_type=F32)
        act_scr[:, c:c + size] = (gt * _sigmoid(gt) * up).astype(BF16)
    y = res_ref[...] + jnp.dot(act_scr[...], wd_ref[...], preferred_element_type=F32)
    if final_norm:
        y = _rms(y, gf_ref[...])
    o_ref[...] = y


def _ffn(h, g, w_gate, w_up, w_down, layer, g_final=None, proj=None):
    rows, d = h.shape
    hidden = w_gate.shape[2]
    final_norm = g_final is not None
    fuse_proj = proj is not None
    r = ROW_TILE
    row = lambda i: (i, 0)
    in_specs = [pl.BlockSpec((r, d), row)]
    args = [h]
    scratch = [pltpu.VMEM((r, d), BF16), pltpu.VMEM((r, hidden), BF16)]
    if fuse_proj:
        y, w_out, j = proj
        in_specs += [pl.BlockSpec((r, y.shape[1]), row), _layer_spec(w_out.shape, j)]
        args += [y, w_out]
    in_specs += [_const_spec((1, d)), _layer_spec(w_gate.shape, layer),
                 _layer_spec(w_up.shape, layer), _layer_spec(w_down.shape, layer)]
    args += [g, w_gate, w_up, w_down]
    if final_norm:
        in_specs.append(_const_spec((1, d)))
        args.append(g_final)
    return pl.pallas_call(
        functools.partial(_ffn_kernel, final_norm=final_norm, fuse_proj=fuse_proj),
        grid=(rows // r,),
        in_specs=in_specs,
        out_specs=pl.BlockSpec((r, d), row),
        out_shape=jax.ShapeDtypeStruct((rows, d), F32),
        scratch_shapes=scratch,
        compiler_params=_cparams("parallel"),
        name="ffn" + ("_proj" if fuse_proj else "") + ("_final" if final_norm else ""),
    )(*args)


def _ret_in_kernel(h_ref, g_ref, w_ref, cos_ref, sin_ref, q_ref, k_ref, v_ref,
                   sg_ref, hn_scr, *, heads):
    qk = q_ref.shape[1]
    vd = v_ref.shape[1]
    hd = qk // heads
    half = hd // 2
    hn_scr[...] = _rms(h_ref[...], g_ref[...]).astype(BF16)
    cos = cos_ref[...]
    sin = sin_ref[...]
    k_scale = hd ** -0.5
    for h in range(heads):
        for which, dst, scale in ((0, q_ref, 1.0), (1, k_ref, k_scale)):
            c0 = which * qk + h * hd
            u = jnp.dot(hn_scr[...], w_ref[:, c0:c0 + hd], preferred_element_type=F32)
            x1, x2 = u[:, :half], u[:, half:]
            r1 = x1 * cos - x2 * sin
            r2 = x2 * cos + x1 * sin
            if scale != 1.0:
                r1, r2 = r1 * scale, r2 * scale
            dst[:, h * hd:h * hd + half] = r1.astype(dst.dtype)
            dst[:, h * hd + half:(h + 1) * hd] = r2.astype(dst.dtype)
    step = 512
    for c in range(0, vd, step):
        v_ref[:, c:c + step] = jnp.dot(
            hn_scr[...], w_ref[:, 2 * qk + c:2 * qk + c + step],
            preferred_element_type=F32).astype(v_ref.dtype)
    for c in range(0, vd, step):
        gt = jnp.dot(hn_scr[...], w_ref[:, 2 * qk + vd + c:2 * qk + vd + c + step],
                     preferred_element_type=F32)
        sg_ref[:, c:c + step] = (gt * _sigmoid(gt)).astype(sg_ref.dtype)


def _ret_in(h, g, w_in, j, cos, sin, seq, heads, qk, vd):
    rows, d = h.shape
    r = ROW_TILE
    half = cos.shape[1]
    per_seq = seq // r
    pos = lambda i: (i % per_seq, 0)
    row = lambda i: (i, 0)
    return pl.pallas_call(
        functools.partial(_ret_in_kernel, heads=heads),
        grid=(rows // r,),
        in_specs=[pl.BlockSpec((r, d), row), _const_spec((1, d)),
                  _layer_spec(w_in.shape, j),
                  pl.BlockSpec((r, half), pos), pl.BlockSpec((r, half), pos)],
        out_specs=[pl.BlockSpec((r, qk), row), pl.BlockSpec((r, qk), row),
                   pl.BlockSpec((r, vd), row), pl.BlockSpec((r, vd), row)],
        out_shape=[jax.ShapeDtypeStruct((rows, qk), BF16),
                   jax.ShapeDtypeStruct((rows, qk), BF16),
                   jax.ShapeDtypeStruct((rows, vd), BF16),
                   jax.ShapeDtypeStruct((rows, vd), BF16)],
        scratch_shapes=[pltpu.VMEM((r, d), BF16)],
        compiler_params=_cparams("parallel"),
        name="ret_in",
    )(h, g, w_in, cos, sin)


def _lane_tiled(t, n):
    return jnp.concatenate([t] * (n // LANES), axis=1)


def _ret_core_kernel(q_ref, k_ref, v_ref, sg_ref, dec_ref, zf_ref, zb_ref, xf_ref,
                     xb_ref, gc_ref, o_ref, sf_scr, sb_scr, stf_scr, stb_scr, y_scr):
    seq, dk = q_ref.shape
    dv = v_ref.shape[1]
    c = dec_ref.shape[0]
    nc = seq // c
    gcf = gc_ref[pl.ds(0, 1), :]
    gcb = gc_ref[pl.ds(1, 1), :]

    def kv_update(state, n, z_ref, gcd):
        rs = pl.ds(n * c, c)
        kz = (k_ref[rs, :].astype(F32) * _lane_tiled(z_ref[...], dk)).astype(BF16)
        upd = lax.dot_general(kz, v_ref[rs, :], (((0,), (0,)), ((), ())),
                              preferred_element_type=F32)
        return state * gcd + upd

    stf_scr[...] = jnp.zeros_like(stf_scr)
    stb_scr[...] = jnp.zeros_like(stb_scr)
    sf_scr[0] = jnp.zeros((dk, dv), BF16)
    sb_scr[nc - 1] = jnp.zeros((dk, dv), BF16)
    for s in range(nc - 1):
        stf_scr[...] = kv_update(stf_scr[...], s, zf_ref, gcf)
        sf_scr[s + 1] = stf_scr[...].astype(BF16)
        stb_scr[...] = kv_update(stb_scr[...], nc - 1 - s, zb_ref, gcb)
        sb_scr[nc - 2 - s] = stb_scr[...].astype(BF16)

    def finish(n):
        rs = pl.ds(n * c, c)
        y = y_scr[n % 2]
        y = y * lax.rsqrt(jnp.mean(y * y, axis=-1, keepdims=True) + NORM_EPS)
        o_ref[rs, :] = (y * sg_ref[rs, :].astype(F32)).astype(o_ref.dtype)

    for n in range(nc):
        rs = pl.ds(n * c, c)
        q = q_ref[rs, :]
        scores = lax.dot_general(q, k_ref[rs, :], (((1,), (1,)), ((), ())),
                                 preferred_element_type=F32)
        qf32 = q.astype(F32)
        qf = (qf32 * _lane_tiled(xf_ref[...], dk)).astype(BF16)
        qb = (qf32 * _lane_tiled(xb_ref[...], dk)).astype(BF16)
        p = (scores * dec_ref[...]).astype(BF16)
        lhs = jnp.concatenate([qf, qb, p], axis=1)
        rhs = jnp.concatenate([sf_scr[n], sb_scr[n], v_ref[rs, :]], axis=0)
        y_scr[n % 2] = jnp.dot(lhs, rhs, preferred_element_type=F32)
        if n > 0:
            finish(n - 1)
    finish(nc - 1)


def _ret_core(q, k, v, sg, tabs, heads):
    b, seq, qk = q.shape
    vd = v.shape[2]
    dk, dv = qk // heads, vd // heads
    c = RET_CHUNK
    dec, zf, zb, xf, xb, gc = tabs
    bh = lambda i, j: (i, 0, j)
    hh = lambda i, j: (j, 0, 0)
    return pl.pallas_call(
        _ret_core_kernel,
        grid=(b, heads),
        in_specs=[pl.BlockSpec((None, seq, dk), bh), pl.BlockSpec((None, seq, dk), bh),
                  pl.BlockSpec((None, seq, dv), bh), pl.BlockSpec((None, seq, dv), bh),
                  pl.BlockSpec((None, c, c), hh),
                  pl.BlockSpec((None, c, LANES), hh), pl.BlockSpec((None, c, LANES), hh),
                  pl.BlockSpec((None, c, LANES), hh), pl.BlockSpec((None, c, LANES), hh),
                  pl.BlockSpec((None, SUBLANES, dv), hh)],
        out_specs=pl.BlockSpec((None, seq, dv), bh),
        out_shape=jax.ShapeDtypeStruct((b, seq, vd), BF16),
        scratch_shapes=[pltpu.VMEM((seq // c, dk, dv), BF16),
                        pltpu.VMEM((seq // c, dk, dv), BF16),
                        pltpu.VMEM((dk, dv), F32),
                        pltpu.VMEM((dk, dv), F32),
                        pltpu.VMEM((2, c, dv), F32)],
        compiler_params=_cparams("parallel", "parallel"),
        name="ret_core",
    )(q, k, v, sg, dec, zf, zb, xf, xb, gc)


def _ret_tables(heads, dk, dv):
    c = RET_CHUNK
    log_gf = jnp.log1p(-jnp.exp2(-5.0 - jnp.arange(heads, dtype=F32)))
    log_gb = log_gf[::-1]
    pos = jnp.arange(c, dtype=F32)
    diff = pos[:, None] - pos[None, :]
    lower = diff >= 0
    upper = diff < 0
    dec_f = jnp.where(lower[None], jnp.exp(jnp.where(lower, diff, 0.0)[None]
                                           * log_gf[:, None, None]), 0.0)
    dec_b = jnp.where(upper[None], jnp.exp(jnp.where(upper, -diff, 0.0)[None]
                                           * log_gb[:, None, None]), 0.0)
    dec = dec_f + dec_b
    zeta_f = jnp.exp((c - 1.0 - pos)[None, :] * log_gf[:, None])
    xi_f = jnp.exp((pos + 1.0)[None, :] * log_gf[:, None])
    zeta_b = jnp.exp(pos[None, :] * log_gb[:, None])
    xi_b = jnp.exp((c - pos)[None, :] * log_gb[:, None])
    wide = lambda t: jnp.broadcast_to(t[:, :, None], (heads, c, LANES))
    g_f = jnp.exp(c * log_gf)
    g_b = jnp.exp(c * log_gb)
    gc = jnp.zeros((heads, SUBLANES, dv), F32)
    gc = gc.at[:, 0, :].set(g_f[:, None]).at[:, 1, :].set(g_b[:, None])
    return (dec, wide(zeta_f), wide(zeta_b), wide(xi_f), wide(xi_b), gc)


def _rope_tables(seq, hd):
    half = hd // 2
    inv_freq = 1.0 / (ROPE_BASE ** jnp.linspace(0.0, 1.0, half, dtype=F32))
    ang = jnp.arange(seq, dtype=F32)[:, None] * inv_freq[None, :]
    return jnp.cos(ang), jnp.sin(ang)


def kernel(x, ln_mix, ln_ffn, ln_final, lru_w_in, lru_conv_w, lru_conv_b,
           lru_gate_a_w, lru_gate_a_b, lru_gate_x_w, lru_gate_x_b, lru_lambda,
           lru_w_out, ret_w_in, ret_w_out, ffn_w_gate, ffn_w_up, ffn_w_down):
    b, seq, d = x.shape
    depth = ln_mix.shape[0]
    rows = b * seq
    assert b == SUBLANES and seq % ROW_TILE == 0 and seq % LRU_T == 0
    row = lambda v: v.reshape(1, -1)

    ret_w_in, ret_w_out, ffn_w_gate, ffn_w_up, ffn_w_down = (
        w.astype(BF16) for w in (ret_w_in, ret_w_out, ffn_w_gate, ffn_w_up, ffn_w_down))

    h = x.reshape(rows, d)
    for layer in range(depth):
        j = layer // 2
        g_final = row(ln_final) if layer == depth - 1 else None
        proj = None
        if layer % 2 == 0:
            x3 = h.reshape(b, seq, d)
            xh, hf, gb = _lru_fwd(
                x3, row(ln_mix[layer]), lru_w_in, j, lru_conv_w[j],
                row(lru_conv_b[j]), lru_gate_a_w, lru_gate_x_w,
                row(lru_gate_a_b[j, 0]), row(lru_gate_x_b[j, 0]),
                row(lru_lambda[j, 0]))
            h = _lru_bwd(xh, hf, gb, x3, lru_gate_a_w, lru_gate_x_w,
                         row(lru_gate_a_b[j, 1]), row(lru_gate_x_b[j, 1]),
                         row(lru_lambda[j, 1]), lru_w_out, j).reshape(rows, d)
        else:
            heads = RET_HEADS
            vd = ret_w_out.shape[1]
            qk = (ret_w_in.shape[2] - 2 * vd) // 2
            cos, sin = _rope_tables(seq, qk // heads)
            q, k, v, sg = _ret_in(h, row(ln_mix[layer]), ret_w_in, j,
                                  cos, sin, seq, heads, qk, vd)
            tabs = _ret_tables(heads, qk // heads, vd // heads)
            yg = _ret_core(q.reshape(b, seq, qk), k.reshape(b, seq, qk),
                           v.reshape(b, seq, vd), sg.reshape(b, seq, vd), tabs, heads)
            proj = (yg.reshape(rows, vd), ret_w_out, j)
        h = _ffn(h, row(ln_ffn[layer]), ffn_w_gate, ffn_w_up, ffn_w_down, layer,
                 g_final, proj)
    return h.reshape(b, seq, d)
```

```python
import functools
import math

import jax
import jax.numpy as jnp
import numpy as np
from jax import lax
from jax.experimental import pallas as pl
from jax.experimental.pallas import tpu as pltpu

F32 = jnp.float32
BF16 = jnp.bfloat16

NORM_EPS = 1e-6
RSQRT_FLOOR = 1e-30
RG_LRU_C = 8.0
LRU_BLOCKS = 4
CONV_WIDTH = 4
CONV_PAD_LEFT = 2
RET_HEADS = 4
ROPE_BASE = 10000.0

SUBLANES = 8
LANES = 128
VMEM_LIMIT_BYTES = 56 * 1024 * 1024
ROW_TILE = 1024
LRU_T = 128
GATE_ROWS = 256
PROJ_ROWS = 1024
PROJ_COLS = 256
OUT_COLS = 256
RET_CHUNK = 256


def _cparams(*sem):
    return pltpu.CompilerParams(dimension_semantics=sem,
                                vmem_limit_bytes=VMEM_LIMIT_BYTES)


def _const_spec(shape):
    nd = len(shape)
    return pl.BlockSpec(shape, lambda *_: (0,) * nd, pipeline_mode=pl.Buffered(1))


def _layer_spec(shape, layer):
    nd = len(shape)
    return pl.BlockSpec((None,) + tuple(shape[1:]), lambda *_: (layer,) + (0,) * (nd - 1),
                        pipeline_mode=pl.Buffered(1))


def _gate_spec(shape, layer, direction):
    return pl.BlockSpec((None, None) + tuple(shape[2:]),
                        lambda *_: (layer, direction, 0, 0, 0),
                        pipeline_mode=pl.Buffered(1))


def _gate_scratch_shape(shape):
    blocks, k, n = shape[2:]
    return (blocks, k, 2 * n)


def _fill_gate_weights(wg_scr, ga_ref, gx_ref):
    n = ga_ref.shape[2]
    wg_scr[:, :, :n] = ga_ref[...].astype(BF16)
    wg_scr[:, :, n:] = gx_ref[...].astype(BF16)


def _rms(x, g):
    ms = jnp.mean(x * x, axis=-1, keepdims=True)
    return x * lax.rsqrt(ms + NORM_EPS) * g


def _sigmoid(x):
    return 0.5 * jnp.tanh(0.5 * x) + 0.5


def _gelu_tanh(x):
    c = math.sqrt(2.0 / math.pi)
    hx = 0.5 * x
    return hx * jnp.tanh(x * ((c * 0.044715) * (x * x) + c)) + hx


def _softplus(x):
    return jnp.maximum(x, 0.0) + jnp.log(1.0 + jnp.exp(-jnp.abs(x)))


def _lru_gate_pieces(xh_ref, wg_ref, ba_ref, bx_ref, lam_ref, a_dst, inp_dst):
    rows, width = xh_ref.shape
    bw = width // LRU_BLOCKS
    c2 = (-0.5 * RG_LRU_C * math.log2(math.e)) * _softplus(-lam_ref[...])
    ba2 = 0.5 * ba_ref[...]
    bx2 = 0.5 * bx_ref[...]

    def piece(r0, n):
        cs = slice(n * bw, (n + 1) * bw)
        xh = xh_ref[pl.ds(r0, GATE_ROWS), cs]
        z = jnp.dot(xh.astype(BF16), wg_ref[n], preferred_element_type=F32)
        tr = jnp.tanh(z[:, :bw] + ba2[:, cs])
        ti = jnp.tanh(z[:, bw:] + bx2[:, cs])
        a = jnp.exp2(c2[:, cs] * (tr + 1.0))
        om = 1.0 - a * a
        root = om * lax.rsqrt(jnp.maximum(om, RSQRT_FLOOR))
        a_dst[pl.ds(r0, GATE_ROWS), cs] = a
        inp_dst[pl.ds(r0, GATE_ROWS), cs] = ((ti + 1.0) * xh.astype(F32)) * root

    return [functools.partial(piece, r0, n)
            for r0 in range(0, rows, GATE_ROWS) for n in range(LRU_BLOCKS)]


def _lru_fwd_kernel(x_ref, g_ref, w_ref, cw_ref, cb_ref, ga_ref, gx_ref, ba_ref, bx_ref,
                    lam_ref, xh_ref, hf_ref, gb_ref, ext_scr, xh_scr,
                    a_scr, inp_scr, carry_scr, w_scr, wg_scr, hn_scr):
    i = pl.program_id(0)
    nt = pl.num_programs(0) - 2
    nb, t, _ = x_ref.shape
    rows, width = xh_ref.shape
    halo = (CONV_WIDTH - 1) * nb
    nslab = width // LANES

    @pl.when(i == 0)
    def _():
        carry_scr[...] = jnp.zeros_like(carry_scr)
        ext_scr[...] = jnp.zeros_like(ext_scr)
        w_scr[...] = w_ref[...].astype(BF16)
        _fill_gate_weights(wg_scr, ga_ref, gx_ref)

    cwh = 0.5 * cw_ref[...]
    cbh = 0.5 * cb_ref[...]
    for l in range(nslab):
        ls = slice(l * LANES, (l + 1) * LANES)
        acc = cbh[:, ls] + cwh[0:1, ls] * ext_scr[l, pl.ds(0, rows), :]
        for k in range(1, CONV_WIDTH):
            acc = acc + cwh[k:k + 1, ls] * ext_scr[l, pl.ds(k * nb, rows), :]
        xh_scr[:, ls] = acc
    xh_ref[...] = xh_scr[...].astype(xh_ref.dtype)

    ext_scr[:, pl.ds(0, halo), :] = ext_scr[:, pl.ds(rows, halo), :]
    gate_pieces = _lru_gate_pieces(xh_scr, wg_scr, ba_ref, bx_ref, lam_ref,
                                   a_scr, inp_scr)
    nbs = PROJ_ROWS // t
    n_dots = (nb // nbs) * (2 * width // PROJ_COLS)
    per_dot = -(-len(gate_pieces) // n_dots)
    for b0 in range(0, nb, nbs):
        hn = hn_scr.at[b0 // nbs]
        for bb in range(nbs):
            hn[pl.ds(bb * t, t), :] = _rms(x_ref[b0 + bb], g_ref[...]).astype(BF16)
        for c0 in range(0, 2 * width, PROJ_COLS):
            u = jnp.dot(hn[...], w_scr[:, c0:c0 + PROJ_COLS], preferred_element_type=F32)
            for bb in range(nbs):
                ub = u[bb * t:(bb + 1) * t]
                if c0 < width:
                    for l in range(PROJ_COLS // LANES):
                        ext_scr[c0 // LANES + l, pl.ds(halo + b0 + bb, t, stride=nb), :] = (
                            ub[:, l * LANES:(l + 1) * LANES])
                else:
                    gb_ref[b0 + bb, :, c0 - width:c0 - width + PROJ_COLS] = (
                        ub.astype(gb_ref.dtype))
            for _ in range(per_dot):
                if gate_pieces:
                    gate_pieces.pop(0)()
    for piece in gate_pieces:
        piece()

    h = jnp.where(i > 1, carry_scr[...], 0.0)
    for s in range(rows // nb):
        h = a_scr[pl.ds(s * nb, nb), :] * h + inp_scr[pl.ds(s * nb, nb), :]
        if s == 0:
            h = jnp.where(i > 1, h, 0.0)
        inp_scr[pl.ds(s * nb, nb), :] = h
    carry_scr[...] = h
    hf_ref[...] = inp_scr[...].astype(hf_ref.dtype)

    @pl.when(i == nt)
    def _():
        ext_scr[:, pl.ds(halo, nb), :] = jnp.zeros((nslab, nb, LANES), F32)

    @pl.when(i == nt + 1)
    def _():
        xh_ref[pl.ds(2 * nb, rows - 2 * nb), :] = jnp.zeros(
            (rows - 2 * nb, width), xh_ref.dtype)
        xh_ref[pl.ds(0, 2 * nb), :] = jnp.where(
            lax.broadcasted_iota(jnp.int32, (2 * nb, width), 0) < nb,
            xh_ref[pl.ds(0, 2 * nb), :], jnp.zeros((), xh_ref.dtype))


def _lru_fwd(x, g, w_in, j, conv_w, conv_b, ga_w, gx_w, ba, bx, lam):
    nb, seq, d = x.shape
    width = w_in.shape[2] // 2
    t = LRU_T
    r = nb * t
    nt = seq // t
    halo = (CONV_WIDTH - 1) * nb
    assert CONV_WIDTH - 1 - CONV_PAD_LEFT == 1
    tile = lambda i: (0, jnp.minimum(i, nt - 1), 0)
    win = lambda i: (jnp.maximum(i - 1, 0), 0)
    return pl.pallas_call(
        _lru_fwd_kernel,
        grid=(nt + 2,),
        in_specs=[pl.BlockSpec((nb, t, d), tile), _const_spec((1, d)),
                  _layer_spec(w_in.shape, j), _const_spec(conv_w.shape),
                  _const_spec(conv_b.shape), _gate_spec(ga_w.shape, j, 0),
                  _gate_spec(gx_w.shape, j, 0),
                  _const_spec(ba.shape), _const_spec(bx.shape), _const_spec(lam.shape)],
        out_specs=[pl.BlockSpec((r, width), win), pl.BlockSpec((r, width), win),
                   pl.BlockSpec((nb, t, width), tile)],
        out_shape=[jax.ShapeDtypeStruct(((nt + 1) * r, width), BF16),
                   jax.ShapeDtypeStruct(((nt + 1) * r, width), BF16),
                   jax.ShapeDtypeStruct((nb, seq, width), BF16)],
        scratch_shapes=[pltpu.VMEM((width // LANES, halo + r, LANES), F32),
                        pltpu.VMEM((r, width), F32),
                        pltpu.VMEM((r, width), F32),
                        pltpu.VMEM((r, width), F32),
                        pltpu.VMEM((nb, width), F32),
                        pltpu.VMEM(w_in.shape[1:], BF16),
                        pltpu.VMEM(_gate_scratch_shape(ga_w.shape), BF16),
                        pltpu.VMEM((r // PROJ_ROWS, PROJ_ROWS, d), BF16)],
        compiler_params=_cparams("arbitrary"),
        name="lru_fwd",
    )(x, g, w_in, conv_w, conv_b, ga_w, gx_w, ba, bx, lam)


def _lru_bwd_kernel(xh_ref, hf_ref, gb_ref, x_ref, ga_ref, gx_ref, ba_ref, bx_ref,
                    lam_ref, wo_ref, o_ref, a_scr, inp_scr, hb_scr, carry_scr,
                    ysum_scr, ybm_scr, wo_scr, wg_scr):
    i = pl.program_id(0)
    nb, t, _ = x_ref.shape
    rows, width = xh_ref.shape
    nsteps = rows // nb
    nslab = width // LANES

    @pl.when(i == 0)
    def _():
        carry_scr[...] = jnp.zeros_like(carry_scr)
        hb_scr[...] = jnp.zeros_like(hb_scr)
        ysum_scr[:, pl.ds(rows, nb), :] = jnp.zeros((nslab, nb, LANES), F32)
        wo_scr[...] = wo_ref[...].astype(BF16)
        _fill_gate_weights(wg_scr, ga_ref, gx_ref)

    for l in range(nslab):
        ls = slice(l * LANES, (l + 1) * LANES)
        ysum_scr[l, pl.ds(0, rows), :] = hb_scr[:, ls] + hf_ref[:, ls].astype(F32)

    for b in range(nb):
        for l in range(nslab):
            ls = slice(l * LANES, (l + 1) * LANES)
            yb = ysum_scr[l, pl.ds(nb + b, t, stride=nb), :]
            ybm_scr[pl.ds(b * t, t), ls] = (
                yb * _gelu_tanh(gb_ref[b, :, ls].astype(F32))).astype(BF16)
    ysum_scr[:, pl.ds(rows, nb), :] = ysum_scr[:, pl.ds(0, nb), :]

    gate_pieces = _lru_gate_pieces(xh_ref, wg_scr, ba_ref, bx_ref, lam_ref,
                                   a_scr, inp_scr)
    gate_pieces = [p for r0 in range(rows - GATE_ROWS, -1, -GATE_ROWS)
                   for p in gate_pieces[(r0 // GATE_ROWS) * LRU_BLOCKS:
                                        (r0 // GATE_ROWS + 1) * LRU_BLOCKS]]
    d_out = wo_ref.shape[1]
    per_dot = -(-len(gate_pieces) * OUT_COLS // d_out)
    for c0 in range(0, d_out, OUT_COLS):
        cs = slice(c0, c0 + OUT_COLS)
        res = jnp.dot(ybm_scr[...], wo_scr[:, cs], preferred_element_type=F32)
        for b in range(nb):
            o_ref[b, :, cs] = x_ref[b, :, cs] + res[b * t:(b + 1) * t]
        for _ in range(per_dot):
            if gate_pieces:
                gate_pieces.pop(0)()
    for piece in gate_pieces:
        piece()

    h = carry_scr[...]
    for s in range(nsteps - 1, -1, -1):
        h = a_scr[pl.ds(s * nb, nb), :] * h + inp_scr[pl.ds(s * nb, nb), :]
        hb_scr[pl.ds(s * nb, nb), :] = h
    carry_scr[...] = h


def _lru_bwd(xh, hf, gb, x, ga_w, gx_w, ba, bx, lam, w_out, j):
    nb, seq, d = x.shape
    width = xh.shape[1]
    t = LRU_T
    r = nb * t
    nt = seq // t
    win_g = lambda i: (jnp.maximum(nt - i, 0), 0)
    win_s = lambda i: (jnp.minimum(nt - i + 1, nt), 0)
    tile = lambda i: (0, jnp.minimum(nt - i + 1, nt - 1), 0)
    return pl.pallas_call(
        _lru_bwd_kernel,
        grid=(nt + 2,),
        in_specs=[pl.BlockSpec((r, width), win_g), pl.BlockSpec((r, width), win_s),
                  pl.BlockSpec((nb, t, width), tile), pl.BlockSpec((nb, t, d), tile),
                  _gate_spec(ga_w.shape, j, 1), _gate_spec(gx_w.shape, j, 1),
                  _const_spec(ba.shape), _const_spec(bx.shape),
                  _const_spec(lam.shape), _layer_spec(w_out.shape, j)],
        out_specs=pl.BlockSpec((nb, t, d), tile),
        out_shape=jax.ShapeDtypeStruct((nb, seq, d), F32),
        scratch_shapes=[pltpu.VMEM((r, width), F32),
                        pltpu.VMEM((r, width), F32),
                        pltpu.VMEM((r, width), F32),
                        pltpu.VMEM((nb, width), F32),
                        pltpu.VMEM((width // LANES, r + nb, LANES), F32),
                        pltpu.VMEM((r, width), BF16),
                        pltpu.VMEM(w_out.shape[1:], BF16),
                        pltpu.VMEM(_gate_scratch_shape(ga_w.shape), BF16)],
        compiler_params=_cparams("arbitrary"),
        name="lru_bwd",
    )(xh, hf, gb, x, ga_w, gx_w, ba, bx, lam, w_out)


def _ffn_chunks(hidden):
    chunks, c = [], 0
    while c < hidden:
        size = 512 if hidden - c >= 512 else hidden - c
        chunks.append((c, size))
        c += size
    return chunks


def _ffn_kernel(*refs, final_norm, fuse_proj):
    refs = list(refs)
    h_ref = refs.pop(0)
    if fuse_proj:
        y_ref, wo_ref = refs.pop(0), refs.pop(0)
    g_ref, wg_ref, wu_ref, wd_ref = (refs.pop(0) for _ in range(4))
    gf_ref = refs.pop(0) if final_norm else None
    o_ref, hn_scr, act_scr = refs
    if fuse_proj:
        res_ref = o_ref
        res_ref[...] = h_ref[...] + jnp.dot(y_ref[...], wo_ref[...],
                                            preferred_element_type=F32)
    else:
        res_ref = h_ref
    hn_scr[...] = _rms(res_ref[...], g_ref[...]).astype(BF16)
    for c, size in _ffn_chunks(wg_ref.shape[1]):
        gt = jnp.dot(hn_scr[...], wg_ref[:, c:c + size], preferred_element_type=F32)
        up = jnp.dot(hn_scr[...], wu_ref[:, c:c + size], preferred_element_type=F32)
        act_scr[:, c:c + size] = (gt * _sigmoid(gt) * up).astype(BF16)
    y = res_ref[...] + jnp.dot(act_scr[...], wd_ref[...], preferred_element_type=F32)
    if final_norm:
        y = _rms(y, gf_ref[...])
    o_ref[...] = y


def _ffn(h, g, w_gate, w_up, w_down, layer, g_final=None, proj=None):
    rows, d = h.shape
    hidden = w_gate.shape[2]
    final_norm = g_final is not None
    fuse_proj = proj is not None
    r = ROW_TILE
    row = lambda i: (i, 0)
    in_specs = [pl.BlockSpec((r, d), row)]
    args = [h]
    scratch = [pltpu.VMEM((r, d), BF16), pltpu.VMEM((r, hidden), BF16)]
    if fuse_proj:
        y, w_out, j = proj
        in_specs += [pl.BlockSpec((r, y.shape[1]), row), _layer_spec(w_out.shape, j)]
        args += [y, w_out]
    in_specs += [_const_spec((1, d)), _layer_spec(w_gate.shape, layer),
                 _layer_spec(w_up.shape, layer), _layer_spec(w_down.shape, layer)]
    args += [g, w_gate, w_up, w_down]
    if final_norm:
        in_specs.append(_const_spec((1, d)))
        args.append(g_final)
    return pl.pallas_call(
        functools.partial(_ffn_kernel, final_norm=final_norm, fuse_proj=fuse_proj),
        grid=(rows // r,),
        in_specs=in_specs,
        out_specs=pl.BlockSpec((r, d), row),
        out_shape=jax.ShapeDtypeStruct((rows, d), F32),
        scratch_shapes=scratch,
        compiler_params=_cparams("parallel"),
        name="ffn" + ("_proj" if fuse_proj else "") + ("_final" if final_norm else ""),
    )(*args)


def _ret_in_kernel(h_ref, g_ref, w_ref, cos_ref, sin_ref, q_ref, k_ref, v_ref,
                   sg_ref, hn_scr, *, heads):
    qk = q_ref.shape[1]
    vd = v_ref.shape[1]
    hd = qk // heads
    half = hd // 2
    hn_scr[...] = _rms(h_ref[...], g_ref[...]).astype(BF16)
    cos = cos_ref[...]
    sin = sin_ref[...]
    k_scale = hd ** -0.5
    for h in range(heads):
        for which, dst, scale in ((0, q_ref, 1.0), (1, k_ref, k_scale)):
            c0 = which * qk + h * hd
            u = jnp.dot(hn_scr[...], w_ref[:, c0:c0 + hd], preferred_element_type=F32)
            x1, x2 = u[:, :half], u[:, half:]
            r1 = x1 * cos - x2 * sin
            r2 = x2 * cos + x1 * sin
            if scale != 1.0:
                r1, r2 = r1 * scale, r2 * scale
            dst[:, h * hd:h * hd + half] = r1.astype(dst.dtype)
            dst[:, h * hd + half:(h + 1) * hd] = r2.astype(dst.dtype)
    step = 512
    for c in range(0, vd, step):
        v_ref[:, c:c + step] = jnp.dot(
            hn_scr[...], w_ref[:, 2 * qk + c:2 * qk + c + step],
            preferred_element_type=F32).astype(v_ref.dtype)
    for c in range(0, vd, step):
        gt = jnp.dot(hn_scr[...], w_ref[:, 2 * qk + vd + c:2 * qk + vd + c + step],
                     preferred_element_type=F32)
        sg_ref[:, c:c + step] = (gt * _sigmoid(gt)).astype(sg_ref.dtype)


def _ret_in(h, g, w_in, j, cos, sin, seq, heads, qk, vd):
    rows, d = h.shape
    r = ROW_TILE
    half = cos.shape[1]
    per_seq = seq // r
    pos = lambda i: (i % per_seq, 0)
    row = lambda i: (i, 0)
    return pl.pallas_call(
        functools.partial(_ret_in_kernel, heads=heads),
        grid=(rows // r,),
        in_specs=[pl.BlockSpec((r, d), row), _const_spec((1, d)),
                  _layer_spec(w_in.shape, j),
                  pl.BlockSpec((r, half), pos), pl.BlockSpec((r, half), pos)],
        out_specs=[pl.BlockSpec((r, qk), row), pl.BlockSpec((r, qk), row),
                   pl.BlockSpec((r, vd), row), pl.BlockSpec((r, vd), row)],
        out_shape=[jax.ShapeDtypeStruct((rows, qk), BF16),
                   jax.ShapeDtypeStruct((rows, qk), BF16),
                   jax.ShapeDtypeStruct((rows, vd), BF16),
                   jax.ShapeDtypeStruct((rows, vd), BF16)],
        scratch_shapes=[pltpu.VMEM((r, d), BF16)],
        compiler_params=_cparams("parallel"),
        name="ret_in",
    )(h, g, w_in, cos, sin)


def _lane_tiled(t, n):
    return jnp.concatenate([t] * (n // LANES), axis=1)


def _ret_core_kernel(q_ref, k_ref, v_ref, sg_ref, dec_ref, zf_ref, zb_ref, xf_ref,
                     xb_ref, gc_ref, o_ref, sf_scr, sb_scr, stf_scr, stb_scr, y_scr):
    seq, dk = q_ref.shape
    dv = v_ref.shape[1]
    c = dec_ref.shape[0]
    nc = seq // c
    gcf = gc_ref[pl.ds(0, 1), :]
    gcb = gc_ref[pl.ds(1, 1), :]

    def kv_update(state, n, z_ref, gcd):
        rs = pl.ds(n * c, c)
        kz = (k_ref[rs, :].astype(F32) * _lane_tiled(z_ref[...], dk)).astype(BF16)
        upd = lax.dot_general(kz, v_ref[rs, :], (((0,), (0,)), ((), ())),
                              preferred_element_type=F32)
        return state * gcd + upd

    stf_scr[...] = jnp.zeros_like(stf_scr)
    stb_scr[...] = jnp.zeros_like(stb_scr)
    sf_scr[0] = jnp.zeros((dk, dv), BF16)
    sb_scr[nc - 1] = jnp.zeros((dk, dv), BF16)
    for s in range(nc - 1):
        stf_scr[...] = kv_update(stf_scr[...], s, zf_ref, gcf)
        sf_scr[s + 1] = stf_scr[...].astype(BF16)
        stb_scr[...] = kv_update(stb_scr[...], nc - 1 - s, zb_ref, gcb)
        sb_scr[nc - 2 - s] = stb_scr[...].astype(BF16)

    def finish(n):
        rs = pl.ds(n * c, c)
        y = y_scr[n % 2]
        y = y * lax.rsqrt(jnp.mean(y * y, axis=-1, keepdims=True) + NORM_EPS)
        o_ref[rs, :] = (y * sg_ref[rs, :].astype(F32)).astype(o_ref.dtype)

    for n in range(nc):
        rs = pl.ds(n * c, c)
        q = q_ref[rs, :]
        scores = lax.dot_general(q, k_ref[rs, :], (((1,), (1,)), ((), ())),
                                 preferred_element_type=F32)
        qf32 = q.astype(F32)
        qf = (qf32 * _lane_tiled(xf_ref[...], dk)).astype(BF16)
        qb = (qf32 * _lane_tiled(xb_ref[...], dk)).astype(BF16)
        p = (scores * dec_ref[...]).astype(BF16)
        lhs = jnp.concatenate([qf, qb, p], axis=1)
        rhs = jnp.concatenate([sf_scr[n], sb_scr[n], v_ref[rs, :]], axis=0)
        y_scr[n % 2] = jnp.dot(lhs, rhs, preferred_element_type=F32)
        if n > 0:
            finish(n - 1)
    finish(nc - 1)


def _ret_core(q, k, v, sg, tabs, heads):
    b, seq, qk = q.shape
    vd = v.shape[2]
    dk, dv = qk // heads, vd // heads
    c = RET_CHUNK
    dec, zf, zb, xf, xb, gc = tabs
    bh = lambda i, j: (i, 0, j)
    hh = lambda i, j: (j, 0, 0)
    return pl.pallas_call(
        _ret_core_kernel,
        grid=(b, heads),
        in_specs=[pl.BlockSpec((None, seq, dk), bh), pl.BlockSpec((None, seq, dk), bh),
                  pl.BlockSpec((None, seq, dv), bh), pl.BlockSpec((None, seq, dv), bh),
                  pl.BlockSpec((None, c, c), hh),
                  pl.BlockSpec((None, c, LANES), hh), pl.BlockSpec((None, c, LANES), hh),
                  pl.BlockSpec((None, c, LANES), hh), pl.BlockSpec((None, c, LANES), hh),
                  pl.BlockSpec((None, SUBLANES, dv), hh)],
        out_specs=pl.BlockSpec((None, seq, dv), bh),
        out_shape=jax.ShapeDtypeStruct((b, seq, vd), BF16),
        scratch_shapes=[pltpu.VMEM((seq // c, dk, dv), BF16),
                        pltpu.VMEM((seq // c, dk, dv), BF16),
                        pltpu.VMEM((dk, dv), F32),
                        pltpu.VMEM((dk, dv), F32),
                        pltpu.VMEM((2, c, dv), F32)],
        compiler_params=_cparams("parallel", "parallel"),
        name="ret_core",
    )(q, k, v, sg, dec, zf, zb, xf, xb, gc)


def _ret_tables(heads, dk, dv):
    c = RET_CHUNK
    log_gf = np.log1p(-np.exp2(-5.0 - np.arange(heads, dtype=np.float64)))
    log_gb = log_gf[::-1]
    pos = np.arange(c, dtype=np.float64)
    diff = pos[:, None] - pos[None, :]
    dec_f = np.where(diff >= 0, np.exp(np.maximum(diff, 0.0)[None] * log_gf[:, None, None]), 0.0)
    dec_b = np.where(diff < 0, np.exp(np.maximum(-diff, 0.0)[None] * log_gb[:, None, None]), 0.0)
    dec = dec_f + dec_b
    zeta_f = np.exp((c - 1.0 - pos)[None, :] * log_gf[:, None])
    xi_f = np.exp((pos + 1.0)[None, :] * log_gf[:, None])
    zeta_b = np.exp(pos[None, :] * log_gb[:, None])
    xi_b = np.exp((c - pos)[None, :] * log_gb[:, None])
    wide = lambda t: np.broadcast_to(t[:, :, None], (heads, c, LANES))
    gc = np.zeros((heads, SUBLANES, dv))
    gc[:, 0, :] = np.exp(c * log_gf)[:, None]
    gc[:, 1, :] = np.exp(c * log_gb)[:, None]
    return tuple(jnp.asarray(np.ascontiguousarray(t), dtype=F32) for t in
                 (dec, wide(zeta_f), wide(zeta_b), wide(xi_f), wide(xi_b), gc))


def _rope_tables(seq, hd):
    half = hd // 2
    inv_freq = 1.0 / (ROPE_BASE ** np.linspace(0.0, 1.0, half, dtype=np.float64))
    ang = np.arange(seq, dtype=np.float64)[:, None] * inv_freq[None, :]
    return jnp.asarray(np.cos(ang), dtype=F32), jnp.asarray(np.sin(ang), dtype=F32)


def kernel(x, ln_mix, ln_ffn, ln_final, lru_w_in, lru_conv_w, lru_conv_b,
           lru_gate_a_w, lru_gate_a_b, lru_gate_x_w, lru_gate_x_b, lru_lambda,
           lru_w_out, ret_w_in, ret_w_out, ffn_w_gate, ffn_w_up, ffn_w_down):
    b, seq, d = x.shape
    depth = ln_mix.shape[0]
    rows = b * seq
    assert b == SUBLANES and seq % ROW_TILE == 0 and seq % LRU_T == 0
    row = lambda v: v.reshape(1, -1)

    ret_w_in, ret_w_out, ffn_w_gate, ffn_w_up, ffn_w_down = (
        w.astype(BF16) for w in (ret_w_in, ret_w_out, ffn_w_gate, ffn_w_up, ffn_w_down))

    h = x.reshape(rows, d)
    for layer in range(depth):
        j = layer // 2
        g_final = row(ln_final) if layer == depth - 1 else None
        proj = None
        if layer % 2 == 0:
            x3 = h.reshape(b, seq, d)
            xh, hf, gb = _lru_fwd(
                x3, row(ln_mix[layer]), lru_w_in, j, lru_conv_w[j],
                row(lru_conv_b[j]), lru_gate_a_w, lru_gate_x_w,
                row(lru_gate_a_b[j, 0]), row(lru_gate_x_b[j, 0]),
                row(lru_lambda[j, 0]))
            h = _lru_bwd(xh, hf, gb, x3, lru_gate_a_w, lru_gate_x_w,
                         row(lru_gate_a_b[j, 1]), row(lru_gate_x_b[j, 1]),
                         row(lru_lambda[j, 1]), lru_w_out, j).reshape(rows, d)
        else:
            heads = RET_HEADS
            vd = ret_w_out.shape[1]
            qk = (ret_w_in.shape[2] - 2 * vd) // 2
            cos, sin = _rope_tables(seq, qk // heads)
            q, k, v, sg = _ret_in(h, row(ln_mix[layer]), ret_w_in, j,
                                  cos, sin, seq, heads, qk, vd)
            tabs = _ret_tables(heads, qk // heads, vd // heads)
            yg = _ret_core(q.reshape(b, seq, qk), k.reshape(b, seq, qk),
                           v.reshape(b, seq, vd), sg.reshape(b, seq, vd), tabs, heads)
            proj = (yg.reshape(rows, vd), ret_w_out, j)
        h = _ffn(h, row(ln_ffn[layer]), ffn_w_gate, ffn_w_up, ffn_w_down, layer,
                 g_final, proj)
    return h.reshape(b, seq, d)
```

```python
import functools
import math

import jax
import jax.numpy as jnp
import numpy as np
from jax import lax
from jax.experimental import pallas as pl
from jax.experimental.pallas import tpu as pltpu

F32 = jnp.float32
BF16 = jnp.bfloat16

NORM_EPS = 1e-6
RSQRT_FLOOR = 1e-30
RG_LRU_C = 8.0
LRU_BLOCKS = 4
CONV_WIDTH = 4
CONV_PAD_LEFT = 2
RET_HEADS = 4
ROPE_BASE = 10000.0

SUBLANES = 8
LANES = 128
VMEM_LIMIT_BYTES = 56 * 1024 * 1024
ROW_TILE = 1024
LRU_T = 128
GATE_ROWS = 256
PROJ_ROWS = 1024
PROJ_COLS = 256
OUT_COLS = 256
RET_CHUNK = 256


def _cparams(*sem):
    return pltpu.CompilerParams(dimension_semantics=sem,
                                vmem_limit_bytes=VMEM_LIMIT_BYTES)


def _const_spec(shape):
    nd = len(shape)
    return pl.BlockSpec(shape, lambda *_: (0,) * nd, pipeline_mode=pl.Buffered(1))


def _layer_spec(shape, layer):
    nd = len(shape)
    return pl.BlockSpec((None,) + tuple(shape[1:]), lambda *_: (layer,) + (0,) * (nd - 1),
                        pipeline_mode=pl.Buffered(1))


def _gate_spec(shape, layer, direction):
    return pl.BlockSpec((None, None) + tuple(shape[2:]),
                        lambda *_: (layer, direction, 0, 0, 0),
                        pipeline_mode=pl.Buffered(1))


def _gate_scratch_shape(shape):
    blocks, k, n = shape[2:]
    return (blocks, k, 2 * n)


def _fill_gate_weights(wg_scr, ga_ref, gx_ref):
    n = ga_ref.shape[2]
    wg_scr[:, :, :n] = ga_ref[...].astype(BF16)
    wg_scr[:, :, n:] = gx_ref[...].astype(BF16)


def _rms(x, g):
    ms = jnp.mean(x * x, axis=-1, keepdims=True)
    return x * lax.rsqrt(ms + NORM_EPS) * g


def _sigmoid(x):
    return 0.5 * jnp.tanh(0.5 * x) + 0.5


def _gelu_tanh(x):
    c = math.sqrt(2.0 / math.pi)
    hx = 0.5 * x
    return hx * jnp.tanh(x * ((c * 0.044715) * (x * x) + c)) + hx


def _softplus(x):
    return jnp.maximum(x, 0.0) + jnp.log(1.0 + jnp.exp(-jnp.abs(x)))


def _lru_gate_pieces(xh_ref, wg_ref, ba_ref, bx_ref, lam_ref, a_dst, inp_dst):
    rows, width = xh_ref.shape
    bw = width // LRU_BLOCKS
    c2 = (-0.5 * RG_LRU_C * math.log2(math.e)) * _softplus(-lam_ref[...])
    ba2 = 0.5 * ba_ref[...]
    bx2 = 0.5 * bx_ref[...]

    def piece(r0, n):
        cs = slice(n * bw, (n + 1) * bw)
        xh = xh_ref[pl.ds(r0, GATE_ROWS), cs]
        z = jnp.dot(xh.astype(BF16), wg_ref[n], preferred_element_type=F32)
        tr = jnp.tanh(z[:, :bw] + ba2[:, cs])
        ti = jnp.tanh(z[:, bw:] + bx2[:, cs])
        a = jnp.exp2(c2[:, cs] * (tr + 1.0))
        om = 1.0 - a * a
        root = om * lax.rsqrt(jnp.maximum(om, RSQRT_FLOOR))
        a_dst[pl.ds(r0, GATE_ROWS), cs] = a
        inp_dst[pl.ds(r0, GATE_ROWS), cs] = ((ti + 1.0) * xh.astype(F32)) * root

    return [functools.partial(piece, r0, n)
            for r0 in range(0, rows, GATE_ROWS) for n in range(LRU_BLOCKS)]


def _lru_fwd_kernel(x_ref, g_ref, w_ref, cw_ref, cb_ref, ga_ref, gx_ref, ba_ref, bx_ref,
                    lam_ref, xh_ref, hf_ref, gb_ref, ext_scr, xh_scr,
                    a_scr, inp_scr, carry_scr, w_scr, wg_scr, hn_scr):
    i = pl.program_id(0)
    nt = pl.num_programs(0) - 2
    nb, t, _ = x_ref.shape
    rows, width = xh_ref.shape
    halo = (CONV_WIDTH - 1) * nb
    nslab = width // LANES

    @pl.when(i == 0)
    def _():
        carry_scr[...] = jnp.zeros_like(carry_scr)
        ext_scr[...] = jnp.zeros_like(ext_scr)
        w_scr[...] = w_ref[...].astype(BF16)
        _fill_gate_weights(wg_scr, ga_ref, gx_ref)

    cwh = 0.5 * cw_ref[...]
    cbh = 0.5 * cb_ref[...]
    for l in range(nslab):
        ls = slice(l * LANES, (l + 1) * LANES)
        acc = cbh[:, ls] + cwh[0:1, ls] * ext_scr[l, pl.ds(0, rows), :]
        for k in range(1, CONV_WIDTH):
            acc = acc + cwh[k:k + 1, ls] * ext_scr[l, pl.ds(k * nb, rows), :]
        xh_scr[:, ls] = acc
    xh_ref[...] = xh_scr[...].astype(xh_ref.dtype)

    ext_scr[:, pl.ds(0, halo), :] = ext_scr[:, pl.ds(rows, halo), :]
    gate_pieces = _lru_gate_pieces(xh_scr, wg_scr, ba_ref, bx_ref, lam_ref,
                                   a_scr, inp_scr)
    nbs = PROJ_ROWS // t
    n_dots = (nb // nbs) * (2 * width // PROJ_COLS)
    per_dot = -(-len(gate_pieces) // n_dots)
    for b0 in range(0, nb, nbs):
        hn = hn_scr.at[b0 // nbs]
        for bb in range(nbs):
            hn[pl.ds(bb * t, t), :] = _rms(x_ref[b0 + bb], g_ref[...]).astype(BF16)
        for c0 in range(0, 2 * width, PROJ_COLS):
            u = jnp.dot(hn[...], w_scr[:, c0:c0 + PROJ_COLS], preferred_element_type=F32)
            for bb in range(nbs):
                ub = u[bb * t:(bb + 1) * t]
                if c0 < width:
                    for l in range(PROJ_COLS // LANES):
                        ext_scr[c0 // LANES + l, pl.ds(halo + b0 + bb, t, stride=nb), :] = (
                            ub[:, l * LANES:(l + 1) * LANES])
                else:
                    gb_ref[b0 + bb, :, c0 - width:c0 - width + PROJ_COLS] = (
                        ub.astype(gb_ref.dtype))
            for _ in range(per_dot):
                if gate_pieces:
                    gate_pieces.pop(0)()
    for piece in gate_pieces:
        piece()

    h = jnp.where(i > 1, carry_scr[...], 0.0)
    for s in range(rows // nb):
        h = a_scr[pl.ds(s * nb, nb), :] * h + inp_scr[pl.ds(s * nb, nb), :]
        if s == 0:
            h = jnp.where(i > 1, h, 0.0)
        inp_scr[pl.ds(s * nb, nb), :] = h
    carry_scr[...] = h
    hf_ref[...] = inp_scr[...].astype(hf_ref.dtype)

    @pl.when(i == nt)
    def _():
        ext_scr[:, pl.ds(halo, nb), :] = jnp.zeros((nslab, nb, LANES), F32)

    @pl.when(i == nt + 1)
    def _():
        xh_ref[pl.ds(2 * nb, rows - 2 * nb), :] = jnp.zeros(
            (rows - 2 * nb, width), xh_ref.dtype)
        xh_ref[pl.ds(0, 2 * nb), :] = jnp.where(
            lax.broadcasted_iota(jnp.int32, (2 * nb, width), 0) < nb,
            xh_ref[pl.ds(0, 2 * nb), :], jnp.zeros((), xh_ref.dtype))


def _lru_fwd(x, g, w_in, j, conv_w, conv_b, ga_w, gx_w, ba, bx, lam):
    nb, seq, d = x.shape
    width = w_in.shape[2] // 2
    t = LRU_T
    r = nb * t
    nt = seq // t
    halo = (CONV_WIDTH - 1) * nb
    assert CONV_WIDTH - 1 - CONV_PAD_LEFT == 1
    tile = lambda i: (0, jnp.minimum(i, nt - 1), 0)
    win = lambda i: (jnp.maximum(i - 1, 0), 0)
    return pl.pallas_call(
        _lru_fwd_kernel,
        grid=(nt + 2,),
        in_specs=[pl.BlockSpec((nb, t, d), tile), _const_spec((1, d)),
                  _layer_spec(w_in.shape, j), _const_spec(conv_w.shape),
                  _const_spec(conv_b.shape), _gate_spec(ga_w.shape, j, 0),
                  _gate_spec(gx_w.shape, j, 0),
                  _const_spec(ba.shape), _const_spec(bx.shape), _const_spec(lam.shape)],
        out_specs=[pl.BlockSpec((r, width), win), pl.BlockSpec((r, width), win),
                   pl.BlockSpec((nb, t, width), tile)],
        out_shape=[jax.ShapeDtypeStruct(((nt + 1) * r, width), BF16),
                   jax.ShapeDtypeStruct(((nt + 1) * r, width), BF16),
                   jax.ShapeDtypeStruct((nb, seq, width), BF16)],
        scratch_shapes=[pltpu.VMEM((width // LANES, halo + r, LANES), F32),
                        pltpu.VMEM((r, width), F32),
                        pltpu.VMEM((r, width), F32),
                        pltpu.VMEM((r, width), F32),
                        pltpu.VMEM((nb, width), F32),
                        pltpu.VMEM(w_in.shape[1:], BF16),
                        pltpu.VMEM(_gate_scratch_shape(ga_w.shape), BF16),
                        pltpu.VMEM((r // PROJ_ROWS, PROJ_ROWS, d), BF16)],
        compiler_params=_cparams("arbitrary"),
        name="lru_fwd",
    )(x, g, w_in, conv_w, conv_b, ga_w, gx_w, ba, bx, lam)


def _lru_bwd_kernel(xh_ref, hf_ref, gb_ref, x_ref, ga_ref, gx_ref, ba_ref, bx_ref,
                    lam_ref, wo_ref, o_ref, a_scr, inp_scr, hb_scr, carry_scr,
                    ysum_scr, ybm_scr, wo_scr, wg_scr):
    i = pl.program_id(0)
    nb, t, _ = x_ref.shape
    rows, width = xh_ref.shape
    nsteps = rows // nb
    nslab = width // LANES

    @pl.when(i == 0)
    def _():
        carry_scr[...] = jnp.zeros_like(carry_scr)
        hb_scr[...] = jnp.zeros_like(hb_scr)
        ysum_scr[:, pl.ds(rows, nb), :] = jnp.zeros((nslab, nb, LANES), F32)
        wo_scr[...] = wo_ref[...].astype(BF16)
        _fill_gate_weights(wg_scr, ga_ref, gx_ref)

    for l in range(nslab):
        ls = slice(l * LANES, (l + 1) * LANES)
        ysum_scr[l, pl.ds(0, rows), :] = hb_scr[:, ls] + hf_ref[:, ls].astype(F32)

    for b in range(nb):
        for l in range(nslab):
            ls = slice(l * LANES, (l + 1) * LANES)
            yb = ysum_scr[l, pl.ds(nb + b, t, stride=nb), :]
            ybm_scr[pl.ds(b * t, t), ls] = (
                yb * _gelu_tanh(gb_ref[b, :, ls].astype(F32))).astype(BF16)
    ysum_scr[:, pl.ds(rows, nb), :] = ysum_scr[:, pl.ds(0, nb), :]

    gate_pieces = _lru_gate_pieces(xh_ref, wg_scr, ba_ref, bx_ref, lam_ref,
                                   a_scr, inp_scr)
    gate_pieces = [p for r0 in range(rows - GATE_ROWS, -1, -GATE_ROWS)
                   for p in gate_pieces[(r0 // GATE_ROWS) * LRU_BLOCKS:
                                        (r0 // GATE_ROWS + 1) * LRU_BLOCKS]]
    d_out = wo_ref.shape[1]
    per_dot = -(-len(gate_pieces) * OUT_COLS // d_out)
    for c0 in range(0, d_out, OUT_COLS):
        cs = slice(c0, c0 + OUT_COLS)
        res = jnp.dot(ybm_scr[...], wo_scr[:, cs], preferred_element_type=F32)
        for b in range(nb):
            o_ref[b, :, cs] = x_ref[b, :, cs] + res[b * t:(b + 1) * t]
        for _ in range(per_dot):
            if gate_pieces:
                gate_pieces.pop(0)()
    for piece in gate_pieces:
        piece()

    h = carry_scr[...]
    for s in range(nsteps - 1, -1, -1):
        h = a_scr[pl.ds(s * nb, nb), :] * h + inp_scr[pl.ds(s * nb, nb), :]
        hb_scr[pl.ds(s * nb, nb), :] = h
    carry_scr[...] = h


def _lru_bwd(xh, hf, gb, x, ga_w, gx_w, ba, bx, lam, w_out, j):
    nb, seq, d = x.shape
    width = xh.shape[1]
    t = LRU_T
    r = nb * t
    nt = seq // t
    win_g = lambda i: (jnp.maximum(nt - i, 0), 0)
    win_s = lambda i: (jnp.minimum(nt - i + 1, nt), 0)
    tile = lambda i: (0, jnp.minimum(nt - i + 1, nt - 1), 0)
    return pl.pallas_call(
        _lru_bwd_kernel,
        grid=(nt + 2,),
        in_specs=[pl.BlockSpec((r, width), win_g), pl.BlockSpec((r, width), win_s),
                  pl.BlockSpec((nb, t, width), tile), pl.BlockSpec((nb, t, d), tile),
                  _gate_spec(ga_w.shape, j, 1), _gate_spec(gx_w.shape, j, 1),
                  _const_spec(ba.shape), _const_spec(bx.shape),
                  _const_spec(lam.shape), _layer_spec(w_out.shape, j)],
        out_specs=pl.BlockSpec((nb, t, d), tile),
        out_shape=jax.ShapeDtypeStruct((nb, seq, d), F32),
        scratch_shapes=[pltpu.VMEM((r, width), F32),
                        pltpu.VMEM((r, width), F32),
                        pltpu.VMEM((r, width), F32),
                        pltpu.VMEM((nb, width), F32),
                        pltpu.VMEM((width // LANES, r + nb, LANES), F32),
                        pltpu.VMEM((r, width), BF16),
                        pltpu.VMEM(w_out.shape[1:], BF16),
                        pltpu.VMEM(_gate_scratch_shape(ga_w.shape), BF16)],
        compiler_params=_cparams("arbitrary"),
        name="lru_bwd",
    )(xh, hf, gb, x, ga_w, gx_w, ba, bx, lam, w_out)


def _ffn_chunks(hidden):
    chunks, c = [], 0
    while c < hidden:
        size = 512 if hidden - c >= 512 else hidden - c
        chunks.append((c, size))
        c += size
    return chunks


def _ffn_kernel(*refs, final_norm, fuse_proj):
    refs = list(refs)
    h_ref = refs.pop(0)
    if fuse_proj:
        y_ref, wo_ref = refs.pop(0), refs.pop(0)
    g_ref, wg_ref, wu_ref, wd_ref = (refs.pop(0) for _ in range(4))
    gf_ref = refs.pop(0) if final_norm else None
    o_ref, hn_scr, act_scr = refs
    if fuse_proj:
        res_ref = o_ref
        res_ref[...] = h_ref[...] + jnp.dot(y_ref[...], wo_ref[...],
                                            preferred_element_type=F32)
    else:
        res_ref = h_ref
    hn_scr[...] = _rms(res_ref[...], g_ref[...]).astype(BF16)
    for c, size in _ffn_chunks(wg_ref.shape[1]):
        gt = jnp.dot(hn_scr[...], wg_ref[:, c:c + size], preferred_element_type=F32)
        up = jnp.dot(hn_scr[...], wu_ref[:, c:c + size], preferred_element_type=F32)
        act_scr[:, c:c + size] = (gt * _sigmoid(gt) * up).astype(BF16)
    y = res_ref[...] + jnp.dot(act_scr[...], wd_ref[...], preferred_element_type=F32)
    if final_norm:
        y = _rms(y, gf_ref[...])
    o_ref[...] = y


def _ffn(h, g, w_gate, w_up, w_down, layer, g_final=None, proj=None):
    rows, d = h.shape
    hidden = w_gate.shape[2]
    final_norm = g_final is not None
    fuse_proj = proj is not None
    r = ROW_TILE
    row = lambda i: (i, 0)
    in_specs = [pl.BlockSpec((r, d), row)]
    args = [h]
    scratch = [pltpu.VMEM((r, d), BF16), pltpu.VMEM((r, hidden), BF16)]
    if fuse_proj:
        y, w_out, j = proj
        in_specs += [pl.BlockSpec((r, y.shape[1]), row), _layer_spec(w_out.shape, j)]
        args += [y, w_out]
    in_specs += [_const_spec((1, d)), _layer_spec(w_gate.shape, layer),
                 _layer_spec(w_up.shape, layer), _layer_spec(w_down.shape, layer)]
    args += [g, w_gate, w_up, w_down]
    if final_norm:
        in_specs.append(_const_spec((1, d)))
        args.append(g_final)
    return pl.pallas_call(
        functools.partial(_ffn_kernel, final_norm=final_norm, fuse_proj=fuse_proj),
        grid=(rows // r,),
        in_specs=in_specs,
        out_specs=pl.BlockSpec((r, d), row),
        out_shape=jax.ShapeDtypeStruct((rows, d), F32),
        scratch_shapes=scratch,
        compiler_params=_cparams("parallel"),
        name="ffn" + ("_proj" if fuse_proj else "") + ("_final" if final_norm else ""),
    )(*args)


def _ret_in_kernel(h_ref, g_ref, w_ref, cos_ref, sin_ref, q_ref, k_ref, v_ref,
                   sg_ref, hn_scr, *, heads):
    qk = q_ref.shape[1]
    vd = v_ref.shape[1]
    hd = qk // heads
    half = hd // 2
    hn_scr[...] = _rms(h_ref[...], g_ref[...]).astype(BF16)
    cos = cos_ref[...]
    sin = sin_ref[...]
    for h in range(heads):
        for which, dst in ((0, q_ref), (1, k_ref)):
            c0 = which * qk + h * hd
            u = jnp.dot(hn_scr[...], w_ref[:, c0:c0 + hd], preferred_element_type=F32)
            x1, x2 = u[:, :half], u[:, half:]
            r1 = x1 * cos - x2 * sin
            r2 = x2 * cos + x1 * sin
            dst[:, h * hd:h * hd + half] = r1.astype(dst.dtype)
            dst[:, h * hd + half:(h + 1) * hd] = r2.astype(dst.dtype)
    step = 512
    for c in range(0, vd, step):
        v_ref[:, c:c + step] = jnp.dot(
            hn_scr[...], w_ref[:, 2 * qk + c:2 * qk + c + step],
            preferred_element_type=F32).astype(v_ref.dtype)
    for c in range(0, vd, step):
        gt = jnp.dot(hn_scr[...], w_ref[:, 2 * qk + vd + c:2 * qk + vd + c + step],
                     preferred_element_type=F32)
        sg_ref[:, c:c + step] = (gt * _sigmoid(gt)).astype(sg_ref.dtype)


def _ret_in(h, g, w_in, j, cos, sin, seq, heads, qk, vd):
    rows, d = h.shape
    r = ROW_TILE
    half = cos.shape[1]
    per_seq = seq // r
    pos = lambda i: (i % per_seq, 0)
    row = lambda i: (i, 0)
    return pl.pallas_call(
        functools.partial(_ret_in_kernel, heads=heads),
        grid=(rows // r,),
        in_specs=[pl.BlockSpec((r, d), row), _const_spec((1, d)),
                  _layer_spec(w_in.shape, j),
                  pl.BlockSpec((r, half), pos), pl.BlockSpec((r, half), pos)],
        out_specs=[pl.BlockSpec((r, qk), row), pl.BlockSpec((r, qk), row),
                   pl.BlockSpec((r, vd), row), pl.BlockSpec((r, vd), row)],
        out_shape=[jax.ShapeDtypeStruct((rows, qk), BF16),
                   jax.ShapeDtypeStruct((rows, qk), BF16),
                   jax.ShapeDtypeStruct((rows, vd), BF16),
                   jax.ShapeDtypeStruct((rows, vd), BF16)],
        scratch_shapes=[pltpu.VMEM((r, d), BF16)],
        compiler_params=_cparams("parallel"),
        name="ret_in",
    )(h, g, w_in, cos, sin)


def _lane_tiled(t, n):
    return jnp.concatenate([t] * (n // LANES), axis=1)


def _ret_core_kernel(q_ref, k_ref, v_ref, sg_ref, dec_ref, zf_ref, zb_ref, xf_ref,
                     xb_ref, gc_ref, o_ref, sf_scr, sb_scr, stf_scr, stb_scr, y_scr):
    seq, dk = q_ref.shape
    dv = v_ref.shape[1]
    c = dec_ref.shape[0]
    nc = seq // c
    gcf = gc_ref[pl.ds(0, 1), :]
    gcb = gc_ref[pl.ds(1, 1), :]

    def kv_update(state, n, z_ref, gcd):
        rs = pl.ds(n * c, c)
        kz = (k_ref[rs, :].astype(F32) * _lane_tiled(z_ref[...], dk)).astype(BF16)
        upd = lax.dot_general(kz, v_ref[rs, :], (((0,), (0,)), ((), ())),
                              preferred_element_type=F32)
        return state * gcd + upd

    stf_scr[...] = jnp.zeros_like(stf_scr)
    stb_scr[...] = jnp.zeros_like(stb_scr)
    sf_scr[0] = jnp.zeros((dk, dv), BF16)
    sb_scr[nc - 1] = jnp.zeros((dk, dv), BF16)
    for s in range(nc - 1):
        stf_scr[...] = kv_update(stf_scr[...], s, zf_ref, gcf)
        sf_scr[s + 1] = stf_scr[...].astype(BF16)
        stb_scr[...] = kv_update(stb_scr[...], nc - 1 - s, zb_ref, gcb)
        sb_scr[nc - 2 - s] = stb_scr[...].astype(BF16)

    def finish(n):
        rs = pl.ds(n * c, c)
        y = y_scr[n % 2]
        y = y * lax.rsqrt(jnp.mean(y * y, axis=-1, keepdims=True) + NORM_EPS)
        o_ref[rs, :] = (y * sg_ref[rs, :].astype(F32)).astype(o_ref.dtype)

    for n in range(nc):
        rs = pl.ds(n * c, c)
        q = q_ref[rs, :]
        scores = lax.dot_general(q, k_ref[rs, :], (((1,), (1,)), ((), ())),
                                 preferred_element_type=F32)
        qf32 = q.astype(F32)
        qf = (qf32 * _lane_tiled(xf_ref[...], dk)).astype(BF16)
        qb = (qf32 * _lane_tiled(xb_ref[...], dk)).astype(BF16)
        p = (scores * dec_ref[...]).astype(BF16)
        lhs = jnp.concatenate([qf, qb, p], axis=1)
        rhs = jnp.concatenate([sf_scr[n], sb_scr[n], v_ref[rs, :]], axis=0)
        y_scr[n % 2] = jnp.dot(lhs, rhs, preferred_element_type=F32)
        if n > 0:
            finish(n - 1)
    finish(nc - 1)


def _ret_core(q, k, v, sg, tabs, heads):
    b, seq, qk = q.shape
    vd = v.shape[2]
    dk, dv = qk // heads, vd // heads
    c = RET_CHUNK
    dec, zf, zb, xf, xb, gc = tabs
    bh = lambda i, j: (i, 0, j)
    hh = lambda i, j: (j, 0, 0)
    return pl.pallas_call(
        _ret_core_kernel,
        grid=(b, heads),
        in_specs=[pl.BlockSpec((None, seq, dk), bh), pl.BlockSpec((None, seq, dk), bh),
                  pl.BlockSpec((None, seq, dv), bh), pl.BlockSpec((None, seq, dv), bh),
                  pl.BlockSpec((None, c, c), hh),
                  pl.BlockSpec((None, c, LANES), hh), pl.BlockSpec((None, c, LANES), hh),
                  pl.BlockSpec((None, c, LANES), hh), pl.BlockSpec((None, c, LANES), hh),
                  pl.BlockSpec((None, SUBLANES, dv), hh)],
        out_specs=pl.BlockSpec((None, seq, dv), bh),
        out_shape=jax.ShapeDtypeStruct((b, seq, vd), BF16),
        scratch_shapes=[pltpu.VMEM((seq // c, dk, dv), BF16),
                        pltpu.VMEM((seq // c, dk, dv), BF16),
                        pltpu.VMEM((dk, dv), F32),
                        pltpu.VMEM((dk, dv), F32),
                        pltpu.VMEM((2, c, dv), F32)],
        compiler_params=_cparams("parallel", "parallel"),
        name="ret_core",
    )(q, k, v, sg, dec, zf, zb, xf, xb, gc)


def _ret_tables(heads, dk, dv):
    c = RET_CHUNK
    log_gf = np.log1p(-np.exp2(-5.0 - np.arange(heads, dtype=np.float64)))
    log_gb = log_gf[::-1]
    pos = np.arange(c, dtype=np.float64)
    diff = pos[:, None] - pos[None, :]
    dec_f = np.where(diff >= 0, np.exp(np.maximum(diff, 0.0)[None] * log_gf[:, None, None]), 0.0)
    dec_b = np.where(diff < 0, np.exp(np.maximum(-diff, 0.0)[None] * log_gb[:, None, None]), 0.0)
    k_scale = float(dk) ** -0.5
    dec = (dec_f + dec_b) * k_scale
    zeta_f = np.exp((c - 1.0 - pos)[None, :] * log_gf[:, None])
    xi_f = np.exp((pos + 1.0)[None, :] * log_gf[:, None]) * k_scale
    zeta_b = np.exp(pos[None, :] * log_gb[:, None])
    xi_b = np.exp((c - pos)[None, :] * log_gb[:, None]) * k_scale
    wide = lambda t: np.broadcast_to(t[:, :, None], (heads, c, LANES))
    gc = np.zeros((heads, SUBLANES, dv))
    gc[:, 0, :] = np.exp(c * log_gf)[:, None]
    gc[:, 1, :] = np.exp(c * log_gb)[:, None]
    return tuple(jnp.asarray(np.ascontiguousarray(t), dtype=F32) for t in
                 (dec, wide(zeta_f), wide(zeta_b), wide(xi_f), wide(xi_b), gc))


def _rope_tables(seq, hd):
    half = hd // 2
    inv_freq = 1.0 / (ROPE_BASE ** np.linspace(0.0, 1.0, half, dtype=np.float64))
    ang = np.arange(seq, dtype=np.float64)[:, None] * inv_freq[None, :]
    return jnp.asarray(np.cos(ang), dtype=F32), jnp.asarray(np.sin(ang), dtype=F32)


def kernel(x, ln_mix, ln_ffn, ln_final, lru_w_in, lru_conv_w, lru_conv_b,
           lru_gate_a_w, lru_gate_a_b, lru_gate_x_w, lru_gate_x_b, lru_lambda,
           lru_w_out, ret_w_in, ret_w_out, ffn_w_gate, ffn_w_up, ffn_w_down):
    b, seq, d = x.shape
    depth = ln_mix.shape[0]
    rows = b * seq
    assert b == SUBLANES and seq % ROW_TILE == 0 and seq % LRU_T == 0
    row = lambda v: v.reshape(1, -1)

    ret_w_in, ret_w_out, ffn_w_gate, ffn_w_up, ffn_w_down = (
        w.astype(BF16) for w in (ret_w_in, ret_w_out, ffn_w_gate, ffn_w_up, ffn_w_down))

    h = x.reshape(rows, d)
    for layer in range(depth):
        j = layer // 2
        g_final = row(ln_final) if layer == depth - 1 else None
        proj = None
        if layer % 2 == 0:
            x3 = h.reshape(b, seq, d)
            xh, hf, gb = _lru_fwd(
                x3, row(ln_mix[layer]), lru_w_in, j, lru_conv_w[j],
                row(lru_conv_b[j]), lru_gate_a_w, lru_gate_x_w,
                row(lru_gate_a_b[j, 0]), row(lru_gate_x_b[j, 0]),
                row(lru_lambda[j, 0]))
            h = _lru_bwd(xh, hf, gb, x3, lru_gate_a_w, lru_gate_x_w,
                         row(lru_gate_a_b[j, 1]), row(lru_gate_x_b[j, 1]),
                         row(lru_lambda[j, 1]), lru_w_out, j).reshape(rows, d)
        else:
            heads = RET_HEADS
            vd = ret_w_out.shape[1]
            qk = (ret_w_in.shape[2] - 2 * vd) // 2
            cos, sin = _rope_tables(seq, qk // heads)
            q, k, v, sg = _ret_in(h, row(ln_mix[layer]), ret_w_in, j,
                                  cos, sin, seq, heads, qk, vd)
            tabs = _ret_tables(heads, qk // heads, vd // heads)
            yg = _ret_core(q.reshape(b, seq, qk), k.reshape(b, seq, qk),
                           v.reshape(b, seq, vd), sg.reshape(b, seq, vd), tabs, heads)
            proj = (yg.reshape(rows, vd), ret_w_out, j)
        h = _ffn(h, row(ln_ffn[layer]), ffn_w_gate, ffn_w_up, ffn_w_down, layer,
                 g_final, proj)
    return h.reshape(b, seq, d)
```
